```python
import math
import jax, jax.numpy as jnp
from jax import lax
import numpy as np

D_MODEL = 2048
BATCH = 4
SEQ = 2048
DEPTH = 4
DEC_BATCH = 8
DEC_SEQ = 4
PAST_LEN = 16384
PAGE_SIZE = 128

N_BRANCH = 4
MIX_W = D_MODEL // N_BRANCH
N_HEADS_MIX = 4
HEAD_DIM = MIX_W // N_HEADS_MIX
MLSTM_CHUNK = 64
SB_BLOCK = 128
SB_BIAS_INIT = -6.0
S5_GROUP_CH = 16
S5_GROUPS = MIX_W // S5_GROUP_CH
S5_STATE = 64
CONV_K = 3
FF_DIM = ((8 * D_MODEL // 3 + 255) // 256) * 256
IN_COLS = 11 * MIX_W + 2 * N_HEADS_MIX
EPS = 1e-6

kernel_name = "hybrid_mlstm_stickbreak_s5_shortconv_decoder_step"


def _col_splits():
    sizes = [MIX_W] * 4 + [N_HEADS_MIX] * 2 + [MIX_W] * 3 + [MIX_W] + [MIX_W] * 3
    return [int(s) for s in np.cumsum(sizes)[:-1]]


def rmsnorm(x, g=None):
    xf = x.astype(jnp.float32)
    y = xf * lax.rsqrt(jnp.mean(xf * xf, axis=-1, keepdims=True) + EPS)
    if g is not None:
        y = y * g.astype(jnp.float32)
    return y.astype(x.dtype)


def _mlstm_chunk(carry, xs):
    C, n, m = carry
    q, k, v, ig, lf = xs
    L = q.shape[2]
    b = jnp.cumsum(lf, axis=-1)
    causal = jnp.tril(jnp.ones((L, L), dtype=bool))
    d_intra = jnp.where(causal, b[..., :, None] - b[..., None, :] + ig[..., None, :], -jnp.inf)
    m_inter = b + m[..., None]
    m_t = jnp.maximum(m_inter, d_intra.max(-1))
    w_intra = jnp.exp(d_intra - m_t[..., None])
    w_inter = jnp.exp(m_inter - m_t)
    s = w_intra * jnp.einsum('bhtk,bhsk->bhts', q, k)
    num = w_inter[..., None] * jnp.einsum('bhvk,bhtk->bhtv', C, q) + jnp.einsum('bhts,bhsv->bhtv', s, v)
    den = w_inter * jnp.einsum('bhk,bhtk->bht', n, q) + s.sum(-1)
    h = num / jnp.maximum(jnp.abs(den), jnp.exp(-m_t))[..., None]
    b_end = b[..., -1]
    log_w = b_end[..., None] - b + ig
    m_end = jnp.maximum(b_end + m, log_w.max(-1))
    w_k = jnp.exp(log_w - m_end[..., None])
    decay = jnp.exp(b_end + m - m_end)
    C_new = decay[..., None, None] * C + jnp.einsum('bhsv,bhsk->bhvk', w_k[..., None] * v, k)
    n_new = decay[..., None] * n + jnp.einsum('bhs,bhsk->bhk', w_k, k)
    return (C_new, n_new, m_end), h


def mlstm(q, k, v, ig, f_pre, C0, n0, m0, chunk):
    B_, T_, H_, d_ = q.shape
    nc = T_ // chunk
    f32 = jnp.float32

    def blocks(a):
        a = a.astype(f32).reshape((B_, nc, chunk) + a.shape[2:])
        return a.transpose((1, 0, 3, 2) + tuple(range(4, a.ndim)))

    lf = jax.nn.log_sigmoid(f_pre.astype(f32))
    xs = (blocks(q), blocks(k * (d_ ** -0.5)), blocks(v), blocks(ig), blocks(lf))
    carry0 = (C0.astype(f32), n0.astype(f32), m0.astype(f32))
    (C_n, n_n, m_n), h = lax.scan(_mlstm_chunk, carry0, xs)
    h = h.transpose(1, 0, 3, 2, 4).reshape(B_, T_, H_, d_)
    return h, C_n, n_n, m_n


def stick_breaking(q, k, v, bias, q_pos, k_pos):
    d_ = q.shape[-1]
    z = (jnp.einsum('bqhd,bkhd->bhqk', q.astype(jnp.float32), k.astype(jnp.float32)) * (d_ ** -0.5)
         + bias.astype(jnp.float32)[None, :, None, None])
    valid = k_pos[None, :] < q_pos[:, None]
    log1m = jnp.where(valid, jax.nn.log_sigmoid(-z), 0.0)
    excl = lax.cumsum(log1m, axis=3, reverse=True) - log1m
    a = jnp.where(valid, jnp.exp(jax.nn.log_sigmoid(z) + excl), 0.0)
    return jnp.einsum('bhqk,bkhd->bqhd', a, v.astype(jnp.float32))


def stick_breaking_prompt(q, k, v, bias):
    B_, T_, H_, d_ = q.shape
    nb = T_ // SB_BLOCK
    qb = q.reshape(B_, nb, SB_BLOCK, H_, d_).transpose(1, 0, 2, 3, 4)
    k_pos = jnp.arange(T_, dtype=jnp.int32)

    def one_block(args):
        q_blk, i = args
        q_pos = i * SB_BLOCK + jnp.arange(SB_BLOCK, dtype=jnp.int32)
        return stick_breaking(q_blk, k, v, bias, q_pos, k_pos)

    out = lax.map(one_block, (qb, jnp.arange(nb, dtype=jnp.int32)))
    return out.transpose(1, 0, 2, 3, 4).reshape(B_, T_, H_, d_)


def _complex_affine_combine(e1, e2):
    a1r, a1i, b1r, b1i = e1
    a2r, a2i, b2r, b2i = e2
    return (a2r * a1r - a2i * a1i,
            a2r * a1i + a2i * a1r,
            a2r * b1r - a2i * b1i + b2r,
            a2r * b1i + a2i * b1r + b2i)


def s5_branch(u, p, h0):
    B_, T_, _ = u.shape
    f32 = jnp.float32
    lam_re = p['s5_lambda_re'].astype(f32)
    lam_im = p['s5_lambda_im'].astype(f32)
    step = jnp.exp(p['s5_log_step'].astype(f32))[:, None]
    decay = jnp.exp(lam_re * step)
    ab_re = decay * jnp.cos(lam_im * step)
    ab_im = decay * jnp.sin(lam_im * step)
    inv = 1.0 / (lam_re * lam_re + lam_im * lam_im)
    f_re = ((ab_re - 1.0) * lam_re + ab_im * lam_im) * inv
    f_im = (ab_im * lam_re - (ab_re - 1.0) * lam_im) * inv
    b_re = p['s5_b_re'].astype(f32)
    b_im = p['s5_b_im'].astype(f32)
    bb_re = f_re[..., None] * b_re - f_im[..., None] * b_im
    bb_im = f_re[..., None] * b_im + f_im[..., None] * b_re
    uf = u.astype(f32)
    ug = uf.reshape(B_, T_, S5_GROUPS, S5_GROUP_CH)
    bu_re = jnp.einsum('gpc,btgc->btgp', bb_re, ug)
    bu_im = jnp.einsum('gpc,btgc->btgp', bb_im, ug)
    a_re = jnp.broadcast_to(ab_re, bu_re.shape)
    a_im = jnp.broadcast_to(ab_im, bu_im.shape)
    A_re, A_im, s_re, s_im = lax.associative_scan(_complex_affine_combine, (a_re, a_im, bu_re, bu_im), axis=1)
    if h0 is not None:
        h_re = h0[0].astype(f32)[:, None]
        h_im = h0[1].astype(f32)[:, None]
        s_re, s_im = (s_re + A_re * h_re - A_im * h_im,
                      s_im + A_re * h_im + A_im * h_re)
    y = (jnp.einsum('gcp,btgp->btgc', p['s5_c_re'].astype(f32), s_re)
         - jnp.einsum('gcp,btgp->btgc', p['s5_c_im'].astype(f32), s_im))
    y = y.reshape(B_, T_, MIX_W) + p['s5_d'].astype(f32) * uf
    y = jax.nn.gelu(y)
    ya, yb = jnp.split(y @ p['w_s5_glu'].astype(f32), 2, axis=-1)
    out = ya * jax.nn.sigmoid(yb)
    return out.astype(u.dtype), s_re[:, -1], s_im[:, -1]


def conv_branch(gb, gc, xv, w, buf):
    z = gc * xv
    T_ = z.shape[1]
    if buf is None:
        zp = jnp.pad(z, ((0, 0), (CONV_K - 1, 0), (0, 0)))
    else:
        zp = jnp.concatenate([buf.astype(z.dtype), z], axis=1)
    y = sum(w[j] * zp[:, j:j + T_] for j in range(CONV_K))
    return gb * y, zp[:, -(CONV_K - 1):]


def hybrid_mixer(h, p, past):
    B_, T_, _ = h.shape
    (a_q, a_k, a_v, a_o, a_i, a_f, b_q, b_k, b_v, s_u, c_b, c_c, c_x) = jnp.split(
        h @ p['w_in'], _col_splits(), axis=-1)
    hd = lambda t: t.reshape(B_, T_, N_HEADS_MIX, HEAD_DIM)
    ig = a_i + p['b_mlstm_i']
    fg = a_f + p['b_mlstm_f']
    kb, vb = hd(b_k), hd(b_v)
    if past is None:
        C0 = jnp.zeros((B_, N_HEADS_MIX, HEAD_DIM, HEAD_DIM), jnp.float32)
        n0 = jnp.zeros((B_, N_HEADS_MIX, HEAD_DIM), jnp.float32)
        m0 = jnp.zeros((B_, N_HEADS_MIX), jnp.float32)
        h_a, mc, mn, mm = mlstm(hd(a_q), hd(a_k), hd(a_v), ig, fg, C0, n0, m0, MLSTM_CHUNK)
        h_b = stick_breaking_prompt(hd(b_q), kb, vb, p['b_sb'])
        y_s, s_re, s_im = s5_branch(s_u, p, None)
        y_c, conv_new = conv_branch(c_b, c_c, c_x, p['conv_w'], None)
    else:
        h_a, mc, mn, mm = mlstm(hd(a_q), hd(a_k), hd(a_v), ig, fg,
                                past['mlstm_c'], past['mlstm_n'], past['mlstm_m'], T_)
        past_len = past['sb_k'].shape[1]
        k_all = jnp.concatenate([past['sb_k'].astype(kb.dtype), kb], axis=1)
        v_all = jnp.concatenate([past['sb_v'].astype(vb.dtype), vb], axis=1)
        q_pos = past_len + jnp.arange(T_, dtype=jnp.int32)
        k_pos = jnp.arange(past_len + T_, dtype=jnp.int32)
        h_b = stick_breaking(hd(b_q), k_all, v_all, p['b_sb'], q_pos, k_pos)
        y_s, s_re, s_im = s5_branch(s_u, p, (past['s5_re'], past['s5_im']))
        y_c, conv_new = conv_branch(c_b, c_c, c_x, p['conv_w'], past['conv'])
    y_a = (rmsnorm(h_a).reshape(B_, T_, MIX_W) * p['g_mlstm_head'].astype(jnp.float32)).astype(h.dtype)
    y_a = y_a * jax.nn.sigmoid(a_o)
    y_b = h_b.reshape(B_, T_, MIX_W).astype(h.dtype)
    y_all = jnp.stack([y_a, y_b, y_s, y_c], axis=2)
    branches = jnp.einsum('btgc,gcd->btgd', y_all, p['w_branch'])
    gates = jax.nn.sigmoid(h @ p['w_gate'] + p['b_gate']).reshape(B_, T_, N_BRANCH, D_MODEL)
    merged = jnp.einsum('btgd,btgd->btd', gates, branches)
    new_state = {'sb_k': kb, 'sb_v': vb, 'mlstm_c': mc, 'mlstm_n': mn, 'mlstm_m': mm,
                 's5_re': s_re, 's5_im': s_im, 'conv': conv_new}
    return merged @ p['w_out'], new_state


def trunk_layer(x, c, p, past):
    mod = jax.nn.silu(c) @ p['w_ada'] + p['b_ada']
    sh1, sc1, g1, sh2, sc2, g2 = jnp.split(mod[:, None, :], 6, axis=-1)
    h = rmsnorm(x, p['g_norm1']) * (1.0 + sc1) + sh1
    mix, new_state = hybrid_mixer(h, p, past)
    x = x + g1 * mix
    h = rmsnorm(x, p['g_norm2']) * (1.0 + sc2) + sh2
    ffn = (jax.nn.silu(h @ p['w_ffn_gate']) * (h @ p['w_ffn_up'])) @ p['w_ffn_down']
    x = x + g2 * ffn
    return x, new_state


def setup_inputs(seed: int = 0) -> dict:
    key = jax.random.key(seed)
    ks = iter(jax.random.split(key, 64))
    f32 = jnp.float32
    nrm = lambda shape, s=1.0: s * jax.random.normal(next(ks), shape, f32)
    L = DEPTH
    n_pages = PAST_LEN // PAGE_SIZE
    n_used = DEC_BATCH * n_pages
    n_phys = n_used + max(1, n_used // 4)
    page_table = jax.random.permutation(next(ks), n_phys)[:n_used].reshape(DEC_BATCH, n_pages).astype(jnp.int32)
    lam_im_base = math.pi * jnp.arange(S5_STATE, dtype=f32)
    log_step = jax.random.uniform(next(ks), (L, S5_GROUPS), f32, minval=math.log(1e-3), maxval=math.log(1e-1))
    return {
        'x_prompt': nrm((BATCH, SEQ, D_MODEL)),
        'x_sample': nrm((DEC_BATCH, DEC_SEQ, D_MODEL)),
        'cache_sb_k': nrm((L, n_phys, PAGE_SIZE, N_HEADS_MIX, HEAD_DIM)),
        'cache_sb_v': nrm((L, n_phys, PAGE_SIZE, N_HEADS_MIX, HEAD_DIM)),
        'state_mlstm_c': nrm((L, DEC_BATCH, N_HEADS_MIX, HEAD_DIM, HEAD_DIM), 0.5),
        'state_mlstm_n': nrm((L, DEC_BATCH, N_HEADS_MIX, HEAD_DIM), 0.5),
        'state_mlstm_m': nrm((L, DEC_BATCH, N_HEADS_MIX)),
        'state_s5_re': nrm((L, DEC_BATCH, S5_GROUPS, S5_STATE), 0.3),
        'state_s5_im': nrm((L, DEC_BATCH, S5_GROUPS, S5_STATE), 0.3),
        'state_conv': nrm((L, DEC_BATCH, CONV_K - 1, MIX_W)),
        'page_table': page_table,
        'c_prompt': nrm((BATCH, D_MODEL)),
        'c_sample': nrm((DEC_BATCH, D_MODEL)),
        'w_ada': nrm((L, D_MODEL, 6 * D_MODEL), 0.5 * D_MODEL ** -0.5),
        'b_ada': nrm((L, 6 * D_MODEL), 0.02),
        'g_norm1': 1.0 + nrm((L, D_MODEL), 0.05),
        'g_norm2': 1.0 + nrm((L, D_MODEL), 0.05),
        'w_in': nrm((L, D_MODEL, IN_COLS), D_MODEL ** -0.5),
        'b_mlstm_i': nrm((L, N_HEADS_MIX), 0.1),
        'b_mlstm_f': jnp.linspace(3.0, 6.0, N_HEADS_MIX, dtype=f32)[None, :] + nrm((L, N_HEADS_MIX), 0.1),
        'g_mlstm_head': 1.0 + nrm((L, MIX_W), 0.05),
        'b_sb': SB_BIAS_INIT + nrm((L, N_HEADS_MIX), 0.1),
        's5_lambda_re': -0.5 + nrm((L, S5_GROUPS, S5_STATE), 0.01),
        's5_lambda_im': lam_im_base[None, None, :] + nrm((L, S5_GROUPS, S5_STATE), 0.01),
        's5_b_re': nrm((L, S5_GROUPS, S5_STATE, S5_GROUP_CH), (2 * S5_GROUP_CH) ** -0.5),
        's5_b_im': nrm((L, S5_GROUPS, S5_STATE, S5_GROUP_CH), (2 * S5_GROUP_CH) ** -0.5),
        's5_c_re': nrm((L, S5_GROUPS, S5_GROUP_CH, S5_STATE), S5_STATE ** -0.5),
        's5_c_im': nrm((L, S5_GROUPS, S5_GROUP_CH, S5_STATE), S5_STATE ** -0.5),
        's5_d': nrm((L, MIX_W)),
        's5_log_step': log_step,
        'w_s5_glu': nrm((L, MIX_W, 2 * MIX_W), MIX_W ** -0.5),
        'conv_w': nrm((L, CONV_K, MIX_W), CONV_K ** -0.5),
        'w_gate': nrm((L, D_MODEL, N_BRANCH * D_MODEL), D_MODEL ** -0.5),
        'b_gate': nrm((L, N_BRANCH * D_MODEL), 0.02),
        'w_branch': nrm((L, N_BRANCH, MIX_W, D_MODEL), MIX_W ** -0.5),
        'w_out': nrm((L, D_MODEL, D_MODEL), D_MODEL ** -0.5),
        'w_ffn_gate': nrm((L, D_MODEL, FF_DIM), D_MODEL ** -0.5),
        'w_ffn_up': nrm((L, D_MODEL, FF_DIM), D_MODEL ** -0.5),
        'w_ffn_down': nrm((L, FF_DIM, D_MODEL), FF_DIM ** -0.5),
        'g_final': 1.0 + nrm((D_MODEL,), 0.05),
    }


def reference(x_prompt, x_sample, cache_sb_k, cache_sb_v, state_mlstm_c, state_mlstm_n, state_mlstm_m,
              state_s5_re, state_s5_im, state_conv, page_table, c_prompt, c_sample,
              w_ada, b_ada, g_norm1, g_norm2, w_in, b_mlstm_i, b_mlstm_f, g_mlstm_head, b_sb,
              s5_lambda_re, s5_lambda_im, s5_b_re, s5_b_im, s5_c_re, s5_c_im, s5_d, s5_log_step,
              w_s5_glu, conv_w, w_gate, b_gate, w_branch, w_out, w_ffn_gate, w_ffn_up, w_ffn_down, g_final):
    dec_b, n_pages = page_table.shape
    page_size = cache_sb_k.shape[2]
    xp, xs = x_prompt, x_sample
    new_p, new_s = [], []
    for l in range(DEPTH):
        p = {'w_ada': w_ada[l], 'b_ada': b_ada[l], 'g_norm1': g_norm1[l], 'g_norm2': g_norm2[l],
             'w_in': w_in[l], 'b_mlstm_i': b_mlstm_i[l], 'b_mlstm_f': b_mlstm_f[l],
             'g_mlstm_head': g_mlstm_head[l], 'b_sb': b_sb[l],
             's5_lambda_re': s5_lambda_re[l], 's5_lambda_im': s5_lambda_im[l],
             's5_b_re': s5_b_re[l], 's5_b_im': s5_b_im[l], 's5_c_re': s5_c_re[l], 's5_c_im': s5_c_im[l],
             's5_d': s5_d[l], 's5_log_step': s5_log_step[l], 'w_s5_glu': w_s5_glu[l], 'conv_w': conv_w[l],
             'w_gate': w_gate[l], 'b_gate': b_gate[l], 'w_branch': w_branch[l], 'w_out': w_out[l],
             'w_ffn_gate': w_ffn_gate[l], 'w_ffn_up': w_ffn_up[l], 'w_ffn_down': w_ffn_down[l]}
        past = {
            'sb_k': cache_sb_k[l][page_table].reshape(dec_b, n_pages * page_size, N_HEADS_MIX, HEAD_DIM),
            'sb_v': cache_sb_v[l][page_table].reshape(dec_b, n_pages * page_size, N_HEADS_MIX, HEAD_DIM),
            'mlstm_c': state_mlstm_c[l], 'mlstm_n': state_mlstm_n[l], 'mlstm_m': state_mlstm_m[l],
            's5_re': state_s5_re[l], 's5_im': state_s5_im[l], 'conv': state_conv[l]}
        xp, st_p = trunk_layer(xp, c_prompt, p, None)
        xs, st_s = trunk_layer(xs, c_sample, p, past)
        new_p.append(st_p)
        new_s.append(st_s)
    y_prompt = rmsnorm(xp, g_final)
    y_sample = rmsnorm(xs, g_final)
    stk = lambda states, name: jnp.stack([s[name] for s in states])
    return (y_prompt, y_sample,
            stk(new_p, 'sb_k'), stk(new_p, 'sb_v'), stk(new_s, 'sb_k'), stk(new_s, 'sb_v'),
            stk(new_p, 'mlstm_c'), stk(new_p, 'mlstm_n'), stk(new_p, 'mlstm_m'),
            stk(new_s, 'mlstm_c'), stk(new_s, 'mlstm_n'), stk(new_s, 'mlstm_m'),
            stk(new_p, 's5_re'), stk(new_p, 's5_im'), stk(new_s, 's5_re'), stk(new_s, 's5_im'),
            stk(new_p, 'conv'), stk(new_s, 'conv'))
```

```python
import functools
import math

import jax
import jax.numpy as jnp
from jax import lax
from jax.experimental import pallas as pl
from jax.experimental.pallas import tpu as pltpu

F32 = jnp.float32
BF16 = jnp.bfloat16

N_HEADS = 4
HEAD_DIM = 128
MIX_W = N_HEADS * HEAD_DIM
N_BRANCH = 4
S5_GROUP_CH = 16
S5_GROUPS = MIX_W // S5_GROUP_CH
S5_STATE = 64
S5_WIDTH = S5_GROUPS * S5_STATE
S5_SUPER = 4
CONV_K = 3
EPS = 1e-6
SUBLANES = 8
GATE_LANES = 128
V7X_VMEM_LIMIT = 56 * 1024 * 1024

COL_AQ, COL_AK, COL_AV, COL_AO, COL_BQ, COL_BK, COL_BV, COL_SU, COL_CB, COL_CC, COL_CX = range(11)
N_PROJ_BLOCKS = 11

NT_DIMS = (((1,), (1,)), ((), ()))
TN_DIMS = (((0,), (0,)), ((), ()))


def _dot(a, b):
    return jnp.dot(a, b, preferred_element_type=F32)


def _dot_nt(a, b):
    return lax.dot_general(a, b, NT_DIMS, preferred_element_type=F32)


def _dot_exact(a, b):
    return jnp.dot(a, b, precision=lax.Precision.HIGHEST, preferred_element_type=F32)


def _softplus_neg_abs(z):
    return jnp.log1p(jnp.exp(-jnp.abs(z)))


def _log_sigmoid(z):
    return jnp.minimum(z, 0.0) - _softplus_neg_abs(z)


def _pad_rows(x, rows):
    if x.shape[0] == rows:
        return x
    return jnp.concatenate([x, jnp.zeros((rows - x.shape[0],) + x.shape[1:], x.dtype)], axis=0)


def _params(sem, vmem=None):
    return pltpu.CompilerParams(dimension_semantics=sem, vmem_limit_bytes=vmem)


def _ada_kernel(c_ref, w_ref, b_ref, o_ref):
    c = c_ref[...]
    a = (c * jax.nn.sigmoid(c)).astype(BF16)
    o_ref[...] = _dot(a, w_ref[...].astype(BF16)) + b_ref[...]


def _ada_all(c, w_ada, b_ada, tn=1024):
    depth, d, n = w_ada.shape
    rows = c.shape[0]
    return pl.pallas_call(
        _ada_kernel,
        out_shape=jax.ShapeDtypeStruct((depth, rows, n), F32),
        grid=(depth, n // tn),
        in_specs=[pl.BlockSpec((rows, d), lambda l, j: (0, 0)),
                  pl.BlockSpec((None, d, tn), lambda l, j: (l, 0, j)),
                  pl.BlockSpec((None, 1, tn), lambda l, j: (l, 0, j))],
        out_specs=pl.BlockSpec((None, rows, tn), lambda l, j: (l, 0, j)),
        compiler_params=_params(("parallel", "parallel"), V7X_VMEM_LIMIT),
        name="ada_mod",
    )(c, w_ada, b_ada.reshape(depth, 1, n))


def _norm_mod_kernel(x_ref, g_ref, sc_ref, sh_ref, o_ref):
    x = x_ref[...]
    y = x * lax.rsqrt(jnp.mean(x * x, axis=-1, keepdims=True) + EPS) * g_ref[...]
    o_ref[...] = (y * (1.0 + sc_ref[...]) + sh_ref[...]).astype(o_ref.dtype)


def _norm_kernel(x_ref, g_ref, o_ref):
    x = x_ref[...]
    o_ref[...] = x * lax.rsqrt(jnp.mean(x * x, axis=-1, keepdims=True) + EPS) * g_ref[...]


def _row_spec(arr, tm, width, col):
    if arr.shape[1] == 1:
        return pl.BlockSpec((None, 1, width), lambda b, i, *r: (b, 0, col(*r)))
    return pl.BlockSpec((None, tm, width), lambda b, i, *r: (b, i, col(*r)))


def _norm_mod(x, g, l, sc, sh, tm=512):
    bsz, t, d = x.shape
    tm = min(tm, t)
    zero = lambda *r: 0
    return pl.pallas_call(
        _norm_mod_kernel,
        out_shape=jax.ShapeDtypeStruct((bsz, t, d), BF16),
        grid=(bsz, t // tm),
        in_specs=[pl.BlockSpec((None, tm, d), lambda b, i: (b, i, 0)),
                  pl.BlockSpec((None, 1, d), lambda b, i: (l, 0, 0)),
                  _row_spec(sc, tm, d, zero), _row_spec(sh, tm, d, zero)],
        out_specs=pl.BlockSpec((None, tm, d), lambda b, i: (b, i, 0)),
        compiler_params=_params(("parallel", "parallel")),
        name="norm_mod",
    )(x, g, sc, sh)


def _final_norm(x, g, tm=512):
    bsz, t, d = x.shape
    tm = min(tm, t)
    return pl.pallas_call(
        _norm_kernel,
        out_shape=jax.ShapeDtypeStruct((bsz, t, d), F32),
        grid=(bsz, t // tm),
        in_specs=[pl.BlockSpec((None, tm, d), lambda b, i: (b, i, 0)),
                  pl.BlockSpec((1, d), lambda b, i: (0, 0))],
        out_specs=pl.BlockSpec((None, tm, d), lambda b, i: (b, i, 0)),
        compiler_params=_params(("parallel", "parallel")),
        name="final_norm",
    )(x, g)


def _mm_kernel(a_ref, w_ref, o_ref):
    o_ref[...] = _dot(a_ref[...], w_ref[...]).astype(o_ref.dtype)


def _linear(a, w, l, out_dtype, tm, tn, name):
    bsz, t, k = a.shape
    n = w.shape[-1]
    tm, tn = min(tm, t), min(tn, n)
    return pl.pallas_call(
        _mm_kernel,
        out_shape=jax.ShapeDtypeStruct((bsz, t, n), out_dtype),
        grid=(bsz, t // tm, n // tn),
        in_specs=[pl.BlockSpec((None, tm, k), lambda b, i, j: (b, i, 0)),
                  pl.BlockSpec((None, k, tn), lambda b, i, j: (l, 0, j))],
        out_specs=pl.BlockSpec((None, tm, tn), lambda b, i, j: (b, i, j)),
        compiler_params=_params(("parallel", "parallel", "parallel"), V7X_VMEM_LIMIT),
        name=name,
    )(a, w)


def _mm_res_kernel(a_ref, w_ref, x_ref, g_ref, o_ref):
    o_ref[...] = x_ref[...] + g_ref[...] * _dot(a_ref[...], w_ref[...])


def _linear_residual(a, w, l, x, g, tm, tn, name):
    bsz, t, k = a.shape
    n = w.shape[-1]
    tm, tn = min(tm, t), min(tn, n)
    return pl.pallas_call(
        _mm_res_kernel,
        out_shape=jax.ShapeDtypeStruct((bsz, t, n), F32),
        grid=(bsz, t // tm, n // tn),
        in_specs=[pl.BlockSpec((None, tm, k), lambda b, i, j: (b, i, 0)),
                  pl.BlockSpec((None, k, tn), lambda b, i, j: (l, 0, j)),
                  pl.BlockSpec((None, tm, tn), lambda b, i, j: (b, i, j)),
                  _row_spec(g, tm, tn, lambda j: j)],
        out_specs=pl.BlockSpec((None, tm, tn), lambda b, i, j: (b, i, j)),
        compiler_params=_params(("parallel", "parallel", "parallel"), V7X_VMEM_LIMIT),
        name=name,
    )(a, w, x, g)


def _ffn_up_kernel(a_ref, wg_ref, wu_ref, o_ref):
    a = a_ref[...]
    gate = _dot(a, wg_ref[...])
    o_ref[...] = (gate * jax.nn.sigmoid(gate) * _dot(a, wu_ref[...])).astype(o_ref.dtype)


def _ffn_up(a, wg, wu, l, tm, tn):
    bsz, t, k = a.shape
    n = wg.shape[-1]
    tm, tn = min(tm, t), min(tn, n)
    wspec = pl.BlockSpec((None, k, tn), lambda b, i, j: (l, 0, j))
    return pl.pallas_call(
        _ffn_up_kernel,
        out_shape=jax.ShapeDtypeStruct((bsz, t, n), BF16),
        grid=(bsz, t // tm, n // tn),
        in_specs=[pl.BlockSpec((None, tm, k), lambda b, i, j: (b, i, 0)), wspec, wspec],
        out_specs=pl.BlockSpec((None, tm, tn), lambda b, i, j: (b, i, j)),
        compiler_params=_params(("parallel", "parallel", "parallel"), V7X_VMEM_LIMIT),
        name="ffn_up",
    )(a, wg, wu)


def _gate_merge_kernel(h_ref, wg_ref, bg_ref, ya_ref, yb_ref, ys_ref, yc_ref, wb_ref, o_ref):
    h = h_ref[...]
    acc = None
    for g, y_ref in enumerate((ya_ref, yb_ref, ys_ref, yc_ref)):
        gate = jax.nn.sigmoid(_dot(h, wg_ref[g]) + bg_ref[g])
        term = gate * _dot(y_ref[...], wb_ref[g])
        acc = term if acc is None else acc + term
    o_ref[...] = acc.astype(o_ref.dtype)


def _gate_merge(h, wg, bg, ys, wb, l, tm, tn):
    bsz, t, k = h.shape
    d = wg.shape[-1]
    tm, tn = min(tm, t), min(tn, d)
    yspec = pl.BlockSpec((None, tm, MIX_W), lambda b, i, j: (b, i, 0))
    return pl.pallas_call(
        _gate_merge_kernel,
        out_shape=jax.ShapeDtypeStruct((bsz, t, d), BF16),
        grid=(bsz, t // tm, d // tn),
        in_specs=[pl.BlockSpec((None, tm, k), lambda b, i, j: (b, i, 0)),
                  pl.BlockSpec((None, N_BRANCH, k, tn), lambda b, i, j: (l, 0, 0, j)),
                  pl.BlockSpec((None, N_BRANCH, 1, tn), lambda b, i, j: (l, 0, 0, j)),
                  yspec, yspec, yspec, yspec,
                  pl.BlockSpec((None, N_BRANCH, MIX_W, tn), lambda b, i, j: (l, 0, 0, j))],
        out_specs=pl.BlockSpec((None, tm, tn), lambda b, i, j: (b, i, j)),
        compiler_params=_params(("parallel", "parallel", "parallel"), V7X_VMEM_LIMIT),
        name="gate_merge",
    )(h, wg, bg, *ys, wb)


def _mlstm_kernel(q_ref, k_ref, v_ref, og_ref, gc_ref, gr_ref, bc_ref, br_ref, gh_ref, c0_ref, n0_ref, m0_ref,
                  y_ref, c_out, n_out, m_out, c_scr, n_scr, m_scr, *, chunk, valid):
    ci = pl.program_id(1)
    rows = q_ref.shape[0]

    @pl.when(ci == 0)
    def _():
        c_scr[...] = c0_ref[...]
        n_scr[...] = n0_ref[...]
        m_scr[...] = m0_ref[...]

    pos_c = lax.broadcasted_iota(jnp.int32, (chunk, 1), 0)
    pos_r = lax.broadcasted_iota(jnp.int32, (1, chunk), 1)
    tri_r = lax.broadcasted_iota(jnp.int32, (chunk, chunk), 0)
    tri_c = lax.broadcasted_iota(jnp.int32, (chunk, chunk), 1)
    causal = tri_c <= tri_r
    lower = causal.astype(F32)
    upper = (tri_r <= tri_c).astype(F32)

    gates_c = _pad_rows(gc_ref[...], chunk) + bc_ref[...]
    gates_r = gr_ref[...] + br_ref[...]
    lf_c = _log_sigmoid(gates_c)
    lf_r = _log_sigmoid(gates_r)
    if valid < chunk:
        lf_c = jnp.where(pos_c < valid, lf_c, 0.0)
        lf_r = jnp.where(pos_r < valid, lf_r, 0.0)
        gates_c = jnp.where(pos_c < valid, gates_c, -jnp.inf)
        gates_r = jnp.where(pos_r < valid, gates_r, -jnp.inf)
    cum_c = _dot_exact(lower, lf_c)
    cum_r = _dot_exact(lf_r, upper)

    for h in range(N_HEADS):
        sl = slice(h * HEAD_DIM, (h + 1) * HEAD_DIM)
        q = _pad_rows(q_ref[:, sl], chunk)
        k = _pad_rows(k_ref[:, sl], chunk) * (HEAD_DIM ** -0.5)
        v = _pad_rows(v_ref[:, sl], chunk)
        qb, kb, vb = q.astype(BF16), k.astype(BF16), v.astype(BF16)
        b_c = cum_c[:, N_HEADS + h:N_HEADS + h + 1]
        b_r = cum_r[N_HEADS + h:N_HEADS + h + 1, :]
        ig_c = gates_c[:, h:h + 1]
        ig_r = gates_r[h:h + 1, :]
        m_prev = m_scr[h]
        c_prev = c_scr[h]
        n_prev = n_scr[h]

        d_intra = jnp.where(causal, b_c - b_r + ig_r, -jnp.inf)
        m_inter = b_c + m_prev
        m_t = jnp.maximum(m_inter, jnp.max(d_intra, axis=1, keepdims=True))
        w_intra = jnp.exp(d_intra - m_t)
        w_inter = jnp.exp(m_inter - m_t)
        s = w_intra * _dot_nt(qb, kb)
        num = w_inter * _dot_nt(qb, c_prev.astype(BF16)) + _dot(s.astype(BF16), vb)
        den = w_inter * jnp.sum(q * n_prev, axis=1, keepdims=True) + jnp.sum(s, axis=1, keepdims=True)
        hh = num / jnp.maximum(jnp.abs(den), jnp.exp(-m_t))

        hn = hh * lax.rsqrt(jnp.mean(hh * hh, axis=-1, keepdims=True) + EPS) * gh_ref[:, sl]
        y = hn[:rows] * jax.nn.sigmoid(og_ref[:, sl])
        y_ref[:, sl] = y.astype(y_ref.dtype)

        b_end = b_c[chunk - 1:chunk, :]
        log_w = b_end - b_c + ig_c
        m_end = jnp.maximum(b_end + m_prev, jnp.max(log_w, axis=0, keepdims=True))
        w_k = jnp.exp(log_w - m_end)
        decay = jnp.exp(b_end + m_prev - m_end)
        wv = (w_k * v).astype(BF16)
        c_scr[h] = decay * c_prev + lax.dot_general(wv, kb, TN_DIMS, preferred_element_type=F32)
        n_scr[h] = decay * n_prev + jnp.sum(w_k * k, axis=0, keepdims=True)
        m_scr[h] = m_end

    @pl.when(ci == pl.num_programs(1) - 1)
    def _():
        c_out[...] = c_scr[...]
        n_out[...] = n_scr[...]
        m_out[...] = m_scr[...]


def _mlstm(proj, gates, gates_t, bias_c, bias_r, ghead, l, c0, n0, m0, chunk, rows, valid):
    bsz, t, _ = proj.shape
    nc = t // rows
    col = lambda c: pl.BlockSpec((None, rows, MIX_W), lambda b, i: (b, i, c))
    st4 = lambda s: pl.BlockSpec((None,) + s, lambda b, i: (b, 0, 0, 0))
    kern = functools.partial(_mlstm_kernel, chunk=chunk, valid=valid)
    return pl.pallas_call(
        kern,
        out_shape=(jax.ShapeDtypeStruct((bsz, t, MIX_W), BF16),
                   jax.ShapeDtypeStruct((bsz, N_HEADS, HEAD_DIM, HEAD_DIM), F32),
                   jax.ShapeDtypeStruct((bsz, N_HEADS, 1, HEAD_DIM), F32),
                   jax.ShapeDtypeStruct((bsz, N_HEADS, 1, 1), F32)),
        grid=(bsz, nc),
        in_specs=[col(COL_AQ), col(COL_AK), col(COL_AV), col(COL_AO),
                  pl.BlockSpec((None, rows, GATE_LANES), lambda b, i: (b, i, 0)),
                  pl.BlockSpec((None, SUBLANES, chunk), lambda b, i: (b, 0, i)),
                  pl.BlockSpec((None, 1, GATE_LANES), lambda b, i: (l, 0, 0)),
                  pl.BlockSpec((None, SUBLANES, 1), lambda b, i: (l, 0, 0)),
                  pl.BlockSpec((None, 1, MIX_W), lambda b, i: (l, 0, 0)),
                  st4((N_HEADS, HEAD_DIM, HEAD_DIM)), st4((N_HEADS, 1, HEAD_DIM)), st4((N_HEADS, 1, 1))],
        out_specs=(pl.BlockSpec((None, rows, MIX_W), lambda b, i: (b, i, 0)),
                   st4((N_HEADS, HEAD_DIM, HEAD_DIM)), st4((N_HEADS, 1, HEAD_DIM)), st4((N_HEADS, 1, 1))),
        scratch_shapes=[pltpu.VMEM((N_HEADS, HEAD_DIM, HEAD_DIM), F32),
                        pltpu.VMEM((N_HEADS, 1, HEAD_DIM), F32),
                        pltpu.VMEM((N_HEADS, 1, 1), F32)],
        compiler_params=_params(("parallel", "arbitrary"), V7X_VMEM_LIMIT),
        name="mlstm",
    )(proj, proj, proj, proj, gates, gates_t, bias_c, bias_r, ghead, c0, n0, m0)


def _sb_block(z, c, suffix, mask):
    sp = _softplus_neg_abs(z)
    ls = jnp.minimum(z, 0.0) - sp
    l1m = -jnp.maximum(z, 0.0) - sp
    if mask is not None:
        l1m = jnp.where(mask, l1m, 0.0)
    hi = l1m.astype(BF16)
    lo = (l1m - hi.astype(F32)).astype(BF16)
    excl = _dot(hi, suffix) + _dot(lo, suffix) + c
    a = jnp.exp(ls + excl)
    if mask is not None:
        a = jnp.where(mask, a, 0.0)
    return a, c + jnp.sum(l1m, axis=1, keepdims=True)


def _strict_suffix_matrix(n):
    r = lax.broadcasted_iota(jnp.int32, (n, n), 0)
    c = lax.broadcasted_iota(jnp.int32, (n, n), 1)
    return (r > c).astype(BF16)


def _sb_prompt_kernel(bias_ref, q_ref, k_ref, v_ref, o_ref, *, blk):
    h = pl.program_id(1)
    i = pl.program_id(2)
    bias = bias_ref[h]
    scale = HEAD_DIM ** -0.5
    qb = q_ref[...].astype(BF16)
    suffix = _strict_suffix_matrix(blk)
    r = lax.broadcasted_iota(jnp.int32, (blk, blk), 0)
    cidx = lax.broadcasted_iota(jnp.int32, (blk, blk), 1)

    def step(jb, c, acc, mask):
        start = pl.multiple_of(jb * blk, blk)
        kb = k_ref[pl.ds(start, blk), :].astype(BF16)
        vb = v_ref[pl.ds(start, blk), :].astype(BF16)
        z = _dot_nt(qb, kb) * scale + bias
        a, c = _sb_block(z, c, suffix, mask)
        return c, acc + _dot(a.astype(BF16), vb)

    c0 = jnp.zeros((blk, 1), F32)
    acc0 = jnp.zeros((blk, HEAD_DIM), F32)
    c, acc = step(i, c0, acc0, cidx < r)
    c, acc = lax.fori_loop(0, i, lambda t, ca: step(i - 1 - t, ca[0], ca[1], None), (c, acc))
    o_ref[...] = acc.astype(o_ref.dtype)


def _sb_prompt(proj, b_sb, l, blk=256):
    bsz, t, _ = proj.shape
    blk = min(blk, t)
    kern = functools.partial(_sb_prompt_kernel, blk=blk)
    kv = lambda c0: pl.BlockSpec((None, t, HEAD_DIM), lambda b, h, i: (b, 0, c0 * N_HEADS + h))
    return pl.pallas_call(
        kern,
        out_shape=jax.ShapeDtypeStruct((bsz, t, MIX_W), BF16),
        grid=(bsz, N_HEADS, t // blk),
        in_specs=[pl.BlockSpec(memory_space=pltpu.SMEM),
                  pl.BlockSpec((None, blk, HEAD_DIM), lambda b, h, i: (b, i, COL_BQ * N_HEADS + h)),
                  kv(COL_BK), kv(COL_BV)],
        out_specs=pl.BlockSpec((None, blk, HEAD_DIM), lambda b, h, i: (b, i, h)),
        compiler_params=_params(("parallel", "parallel", "arbitrary")),
        name="sb_prompt",
    )(b_sb[l], proj, proj, proj)


def _sb_sample_kernel(pt_ref, bias_ref, q_ref, kn_ref, vn_ref, *refs, pages_per_step):
    npg = pages_per_step
    k_refs, v_refs = refs[:npg], refs[npg:2 * npg]
    o_ref, c_scr, acc_scr = refs[2 * npg:]
    j = pl.program_id(1)
    tp = q_ref.shape[0]
    page = k_refs[0].shape[0]
    scale = HEAD_DIM ** -0.5
    suffix = _strict_suffix_matrix(page)
    qs = [q_ref[:, h * HEAD_DIM:(h + 1) * HEAD_DIM].astype(BF16) for h in range(N_HEADS)]

    def block(ks, vs, mask):
        z = jnp.concatenate([_dot_nt(qs[h], ks[h]) * scale + bias_ref[h] for h in range(N_HEADS)], axis=0)
        a, c = _sb_block(z, c_scr[...], suffix, mask)
        c_scr[...] = c
        ab = a.astype(BF16)
        acc_scr[...] += jnp.concatenate(
            [_dot(ab[h * tp:(h + 1) * tp], vs[h]) for h in range(N_HEADS)], axis=0)

    @pl.when(j == 0)
    def _():
        c_scr[...] = jnp.zeros_like(c_scr)
        acc_scr[...] = jnp.zeros_like(acc_scr)
        r = lax.broadcasted_iota(jnp.int32, (N_HEADS * tp, page), 0)
        cidx = lax.broadcasted_iota(jnp.int32, (N_HEADS * tp, page), 1)
        mask = cidx < (r % tp)
        head = lambda ref, h: _pad_rows(ref[:, h * HEAD_DIM:(h + 1) * HEAD_DIM], page).astype(BF16)
        block([head(kn_ref, h) for h in range(N_HEADS)], [head(vn_ref, h) for h in range(N_HEADS)], mask)

    for p in range(npg):
        block([k_refs[p][:, h, :].astype(BF16) for h in range(N_HEADS)],
              [v_refs[p][:, h, :].astype(BF16) for h in range(N_HEADS)], None)

    @pl.when(j == pl.num_programs(1) - 1)
    def _():
        acc = acc_scr[...]
        for h in range(N_HEADS):
            o_ref[:, h * HEAD_DIM:(h + 1) * HEAD_DIM] = acc[h * tp:(h + 1) * tp].astype(o_ref.dtype)


def _sb_sample(proj, cache_k, cache_v, page_table, b_sb, l, pages_per_step=4):
    bsz, tp, _ = proj.shape
    n_pages = page_table.shape[1]
    page = cache_k.shape[2]
    npg = pages_per_step
    steps = n_pages // npg

    def page_spec(p):
        def imap(b, j, pt):
            return (l, pt[b * n_pages + (n_pages - 1 - (j * npg + p))], 0, 0, 0)
        return pl.BlockSpec((None, None, page, N_HEADS, HEAD_DIM), imap)

    col = lambda c: pl.BlockSpec((None, tp, MIX_W), lambda b, j, pt: (b, 0, c))
    kern = functools.partial(_sb_sample_kernel, pages_per_step=npg)
    return pl.pallas_call(
        kern,
        out_shape=jax.ShapeDtypeStruct((bsz, tp, MIX_W), BF16),
        grid_spec=pltpu.PrefetchScalarGridSpec(
            num_scalar_prefetch=1,
            grid=(bsz, steps),
            in_specs=[pl.BlockSpec(memory_space=pltpu.SMEM), col(COL_BQ), col(COL_BK), col(COL_BV)]
                     + [page_spec(p) for p in range(npg)] * 2,
            out_specs=pl.BlockSpec((None, tp, MIX_W), lambda b, j, pt: (b, 0, 0)),
            scratch_shapes=[pltpu.VMEM((N_HEADS * tp, 1), F32), pltpu.VMEM((N_HEADS * tp, HEAD_DIM), F32)]),
        compiler_params=_params(("parallel", "arbitrary")),
        name="sb_sample",
    )(page_table.reshape(-1), b_sb[l], proj, proj, proj, *([cache_k] * npg), *([cache_v] * npg))


def _cmul(ar, ai, br, bi):
    return ar * br - ai * bi, ar * bi + ai * br


def _s5_prep_kernel(lre_ref, lim_ref, ls_ref, bre_ref, bim_ref, tab_ref, bbre_ref, bbim_ref):
    lam_re, lam_im = lre_ref[...], lim_ref[...]
    step = jnp.exp(ls_ref[...])
    decay = jnp.exp(lam_re * step)
    a_re = decay * jnp.cos(lam_im * step)
    a_im = decay * jnp.sin(lam_im * step)
    inv = 1.0 / (lam_re * lam_re + lam_im * lam_im)
    f_re = ((a_re - 1.0) * lam_re + a_im * lam_im) * inv
    f_im = (a_im * lam_re - (a_re - 1.0) * lam_im) * inv
    b_re, b_im = bre_ref[...], bim_ref[...]
    bbre_ref[...] = f_re * b_re - f_im * b_im
    bbim_ref[...] = f_re * b_im + f_im * b_re

    pw = {1: (a_re, a_im)}
    pw[2] = _cmul(*pw[1], *pw[1])
    pw[3] = _cmul(*pw[2], *pw[1])
    pw[4] = _cmul(*pw[2], *pw[2])
    pw[5] = _cmul(*pw[4], *pw[1])
    pw[6] = _cmul(*pw[4], *pw[2])
    pw[7] = _cmul(*pw[4], *pw[3])
    pw[8] = _cmul(*pw[4], *pw[4])
    row = lax.broadcasted_iota(jnp.int32, (SUBLANES, lam_re.shape[1]), 0)
    for part in range(2):
        carry = jnp.zeros(row.shape, F32)
        for r in range(SUBLANES):
            carry = jnp.where(row == r, pw[r + 1][part], carry)
        tab_ref[part] = carry
        for idx, k in enumerate((1, 2, 4)):
            tab_ref[2 + 2 * idx + part] = jnp.where(row >= k, pw[k][part], 0.0)


def _s5_prep(lam_re, lam_im, log_step, b_re, b_im):
    depth = lam_re.shape[0]
    flat = lambda a: a.reshape(depth, 1, S5_WIDTH)
    bt = lambda a: a.transpose(0, 3, 1, 2).reshape(depth, S5_GROUP_CH, S5_WIDTH)
    step = jnp.broadcast_to(log_step[:, :, None], (depth, S5_GROUPS, S5_STATE))
    vec = pl.BlockSpec((None, 1, S5_WIDTH), lambda l: (l, 0, 0))
    mat = pl.BlockSpec((None, S5_GROUP_CH, S5_WIDTH), lambda l: (l, 0, 0))
    return pl.pallas_call(
        _s5_prep_kernel,
        out_shape=(jax.ShapeDtypeStruct((depth, 8, SUBLANES, S5_WIDTH), F32),
                   jax.ShapeDtypeStruct((depth, S5_GROUP_CH, S5_WIDTH), F32),
                   jax.ShapeDtypeStruct((depth, S5_GROUP_CH, S5_WIDTH), F32)),
        grid=(depth,),
        in_specs=[vec, vec, vec, mat, mat],
        out_specs=(pl.BlockSpec((None, 8, SUBLANES, S5_WIDTH), lambda l: (l, 0, 0, 0)), mat, mat),
        compiler_params=_params(("parallel",)),
        name="s5_prep",
    )(flat(lam_re), flat(lam_im), flat(step), bt(b_re), bt(b_im))


def _s5_kernel(u_ref, wb_ref, tab_ref, hre_ref, him_ref, wcre_ref, wcim_ref, d_ref, wglu_ref,
               y_ref, sre_out, sim_out, sre, sim, cre, cim, *, valid_last, slab):
    ci = pl.program_id(1)
    tc = u_ref.shape[0]
    sub = MIX_W // S5_SUPER
    wid = S5_WIDTH // S5_SUPER

    @pl.when(ci == 0)
    def _():
        cre[...] = jnp.broadcast_to(hre_ref[...], cre.shape)
        cim[...] = jnp.broadcast_to(him_ref[...], cim.shape)

    u = u_ref[...]
    ub = u.astype(BF16)
    for g in range(S5_SUPER):
        bu = _dot(ub[:, g * sub:(g + 1) * sub], wb_ref[g])
        sre[:, g * wid:(g + 1) * wid] = bu[:, :wid]
        sim[:, g * wid:(g + 1) * wid] = bu[:, wid:]

    for s0 in range(0, S5_WIDTH, slab):
        lanes = slice(s0, s0 + slab)
        pr, pi = tab_ref[0, :, lanes], tab_ref[1, :, lanes]
        levels = [(k, tab_ref[2 + 2 * idx, :, lanes], tab_ref[3 + 2 * idx, :, lanes])
                  for idx, k in enumerate((1, 2, 4))]

        def body(r, carry, lanes=lanes, pr=pr, pi=pi, levels=levels):
            c_re, c_im = carry
            row = pl.multiple_of(r * SUBLANES, SUBLANES)
            xr = sre[pl.ds(row, SUBLANES), lanes]
            xi = sim[pl.ds(row, SUBLANES), lanes]
            for k, mr, mi in levels:
                rr = pltpu.roll(xr, k, axis=0)
                ri = pltpu.roll(xi, k, axis=0)
                xr, xi = xr + mr * rr - mi * ri, xi + mr * ri + mi * rr
            xr, xi = xr + pr * c_re - pi * c_im, xi + pr * c_im + pi * c_re
            sre[pl.ds(row, SUBLANES), lanes] = xr
            sim[pl.ds(row, SUBLANES), lanes] = xi
            last = SUBLANES - 1
            return (jnp.broadcast_to(xr[last:last + 1, :], xr.shape),
                    jnp.broadcast_to(xi[last:last + 1, :], xi.shape))

        c_re, c_im = lax.fori_loop(0, tc // SUBLANES, body, (cre[:, lanes], cim[:, lanes]))
        cre[:, lanes] = c_re
        cim[:, lanes] = c_im

    s_re_b = sre[...].astype(BF16)
    s_im_b = sim[...].astype(BF16)
    y = jnp.concatenate(
        [_dot(s_re_b[:, g * wid:(g + 1) * wid], wcre_ref[g]) - _dot(s_im_b[:, g * wid:(g + 1) * wid], wcim_ref[g])
         for g in range(S5_SUPER)], axis=1)
    y = y + d_ref[...] * u
    y = 0.5 * y * (1.0 + jnp.tanh(math.sqrt(2.0 / math.pi) * (y + 0.044715 * (y * y * y))))
    yy = _dot(y.astype(BF16), wglu_ref[...])
    y_ref[...] = (yy[:, :MIX_W] * jax.nn.sigmoid(yy[:, MIX_W:])).astype(y_ref.dtype)

    @pl.when(ci == pl.num_programs(1) - 1)
    def _():
        sre_out[...] = sre[valid_last - 1:valid_last, :]
        sim_out[...] = sim[valid_last - 1:valid_last, :]


def _s5(proj, wb, tab, h_re, h_im, wc_re, wc_im, d_skip, w_glu, l, tc, valid_last, slab=256):
    bsz, t, _ = proj.shape
    tc = min(tc, t)
    layer = lambda s: pl.BlockSpec((None,) + s, lambda b, i: (l,) + (0,) * len(s))
    st = pl.BlockSpec((None, 1, S5_WIDTH), lambda b, i: (b, 0, 0))
    kern = functools.partial(_s5_kernel, valid_last=valid_last, slab=slab)
    wid = S5_WIDTH // S5_SUPER
    return pl.pallas_call(
        kern,
        out_shape=(jax.ShapeDtypeStruct((bsz, t, MIX_W), BF16),
                   jax.ShapeDtypeStruct((bsz, 1, S5_WIDTH), F32),
                   jax.ShapeDtypeStruct((bsz, 1, S5_WIDTH), F32)),
        grid=(bsz, t // tc),
        in_specs=[pl.BlockSpec((None, tc, MIX_W), lambda b, i: (b, i, COL_SU)),
                  layer((S5_SUPER, MIX_W // S5_SUPER, 2 * wid)),
                  layer((8, SUBLANES, S5_WIDTH)),
                  st, st,
                  layer((S5_SUPER, wid, MIX_W // S5_SUPER)), layer((S5_SUPER, wid, MIX_W // S5_SUPER)),
                  layer((1, MIX_W)), layer((MIX_W, 2 * MIX_W))],
        out_specs=(pl.BlockSpec((None, tc, MIX_W), lambda b, i: (b, i, 0)), st, st),
        scratch_shapes=[pltpu.VMEM((tc, S5_WIDTH), F32), pltpu.VMEM((tc, S5_WIDTH), F32),
                        pltpu.VMEM((SUBLANES, S5_WIDTH), F32), pltpu.VMEM((SUBLANES, S5_WIDTH), F32)],
        compiler_params=_params(("parallel", "arbitrary"), V7X_VMEM_LIMIT),
        name="s5",
    )(proj, wb, tab, h_re, h_im, wc_re, wc_im, d_skip, w_glu)


def _block_diag_groups(w):
    depth, _, a, b = w.shape
    per = S5_GROUPS // S5_SUPER
    w = w.reshape(depth, S5_SUPER, per, a, b)
    eye = jnp.eye(per, dtype=w.dtype)
    bd = w[:, :, :, :, None, :] * eye[None, None, :, None, :, None]
    return bd.reshape(depth, S5_SUPER, per * a, per * b)


def _conv_kernel(gb_ref, gc_ref, xv_ref, hc_ref, hx_ref, buf_ref, w_ref, y_ref, new_ref, *, valid_last):
    ci = pl.program_id(1)
    tc = gb_ref.shape[0]
    z = gc_ref[...] * xv_ref[...]
    prev = jnp.where(ci == 0, buf_ref[...], hc_ref[...] * hx_ref[...])
    zz = jnp.concatenate([prev, z], axis=0)
    w = w_ref[...]
    y = sum(w[j:j + 1, :] * zz[SUBLANES - (CONV_K - 1) + j:SUBLANES - (CONV_K - 1) + j + tc] for j in range(CONV_K))
    y_ref[...] = (gb_ref[...] * y).astype(y_ref.dtype)

    @pl.when(ci == pl.num_programs(1) - 1)
    def _():
        end = SUBLANES + valid_last
        new_ref[...] = zz[end - (CONV_K - 1):end]


def _conv(proj, buf8, w, l, tc, valid_last):
    bsz, t, _ = proj.shape
    tc = min(tc, t)
    per = tc // SUBLANES
    col = lambda c: pl.BlockSpec((None, tc, MIX_W), lambda b, i: (b, i, c))
    halo = lambda c: pl.BlockSpec((None, SUBLANES, MIX_W), lambda b, i: (b, jnp.maximum(i * per - 1, 0), c))
    kern = functools.partial(_conv_kernel, valid_last=valid_last)
    return pl.pallas_call(
        kern,
        out_shape=(jax.ShapeDtypeStruct((bsz, t, MIX_W), BF16),
                   jax.ShapeDtypeStruct((bsz, CONV_K - 1, MIX_W), F32)),
        grid=(bsz, t // tc),
        in_specs=[col(COL_CB), col(COL_CC), col(COL_CX), halo(COL_CC), halo(COL_CX),
                  pl.BlockSpec((None, SUBLANES, MIX_W), lambda b, i: (b, 0, 0)),
                  pl.BlockSpec((None, CONV_K, MIX_W), lambda b, i: (l, 0, 0))],
        out_specs=(pl.BlockSpec((None, tc, MIX_W), lambda b, i: (b, i, 0)),
                   pl.BlockSpec((None, CONV_K - 1, MIX_W), lambda b, i: (b, 0, 0))),
        compiler_params=_params(("parallel", "arbitrary")),
        name="short_conv",
    )(proj, proj, proj, proj, proj, buf8, w)


def _layer(x, mod, wts, l, past, sizes):
    bsz, t, d = x.shape
    sh1, sc1, g1, sh2, sc2, g2 = mod
    tm = sizes["tm"]
    flat = sizes["flat"]
    as_mm = lambda a: a.reshape(flat + a.shape[2:])
    as_seq = lambda a: a.reshape((bsz, t) + a.shape[2:])
    mm_mod = lambda a: a if a.shape[1] == 1 else as_mm(a)

    h = _norm_mod(x, wts["g_norm1"], l, sc1, sh1)
    hm = as_mm(h)
    proj = as_seq(_linear(hm, wts["w_in"], l, F32, tm, 512, "in_proj"))
    gates = as_seq(_linear(hm, wts["w_if"], l, F32, tm, GATE_LANES, "gate_proj"))
    gates_t = jnp.swapaxes(gates[:, :, :SUBLANES], 1, 2)
    if sizes["chunk"] > t:
        gates_t = jnp.pad(gates_t, ((0, 0), (0, 0), (0, sizes["chunk"] - t)))

    y_a, mc, mn, mm = _mlstm(proj, gates, gates_t, wts["gate_bias_c"], wts["gate_bias_r"], wts["g_head"], l,
                             past["mlstm_c"], past["mlstm_n"], past["mlstm_m"],
                             sizes["chunk"], sizes["rows"], sizes["valid"])
    if "sb_k" in past:
        y_b = _sb_sample(proj, past["sb_k"], past["sb_v"], past["page_table"], wts["b_sb"], l)
    else:
        y_b = _sb_prompt(proj, wts["b_sb"], l)
    y_s, s_re, s_im = _s5(proj, wts["s5_wb"], wts["s5_tab"], past["s5_re"], past["s5_im"],
                          wts["s5_wc_re"], wts["s5_wc_im"], wts["s5_d"], wts["w_s5_glu"], l,
                          sizes["tc"], sizes["valid_last"])
    y_c, conv_new = _conv(proj, past["conv"], wts["conv_w"], l, sizes["tc"], sizes["valid_last"])

    merged = _gate_merge(hm, wts["w_gate"], wts["b_gate"], [as_mm(y) for y in (y_a, y_b, y_s, y_c)],
                         wts["w_branch"], l, tm, 256)
    x = as_seq(_linear_residual(merged, wts["w_out"], l, as_mm(x), mm_mod(g1), tm, 512, "out_proj"))

    h2 = as_mm(_norm_mod(x, wts["g_norm2"], l, sc2, sh2))
    hidden = _ffn_up(h2, wts["w_ffn_gate"], wts["w_ffn_up"], l, tm, 512)
    x = as_seq(_linear_residual(hidden, wts["w_ffn_down"], l, as_mm(x), mm_mod(g2), min(tm, 512), 512, "ffn_down"))

    nv = sizes["real"]
    kb = proj[:, :nv, COL_BK * MIX_W:(COL_BK + 1) * MIX_W].reshape(bsz, nv, N_HEADS, HEAD_DIM)
    vb = proj[:, :nv, COL_BV * MIX_W:(COL_BV + 1) * MIX_W].reshape(bsz, nv, N_HEADS, HEAD_DIM)
    state = {"sb_k": kb, "sb_v": vb, "mlstm_c": mc, "mlstm_n": mn.reshape(bsz, N_HEADS, HEAD_DIM),
             "mlstm_m": mm.reshape(bsz, N_HEADS),
             "s5_re": s_re.reshape(bsz, S5_GROUPS, S5_STATE), "s5_im": s_im.reshape(bsz, S5_GROUPS, S5_STATE),
             "conv": conv_new}
    return x, state


def kernel(x_prompt, x_sample, cache_sb_k, cache_sb_v, state_mlstm_c, state_mlstm_n, state_mlstm_m, state_s5_re, state_s5_im, state_conv, page_table, c_prompt, c_sample, w_ada, b_ada, g_norm1, g_norm2, w_in, b_mlstm_i, b_mlstm_f, g_mlstm_head, b_sb, s5_lambda_re, s5_lambda_im, s5_b_re, s5_b_im, s5_c_re, s5_c_im, s5_d, s5_log_step, w_s5_glu, conv_w, w_gate, b_gate, w_branch, w_out, w_ffn_gate, w_ffn_up, w_ffn_down, g_final):
    depth = w_in.shape[0]
    bp, tp, d = x_prompt.shape
    bs, ts, _ = x_sample.shape
    tpad = SUBLANES
    n_gate = 2 * N_HEADS
    split = 4 * MIX_W

    gate_bias = jnp.concatenate([b_mlstm_i, b_mlstm_f], axis=1)
    wts = {
        "g_norm1": g_norm1.reshape(depth, 1, d), "g_norm2": g_norm2.reshape(depth, 1, d),
        "w_in": jnp.concatenate([w_in[:, :, :split], w_in[:, :, split + n_gate:]], axis=2).astype(BF16),
        "w_if": jnp.pad(w_in[:, :, split:split + n_gate], ((0, 0), (0, 0), (0, GATE_LANES - n_gate))).astype(BF16),
        "gate_bias_c": jnp.pad(gate_bias, ((0, 0), (0, GATE_LANES - n_gate))).reshape(depth, 1, GATE_LANES),
        "gate_bias_r": gate_bias.reshape(depth, n_gate, 1),
        "g_head": g_mlstm_head.reshape(depth, 1, MIX_W),
        "b_sb": b_sb,
        "s5_d": s5_d.reshape(depth, 1, MIX_W),
        "w_s5_glu": w_s5_glu.astype(BF16),
        "conv_w": conv_w,
        "w_gate": w_gate.reshape(depth, d, N_BRANCH, d).transpose(0, 2, 1, 3).astype(BF16),
        "b_gate": b_gate.reshape(depth, N_BRANCH, 1, d),
        "w_branch": w_branch.astype(BF16),
        "w_out": w_out.astype(BF16),
        "w_ffn_gate": w_ffn_gate.astype(BF16), "w_ffn_up": w_ffn_up.astype(BF16),
        "w_ffn_down": w_ffn_down.astype(BF16),
    }
    tab, bb_re, bb_im = _s5_prep(s5_lambda_re, s5_lambda_im, s5_log_step, s5_b_re, s5_b_im)
    wts["s5_tab"] = tab
    to_gcp = lambda a: a.reshape(depth, S5_GROUP_CH, S5_GROUPS, S5_STATE).transpose(0, 2, 1, 3)
    wts["s5_wb"] = jnp.concatenate([_block_diag_groups(to_gcp(bb_re)), _block_diag_groups(to_gcp(bb_im))],
                                   axis=3).astype(BF16)
    wts["s5_wc_re"] = _block_diag_groups(s5_c_re.transpose(0, 1, 3, 2)).astype(BF16)
    wts["s5_wc_im"] = _block_diag_groups(s5_c_im.transpose(0, 1, 3, 2)).astype(BF16)

    c_all = jnp.concatenate([c_prompt, c_sample], axis=0)
    c_all = jnp.pad(c_all, ((0, -c_all.shape[0] % SUBLANES), (0, 0)))
    mod_all = _ada_all(c_all, w_ada, b_ada).reshape(depth, c_all.shape[0], 6, d)

    sizes_p = {"tm": 1024, "flat": (bp, tp), "chunk": 256, "rows": 256, "valid": 256, "tc": 256, "valid_last": 256, "real": tp}
    sizes_s = {"tm": bs * tpad, "flat": (1, bs * tpad), "chunk": 128, "rows": tpad, "valid": ts, "tc": tpad,
               "valid_last": ts, "real": ts}

    xp = x_prompt
    xs = jnp.pad(x_sample, ((0, 0), (0, tpad - ts), (0, 0)))
    zeros_p = {
        "mlstm_c": jnp.zeros((bp, N_HEADS, HEAD_DIM, HEAD_DIM), F32),
        "mlstm_n": jnp.zeros((bp, N_HEADS, 1, HEAD_DIM), F32),
        "mlstm_m": jnp.zeros((bp, N_HEADS, 1, 1), F32),
        "s5_re": jnp.zeros((bp, 1, S5_WIDTH), F32), "s5_im": jnp.zeros((bp, 1, S5_WIDTH), F32),
        "conv": jnp.zeros((bp, SUBLANES, MIX_W), F32),
    }
    new_p, new_s = [], []
    for l in range(depth):
        mod_p = [mod_all[l, :bp, i].reshape(bp, 1, d) for i in range(6)]
        mod_s = [jnp.repeat(mod_all[l, bp:bp + bs, i], tpad, axis=0).reshape(bs, tpad, d) for i in range(6)]
        past_s = {
            "sb_k": cache_sb_k, "sb_v": cache_sb_v, "page_table": page_table,
            "mlstm_c": state_mlstm_c[l], "mlstm_n": state_mlstm_n[l].reshape(bs, N_HEADS, 1, HEAD_DIM),
            "mlstm_m": state_mlstm_m[l].reshape(bs, N_HEADS, 1, 1),
            "s5_re": state_s5_re[l].reshape(bs, 1, S5_WIDTH), "s5_im": state_s5_im[l].reshape(bs, 1, S5_WIDTH),
            "conv": jnp.pad(state_conv[l], ((0, 0), (SUBLANES - (CONV_K - 1), 0), (0, 0))),
        }
        xp, st_p = _layer(xp, mod_p, wts, l, zeros_p, sizes_p)
        xs, st_s = _layer(xs, mod_s, wts, l, past_s, sizes_s)
        new_p.append(st_p)
        new_s.append(st_s)

    y_prompt = _final_norm(xp, g_final.reshape(1, d))
    y_sample = _final_norm(xs, g_final.reshape(1, d))[:, :ts]
    stk = lambda states, name: jnp.stack([s[name] for s in states])
    return (y_prompt, y_sample,
            stk(new_p, "sb_k"), stk(new_p, "sb_v"), stk(new_s, "sb_k"), stk(new_s, "sb_v"),
            stk(new_p, "mlstm_c"), stk(new_p, "mlstm_n"), stk(new_p, "mlstm_m"),
            stk(new_s, "mlstm_c"), stk(new_s, "mlstm_n"), stk(new_s, "mlstm_m"),
            stk(new_p, "s5_re"), stk(new_p, "s5_im"), stk(new_s, "s5_re"), stk(new_s, "s5_im"),
            stk(new_p, "conv"), stk(new_s, "conv"))
```

```python
import functools
import math

import jax
import jax.numpy as jnp
from jax import lax
from jax.experimental import pallas as pl
from jax.experimental.pallas import tpu as pltpu

F32 = jnp.float32
BF16 = jnp.bfloat16

N_HEADS = 4
HEAD_DIM = 128
MIX_W = N_HEADS * HEAD_DIM
N_BRANCH = 4
S5_GROUP_CH = 16
S5_GROUPS = MIX_W // S5_GROUP_CH
S5_STATE = 64
S5_WIDTH = S5_GROUPS * S5_STATE
S5_SUPER = 4
CONV_K = 3
EPS = 1e-6
SUBLANES = 8
GATE_LANES = 128
V7X_VMEM_LIMIT = 56 * 1024 * 1024

COL_AQ, COL_AK, COL_AV, COL_AO, COL_BQ, COL_BK, COL_BV, COL_SU, COL_CB, COL_CC, COL_CX = range(11)
N_PROJ_BLOCKS = 11

NT_DIMS = (((1,), (1,)), ((), ()))
TN_DIMS = (((0,), (0,)), ((), ()))


def _dot(a, b):
    return jnp.dot(a, b, preferred_element_type=F32)


def _dot_nt(a, b):
    return lax.dot_general(a, b, NT_DIMS, preferred_element_type=F32)


def _dot_exact(a, b):
    return jnp.dot(a, b, precision=lax.Precision.HIGHEST, preferred_element_type=F32)


def _softplus_neg_abs(z):
    return jnp.log(1.0 + jnp.exp(-jnp.abs(z)))


def _log_sigmoid(z):
    return jnp.minimum(z, 0.0) - _softplus_neg_abs(z)


def _pad_rows(x, rows):
    if x.shape[0] == rows:
        return x
    return jnp.concatenate([x, jnp.zeros((rows - x.shape[0],) + x.shape[1:], x.dtype)], axis=0)


def _params(sem, vmem=None):
    return pltpu.CompilerParams(dimension_semantics=sem, vmem_limit_bytes=vmem)


def _ada_kernel(c_ref, w_ref, b_ref, o_ref):
    c = c_ref[...]
    a = (c * jax.nn.sigmoid(c)).astype(BF16)
    o_ref[...] = _dot(a, w_ref[...].astype(BF16)) + b_ref[...]


def _ada_all(c, w_ada, b_ada, tn=1024):
    depth, d, n = w_ada.shape
    rows = c.shape[0]
    return pl.pallas_call(
        _ada_kernel,
        out_shape=jax.ShapeDtypeStruct((depth, rows, n), F32),
        grid=(depth, n // tn),
        in_specs=[pl.BlockSpec((rows, d), lambda l, j: (0, 0)),
                  pl.BlockSpec((None, d, tn), lambda l, j: (l, 0, j)),
                  pl.BlockSpec((None, 1, tn), lambda l, j: (l, 0, j))],
        out_specs=pl.BlockSpec((None, rows, tn), lambda l, j: (l, 0, j)),
        compiler_params=_params(("parallel", "parallel"), V7X_VMEM_LIMIT),
        name="ada_mod",
    )(c, w_ada, b_ada.reshape(depth, 1, n))


def _norm_mod_kernel(x_ref, g_ref, sc_ref, sh_ref, o_ref):
    x = x_ref[...]
    y = x * lax.rsqrt(jnp.mean(x * x, axis=-1, keepdims=True) + EPS) * g_ref[...]
    o_ref[...] = (y * (1.0 + sc_ref[...]) + sh_ref[...]).astype(o_ref.dtype)


def _norm_kernel(x_ref, g_ref, o_ref):
    x = x_ref[...]
    o_ref[...] = x * lax.rsqrt(jnp.mean(x * x, axis=-1, keepdims=True) + EPS) * g_ref[...]


def _row_spec(arr, tm, width, col):
    if arr.shape[1] == 1:
        return pl.BlockSpec((None, 1, width), lambda b, i, *r: (b, 0, col(*r)))
    return pl.BlockSpec((None, tm, width), lambda b, i, *r: (b, i, col(*r)))


def _norm_mod(x, g, l, sc, sh, tm=512):
    bsz, t, d = x.shape
    tm = min(tm, t)
    zero = lambda *r: 0
    return pl.pallas_call(
        _norm_mod_kernel,
        out_shape=jax.ShapeDtypeStruct((bsz, t, d), BF16),
        grid=(bsz, t // tm),
        in_specs=[pl.BlockSpec((None, tm, d), lambda b, i: (b, i, 0)),
                  pl.BlockSpec((None, 1, d), lambda b, i: (l, 0, 0)),
                  _row_spec(sc, tm, d, zero), _row_spec(sh, tm, d, zero)],
        out_specs=pl.BlockSpec((None, tm, d), lambda b, i: (b, i, 0)),
        compiler_params=_params(("parallel", "parallel")),
        name="norm_mod",
    )(x, g, sc, sh)


def _final_norm(x, g, tm=512):
    bsz, t, d = x.shape
    tm = min(tm, t)
    return pl.pallas_call(
        _norm_kernel,
        out_shape=jax.ShapeDtypeStruct((bsz, t, d), F32),
        grid=(bsz, t // tm),
        in_specs=[pl.BlockSpec((None, tm, d), lambda b, i: (b, i, 0)),
                  pl.BlockSpec((1, d), lambda b, i: (0, 0))],
        out_specs=pl.BlockSpec((None, tm, d), lambda b, i: (b, i, 0)),
        compiler_params=_params(("parallel", "parallel")),
        name="final_norm",
    )(x, g)


def _mm_kernel(a_ref, w_ref, o_ref):
    o_ref[...] = _dot(a_ref[...], w_ref[...]).astype(o_ref.dtype)


def _linear(a, w, l, out_dtype, tm, tn, name):
    bsz, t, k = a.shape
    n = w.shape[-1]
    tm, tn = min(tm, t), min(tn, n)
    return pl.pallas_call(
        _mm_kernel,
        out_shape=jax.ShapeDtypeStruct((bsz, t, n), out_dtype),
        grid=(bsz, t // tm, n // tn),
        in_specs=[pl.BlockSpec((None, tm, k), lambda b, i, j: (b, i, 0)),
                  pl.BlockSpec((None, k, tn), lambda b, i, j: (l, 0, j))],
        out_specs=pl.BlockSpec((None, tm, tn), lambda b, i, j: (b, i, j)),
        compiler_params=_params(("parallel", "parallel", "parallel"), V7X_VMEM_LIMIT),
        name=name,
    )(a, w)


def _mm_res_kernel(a_ref, w_ref, x_ref, g_ref, o_ref, *acc, nk):
    part = _dot(a_ref[...], w_ref[...])
    if nk == 1:
        o_ref[...] = x_ref[...] + g_ref[...] * part
        return
    acc_ref, = acc
    kk = pl.program_id(3)

    @pl.when(kk == 0)
    def _():
        acc_ref[...] = part

    @pl.when(kk > 0)
    def _():
        acc_ref[...] += part

    @pl.when(kk == nk - 1)
    def _():
        o_ref[...] = x_ref[...] + g_ref[...] * acc_ref[...]


def _linear_residual(a, w, l, x, g, tm, tn, nk, name):
    bsz, t, k = a.shape
    n = w.shape[-1]
    tm, tn = min(tm, t), min(tn, n)
    tk = k // nk
    return pl.pallas_call(
        functools.partial(_mm_res_kernel, nk=nk),
        out_shape=jax.ShapeDtypeStruct((bsz, t, n), F32),
        grid=(bsz, t // tm, n // tn, nk),
        in_specs=[pl.BlockSpec((None, tm, tk), lambda b, i, j, kk: (b, i, kk)),
                  pl.BlockSpec((None, tk, tn), lambda b, i, j, kk: (l, kk, j)),
                  pl.BlockSpec((None, tm, tn), lambda b, i, j, kk: (b, i, j)),
                  _row_spec(g, tm, tn, lambda j, kk: j)],
        out_specs=pl.BlockSpec((None, tm, tn), lambda b, i, j, kk: (b, i, j)),
        scratch_shapes=[pltpu.VMEM((tm, tn), F32)] if nk > 1 else [],
        compiler_params=_params(("parallel", "parallel", "parallel", "arbitrary"), V7X_VMEM_LIMIT),
        name=name,
    )(a, w, x, g)


def _ffn_up_kernel(a_ref, wg_ref, wu_ref, o_ref):
    a = a_ref[...]
    gate = _dot(a, wg_ref[...].astype(BF16))
    o_ref[...] = (gate * jax.nn.sigmoid(gate) * _dot(a, wu_ref[...].astype(BF16))).astype(o_ref.dtype)


def _ffn_up(a, wg, wu, l, tm, tn):
    bsz, t, k = a.shape
    n = wg.shape[-1]
    tm, tn = min(tm, t), min(tn, n)
    wspec = pl.BlockSpec((None, k, tn), lambda b, i, j: (l, 0, j))
    return pl.pallas_call(
        _ffn_up_kernel,
        out_shape=jax.ShapeDtypeStruct((bsz, t, n), BF16),
        grid=(bsz, t // tm, n // tn),
        in_specs=[pl.BlockSpec((None, tm, k), lambda b, i, j: (b, i, 0)), wspec, wspec],
        out_specs=pl.BlockSpec((None, tm, tn), lambda b, i, j: (b, i, j)),
        compiler_params=_params(("parallel", "parallel", "parallel"), V7X_VMEM_LIMIT),
        name="ffn_up",
    )(a, wg, wu)


def _gate_merge_kernel(h_ref, *refs):
    wg_refs, bg_refs, y_refs = refs[:N_BRANCH], refs[N_BRANCH:2 * N_BRANCH], refs[2 * N_BRANCH:3 * N_BRANCH]
    wb_ref, o_ref = refs[3 * N_BRANCH:]
    h = h_ref[...]
    acc = None
    for g in range(N_BRANCH):
        gate = jax.nn.sigmoid(_dot(h, wg_refs[g][...].astype(BF16)) + bg_refs[g][...])
        term = gate * _dot(y_refs[g][...], wb_ref[g].astype(BF16))
        acc = term if acc is None else acc + term
    o_ref[...] = acc.astype(o_ref.dtype)


def _gate_merge(h, wg, bg, ys, wb, l, tm, tn):
    bsz, t, k = h.shape
    d = wb.shape[-1]
    tm, tn = min(tm, t), min(tn, d)
    per = d // tn
    yspec = pl.BlockSpec((None, tm, MIX_W), lambda b, i, j: (b, i, 0))
    wspec = lambda g: pl.BlockSpec((None, k, tn), lambda b, i, j: (l, 0, g * per + j))
    bspec = lambda g: pl.BlockSpec((None, 1, tn), lambda b, i, j: (l, 0, g * per + j))
    branches = range(N_BRANCH)
    return pl.pallas_call(
        _gate_merge_kernel,
        out_shape=jax.ShapeDtypeStruct((bsz, t, d), BF16),
        grid=(bsz, t // tm, d // tn),
        in_specs=[pl.BlockSpec((None, tm, k), lambda b, i, j: (b, i, 0))]
                 + [wspec(g) for g in branches] + [bspec(g) for g in branches] + [yspec] * N_BRANCH
                 + [pl.BlockSpec((None, N_BRANCH, MIX_W, tn), lambda b, i, j: (l, 0, 0, j))],
        out_specs=pl.BlockSpec((None, tm, tn), lambda b, i, j: (b, i, j)),
        compiler_params=_params(("parallel", "parallel", "parallel"), V7X_VMEM_LIMIT),
        name="gate_merge",
    )(h, *([wg] * N_BRANCH), *([bg] * N_BRANCH), *ys, wb)


def _mlstm_kernel(q_ref, k_ref, v_ref, og_ref, gc_ref, gr_ref, bc_ref, br_ref, gh_ref, c0_ref, n0_ref, m0_ref,
                  y_ref, c_out, n_out, m_out, c_scr, n_scr, m_scr, *, chunk, valid):
    ci = pl.program_id(1)
    rows = q_ref.shape[0]

    @pl.when(ci == 0)
    def _():
        c_scr[...] = c0_ref[...]
        n_scr[...] = n0_ref[...]
        m_scr[...] = m0_ref[...]

    pos_c = lax.broadcasted_iota(jnp.int32, (chunk, 1), 0)
    pos_r = lax.broadcasted_iota(jnp.int32, (1, chunk), 1)
    tri_r = lax.broadcasted_iota(jnp.int32, (chunk, chunk), 0)
    tri_c = lax.broadcasted_iota(jnp.int32, (chunk, chunk), 1)
    causal = tri_c <= tri_r
    lower = causal.astype(F32)
    upper = (tri_r <= tri_c).astype(F32)

    gates_c = _pad_rows(gc_ref[...], chunk) + bc_ref[...]
    gates_r = gr_ref[...] + br_ref[...]
    lf_c = _log_sigmoid(gates_c)
    lf_r = _log_sigmoid(gates_r)
    if valid < chunk:
        lf_c = jnp.where(pos_c < valid, lf_c, 0.0)
        lf_r = jnp.where(pos_r < valid, lf_r, 0.0)
        gates_c = jnp.where(pos_c < valid, gates_c, -jnp.inf)
        gates_r = jnp.where(pos_r < valid, gates_r, -jnp.inf)
    cum_c = _dot_exact(lower, lf_c)
    cum_r = _dot_exact(lf_r, upper)

    for h in range(N_HEADS):
        sl = slice(h * HEAD_DIM, (h + 1) * HEAD_DIM)
        q = _pad_rows(q_ref[:, sl], chunk)
        k = _pad_rows(k_ref[:, sl], chunk) * (HEAD_DIM ** -0.5)
        v = _pad_rows(v_ref[:, sl], chunk)
        qb, kb, vb = q.astype(BF16), k.astype(BF16), v.astype(BF16)
        b_c = cum_c[:, N_HEADS + h:N_HEADS + h + 1]
        b_r = cum_r[N_HEADS + h:N_HEADS + h + 1, :]
        ig_c = gates_c[:, h:h + 1]
        ig_r = gates_r[h:h + 1, :]
        m_prev = m_scr[h]
        c_prev = c_scr[h]
        n_prev = n_scr[h]

        d_intra = jnp.where(causal, b_c - b_r + ig_r, -jnp.inf)
        m_inter = b_c + m_prev
        m_t = jnp.maximum(m_inter, jnp.max(d_intra, axis=1, keepdims=True))
        w_intra = jnp.exp(d_intra - m_t)
        w_inter = jnp.exp(m_inter - m_t)
        s = w_intra * _dot_nt(qb, kb)
        num = w_inter * _dot_nt(qb, c_prev.astype(BF16)) + _dot(s.astype(BF16), vb)
        den = w_inter * jnp.sum(q * n_prev, axis=1, keepdims=True) + jnp.sum(s, axis=1, keepdims=True)
        hh = num / jnp.maximum(jnp.abs(den), jnp.exp(-m_t))

        hn = hh * lax.rsqrt(jnp.mean(hh * hh, axis=-1, keepdims=True) + EPS) * gh_ref[:, sl]
        y = hn[:rows] * jax.nn.sigmoid(og_ref[:, sl])
        y_ref[:, sl] = y.astype(y_ref.dtype)

        b_end = b_c[chunk - 1:chunk, :]
        log_w = b_end - b_c + ig_c
        m_end = jnp.maximum(b_end + m_prev, jnp.max(log_w, axis=0, keepdims=True))
        w_k = jnp.exp(log_w - m_end)
        decay = jnp.exp(b_end + m_prev - m_end)
        wv = (w_k * v).astype(BF16)
        c_scr[h] = decay * c_prev + lax.dot_general(wv, kb, TN_DIMS, preferred_element_type=F32)
        n_scr[h] = decay * n_prev + jnp.sum(w_k * k, axis=0, keepdims=True)
        m_scr[h] = m_end

    @pl.when(ci == pl.num_programs(1) - 1)
    def _():
        c_out[...] = c_scr[...]
        n_out[...] = n_scr[...]
        m_out[...] = m_scr[...]


def _mlstm(proj, gates, gates_t, bias_c, bias_r, ghead, l, c0, n0, m0, chunk, rows, valid):
    bsz, t, _ = proj.shape
    nc = t // rows
    col = lambda c: pl.BlockSpec((None, rows, MIX_W), lambda b, i: (b, i, c))
    st4 = lambda s: pl.BlockSpec((None,) + s, lambda b, i: (b, 0, 0, 0))
    kern = functools.partial(_mlstm_kernel, chunk=chunk, valid=valid)
    return pl.pallas_call(
        kern,
        out_shape=(jax.ShapeDtypeStruct((bsz, t, MIX_W), BF16),
                   jax.ShapeDtypeStruct((bsz, N_HEADS, HEAD_DIM, HEAD_DIM), F32),
                   jax.ShapeDtypeStruct((bsz, N_HEADS, 1, HEAD_DIM), F32),
                   jax.ShapeDtypeStruct((bsz, N_HEADS, 1, 1), F32)),
        grid=(bsz, nc),
        in_specs=[col(COL_AQ), col(COL_AK), col(COL_AV), col(COL_AO),
                  pl.BlockSpec((None, rows, GATE_LANES), lambda b, i: (b, i, 0)),
                  pl.BlockSpec((None, SUBLANES, chunk), lambda b, i: (b, 0, i)),
                  pl.BlockSpec((None, 1, GATE_LANES), lambda b, i: (l, 0, 0)),
                  pl.BlockSpec((None, SUBLANES, 1), lambda b, i: (l, 0, 0)),
                  pl.BlockSpec((None, 1, MIX_W), lambda b, i: (l, 0, 0)),
                  st4((N_HEADS, HEAD_DIM, HEAD_DIM)), st4((N_HEADS, 1, HEAD_DIM)), st4((N_HEADS, 1, 1))],
        out_specs=(pl.BlockSpec((None, rows, MIX_W), lambda b, i: (b, i, 0)),
                   st4((N_HEADS, HEAD_DIM, HEAD_DIM)), st4((N_HEADS, 1, HEAD_DIM)), st4((N_HEADS, 1, 1))),
        scratch_shapes=[pltpu.VMEM((N_HEADS, HEAD_DIM, HEAD_DIM), F32),
                        pltpu.VMEM((N_HEADS, 1, HEAD_DIM), F32),
                        pltpu.VMEM((N_HEADS, 1, 1), F32)],
        compiler_params=_params(("parallel", "arbitrary"), V7X_VMEM_LIMIT),
        name="mlstm",
    )(proj, proj, proj, proj, gates, gates_t, bias_c, bias_r, ghead, c0, n0, m0)


def _sb_block(z, c, suffix, mask):
    sp = _softplus_neg_abs(z)
    ls = jnp.minimum(z, 0.0) - sp
    l1m = -jnp.maximum(z, 0.0) - sp
    if mask is not None:
        l1m = jnp.where(mask, l1m, 0.0)
    hi = l1m.astype(BF16)
    lo = (l1m - hi.astype(F32)).astype(BF16)
    rows, n = z.shape
    w = suffix.shape[0]
    if w == n:
        excl = _dot(hi, suffix) + _dot(lo, suffix) + c
        later = c + jnp.sum(l1m, axis=1, keepdims=True)
    else:
        nt = n // w
        tiles = [slice(t * w, (t + 1) * w) for t in range(nt)]
        ex = _dot(jnp.concatenate([hi[:, s] for s in tiles] + [lo[:, s] for s in tiles], axis=0), suffix)
        sums = [jnp.sum(l1m[:, s], axis=1, keepdims=True) for s in tiles]
        parts, later = [None] * nt, c
        for t in reversed(range(nt)):
            parts[t] = ex[t * rows:(t + 1) * rows] + ex[(nt + t) * rows:(nt + t + 1) * rows] + later
            later = later + sums[t]
        excl = jnp.concatenate(parts, axis=1)
    a = jnp.exp(ls + excl)
    if mask is not None:
        a = jnp.where(mask, a, 0.0)
    return a, later


def _strict_suffix_matrix(n):
    r = lax.broadcasted_iota(jnp.int32, (n, n), 0)
    c = lax.broadcasted_iota(jnp.int32, (n, n), 1)
    return (r > c).astype(BF16)


SB_HEADS_PER_STEP = 2


def _sb_prompt_kernel(bias_ref, q_ref, k_ref, v_ref, o_ref, *, blk):
    hp = pl.program_id(1)
    i = pl.program_id(2)
    scale = HEAD_DIM ** -0.5
    heads = range(SB_HEADS_PER_STEP)
    lanes = [slice(n * HEAD_DIM, (n + 1) * HEAD_DIM) for n in heads]
    bias = [bias_ref[hp * SB_HEADS_PER_STEP + n] for n in heads]
    qb = [q_ref[:, lanes[n]].astype(BF16) for n in heads]
    suffix = _strict_suffix_matrix(blk)
    r = lax.broadcasted_iota(jnp.int32, (blk, blk), 0)
    cidx = lax.broadcasted_iota(jnp.int32, (blk, blk), 1)

    def step(jb, carry, mask):
        start = pl.multiple_of(jb * blk, blk)
        out = []
        for n in heads:
            c, acc = carry[n]
            kb = k_ref[pl.ds(start, blk), lanes[n]].astype(BF16)
            vb = v_ref[pl.ds(start, blk), lanes[n]].astype(BF16)
            z = _dot_nt(qb[n], kb) * scale + bias[n]
            a, c = _sb_block(z, c, suffix, mask)
            out.append((c, acc + _dot(a.astype(BF16), vb)))
        return tuple(out)

    zero = (jnp.zeros((blk, 1), F32), jnp.zeros((blk, HEAD_DIM), F32))
    carry = step(i, (zero,) * SB_HEADS_PER_STEP, cidx < r)
    carry = lax.fori_loop(0, i, lambda t, ca: step(i - 1 - t, ca, None), carry)
    for n in heads:
        o_ref[:, lanes[n]] = carry[n][1].astype(o_ref.dtype)


def _sb_prompt(proj, b_sb, l, blk=256):
    bsz, t, _ = proj.shape
    blk = min(blk, t)
    width = SB_HEADS_PER_STEP * HEAD_DIM
    groups = N_HEADS // SB_HEADS_PER_STEP
    kern = functools.partial(_sb_prompt_kernel, blk=blk)
    kv = lambda c0: pl.BlockSpec((None, t, width), lambda b, h, i: (b, 0, c0 * groups + h))
    return pl.pallas_call(
        kern,
        out_shape=jax.ShapeDtypeStruct((bsz, t, MIX_W), BF16),
        grid=(bsz, groups, t // blk),
        in_specs=[pl.BlockSpec(memory_space=pltpu.SMEM),
                  pl.BlockSpec((None, blk, width), lambda b, h, i: (b, i, COL_BQ * groups + h)),
                  kv(COL_BK), kv(COL_BV)],
        out_specs=pl.BlockSpec((None, blk, width), lambda b, h, i: (b, i, h)),
        compiler_params=_params(("parallel", "parallel", "arbitrary")),
        name="sb_prompt",
    )(b_sb[l], proj, proj, proj)


def _sb_sample_kernel(pt_ref, bias_ref, q_ref, kn_ref, vn_ref, suf_ref, *refs, pages_per_step):
    npg = pages_per_step
    k_refs, v_refs = refs[:npg], refs[npg:2 * npg]
    o_ref, c_scr, acc_scr = refs[2 * npg:]
    j = pl.program_id(1)
    tp = q_ref.shape[0]
    rows = N_HEADS * tp
    flat = k_refs[0].shape[0]
    scale = HEAD_DIM ** -0.5
    by_head = lambda ref: jnp.concatenate(
        [ref[:, h * HEAD_DIM:(h + 1) * HEAD_DIM] for h in range(N_HEADS)], axis=0)
    q_all = by_head(q_ref).astype(BF16)
    bias = jnp.concatenate([jnp.full((tp, 1), bias_ref[h], F32) for h in range(N_HEADS)], axis=0)

    def block(kb, vb, suffix, mask, c, acc):
        z = _dot_nt(q_all, kb) * scale + bias
        a, c = _sb_block(z, c, suffix, mask)
        return c, acc + _dot(a.astype(BF16), vb)

    @pl.when(j == 0)
    def _():
        r = lax.broadcasted_iota(jnp.int32, (rows, HEAD_DIM), 0)
        cidx = lax.broadcasted_iota(jnp.int32, (rows, HEAD_DIM), 1)
        mask = (cidx < rows) & (cidx // tp == r // tp) & (cidx % tp < r % tp)
        c, acc = block(_pad_rows(by_head(kn_ref), HEAD_DIM).astype(BF16),
                       _pad_rows(by_head(vn_ref), HEAD_DIM).astype(BF16),
                       suf_ref[...], mask,
                       jnp.zeros(c_scr.shape, F32), jnp.zeros(acc_scr.shape, F32))
        c_scr[...] = c
        acc_scr[...] = acc

    r = lax.broadcasted_iota(jnp.int32, (rows, flat), 0)
    cidx = lax.broadcasted_iota(jnp.int32, (rows, flat), 1)
    same_head = cidx % N_HEADS == r // tp
    c, acc = c_scr[...], acc_scr[...]
    for p in range(npg):
        c, acc = block(k_refs[p][...].astype(BF16), v_refs[p][...].astype(BF16), suf_ref[...], same_head, c, acc)
    c_scr[...] = c
    acc_scr[...] = acc

    @pl.when(j == pl.num_programs(1) - 1)
    def _():
        acc = acc_scr[...]
        for h in range(N_HEADS):
            o_ref[:, h * HEAD_DIM:(h + 1) * HEAD_DIM] = acc[h * tp:(h + 1) * tp].astype(o_ref.dtype)


def _sb_sample(proj, cache_k, cache_v, page_table, b_sb, l, pages_per_step=8):
    bsz, tp, _ = proj.shape
    n_pages = page_table.shape[1]
    depth, n_phys, page = cache_k.shape[:3]
    flat = page * N_HEADS
    npg = pages_per_step
    steps = n_pages // npg
    cache_k = cache_k.reshape(depth, n_phys, flat, HEAD_DIM)
    cache_v = cache_v.reshape(depth, n_phys, flat, HEAD_DIM)
    suffix = jnp.tril(jnp.ones((HEAD_DIM, HEAD_DIM), BF16), -1)

    def page_spec(p):
        def imap(b, j, pt):
            return (l, pt[b * n_pages + (n_pages - 1 - (j * npg + p))], 0, 0)
        return pl.BlockSpec((None, None, flat, HEAD_DIM), imap)

    col = lambda c: pl.BlockSpec((None, tp, MIX_W), lambda b, j, pt: (b, 0, c))
    kern = functools.partial(_sb_sample_kernel, pages_per_step=npg)
    return pl.pallas_call(
        kern,
        out_shape=jax.ShapeDtypeStruct((bsz, tp, MIX_W), BF16),
        grid_spec=pltpu.PrefetchScalarGridSpec(
            num_scalar_prefetch=1,
            grid=(bsz, steps),
            in_specs=[pl.BlockSpec(memory_space=pltpu.SMEM), col(COL_BQ), col(COL_BK), col(COL_BV),
                      pl.BlockSpec((HEAD_DIM, HEAD_DIM), lambda b, j, pt: (0, 0))]
                     + [page_spec(p) for p in range(npg)] * 2,
            out_specs=pl.BlockSpec((None, tp, MIX_W), lambda b, j, pt: (b, 0, 0)),
            scratch_shapes=[pltpu.VMEM((N_HEADS * tp, 1), F32), pltpu.VMEM((N_HEADS * tp, HEAD_DIM), F32)]),
        compiler_params=_params(("parallel", "arbitrary"), V7X_VMEM_LIMIT),
        name="sb_sample",
    )(page_table.reshape(-1), b_sb[l], proj, proj, proj, suffix, *([cache_k] * npg), *([cache_v] * npg))


def _cmul(ar, ai, br, bi):
    return ar * br - ai * bi, ar * bi + ai * br


def _s5_prep_kernel(lre_ref, lim_ref, ls_ref, bre_ref, bim_ref, tab_ref, bbre_ref, bbim_ref):
    lam_re, lam_im = lre_ref[...], lim_ref[...]
    step = jnp.exp(ls_ref[...])
    decay = jnp.exp(lam_re * step)
    a_re = decay * jnp.cos(lam_im * step)
    a_im = decay * jnp.sin(lam_im * step)
    inv = 1.0 / (lam_re * lam_re + lam_im * lam_im)
    f_re = ((a_re - 1.0) * lam_re + a_im * lam_im) * inv
    f_im = (a_im * lam_re - (a_re - 1.0) * lam_im) * inv
    b_re, b_im = bre_ref[...], bim_ref[...]
    bbre_ref[...] = f_re * b_re - f_im * b_im
    bbim_ref[...] = f_re * b_im + f_im * b_re

    pw = {1: (a_re, a_im)}
    pw[2] = _cmul(*pw[1], *pw[1])
    pw[3] = _cmul(*pw[2], *pw[1])
    pw[4] = _cmul(*pw[2], *pw[2])
    pw[5] = _cmul(*pw[4], *pw[1])
    pw[6] = _cmul(*pw[4], *pw[2])
    pw[7] = _cmul(*pw[4], *pw[3])
    pw[8] = _cmul(*pw[4], *pw[4])
    row = lax.broadcasted_iota(jnp.int32, (SUBLANES, lam_re.shape[1]), 0)
    for part in range(2):
        carry = jnp.zeros(row.shape, F32)
        for r in range(SUBLANES):
            carry = jnp.where(row == r, pw[r + 1][part], carry)
        tab_ref[part] = carry
        for idx, k in enumerate((1, 2, 4)):
            tab_ref[2 + 2 * idx + part] = jnp.where(row >= k, pw[k][part], 0.0)


def _s5_prep(lam_re, lam_im, log_step, b_re, b_im):
    depth = lam_re.shape[0]
    flat = lambda a: a.reshape(depth, 1, S5_WIDTH)
    bt = lambda a: a.transpose(0, 3, 1, 2).reshape(depth, S5_GROUP_CH, S5_WIDTH)
    step = jnp.broadcast_to(log_step[:, :, None], (depth, S5_GROUPS, S5_STATE))
    vec = pl.BlockSpec((None, 1, S5_WIDTH), lambda l: (l, 0, 0))
    mat = pl.BlockSpec((None, S5_GROUP_CH, S5_WIDTH), lambda l: (l, 0, 0))
    return pl.pallas_call(
        _s5_prep_kernel,
        out_shape=(jax.ShapeDtypeStruct((depth, 8, SUBLANES, S5_WIDTH), F32),
                   jax.ShapeDtypeStruct((depth, S5_GROUP_CH, S5_WIDTH), F32),
                   jax.ShapeDtypeStruct((depth, S5_GROUP_CH, S5_WIDTH), F32)),
        grid=(depth,),
        in_specs=[vec, vec, vec, mat, mat],
        out_specs=(pl.BlockSpec((None, 8, SUBLANES, S5_WIDTH), lambda l: (l, 0, 0, 0)), mat, mat),
        compiler_params=_params(("parallel",)),
        name="s5_prep",
    )(flat(lam_re), flat(lam_im), flat(step), bt(b_re), bt(b_im))


def _s5_kernel(u_ref, wb_ref, tab_ref, hre_ref, him_ref, wcre_ref, wcim_ref, d_ref, wglu_ref,
               y_ref, sre_out, sim_out, sre, sim, cre, cim, *, valid_last, slab):
    ci = pl.program_id(1)
    tc = u_ref.shape[0]
    sub = MIX_W // S5_SUPER
    wid = S5_WIDTH // S5_SUPER

    @pl.when(ci == 0)
    def _():
        cre[...] = jnp.broadcast_to(hre_ref[...], cre.shape)
        cim[...] = jnp.broadcast_to(him_ref[...], cim.shape)

    u = u_ref[...]
    ub = u.astype(BF16)
    for g in range(S5_SUPER):
        bu = _dot(ub[:, g * sub:(g + 1) * sub], wb_ref[g])
        sre[:, g * wid:(g + 1) * wid] = bu[:, :wid]
        sim[:, g * wid:(g + 1) * wid] = bu[:, wid:]

    for s0 in range(0, S5_WIDTH, slab):
        lanes = slice(s0, s0 + slab)
        pr, pi = tab_ref[0, :, lanes], tab_ref[1, :, lanes]
        levels = [(k, tab_ref[2 + 2 * idx, :, lanes], tab_ref[3 + 2 * idx, :, lanes])
                  for idx, k in enumerate((1, 2, 4))]

        def body(r, carry, lanes=lanes, pr=pr, pi=pi, levels=levels):
            c_re, c_im = carry
            row = pl.multiple_of(r * SUBLANES, SUBLANES)
            xr = sre[pl.ds(row, SUBLANES), lanes]
            xi = sim[pl.ds(row, SUBLANES), lanes]
            for k, mr, mi in levels:
                rr = pltpu.roll(xr, k, axis=0)
                ri = pltpu.roll(xi, k, axis=0)
                xr, xi = xr + mr * rr - mi * ri, xi + mr * ri + mi * rr
            xr, xi = xr + pr * c_re - pi * c_im, xi + pr * c_im + pi * c_re
            sre[pl.ds(row, SUBLANES), lanes] = xr
            sim[pl.ds(row, SUBLANES), lanes] = xi
            last = SUBLANES - 1
            return (jnp.broadcast_to(xr[last:last + 1, :], xr.shape),
                    jnp.broadcast_to(xi[last:last + 1, :], xi.shape))

        nblk = tc // SUBLANES
        c_re, c_im = lax.fori_loop(0, nblk, body, (cre[:, lanes], cim[:, lanes]), unroll=min(2, nblk))
        cre[:, lanes] = c_re
        cim[:, lanes] = c_im

    s_re_b = sre[...].astype(BF16)
    s_im_b = sim[...].astype(BF16)
    y = jnp.concatenate(
        [_dot(s_re_b[:, g * wid:(g + 1) * wid], wcre_ref[g]) - _dot(s_im_b[:, g * wid:(g + 1) * wid], wcim_ref[g])
         for g in range(S5_SUPER)], axis=1)
    y = y + d_ref[...] * u
    y = 0.5 * y * (1.0 + jnp.tanh(math.sqrt(2.0 / math.pi) * (y + 0.044715 * (y * y * y))))
    yy = _dot(y.astype(BF16), wglu_ref[...])
    y_ref[...] = (yy[:, :MIX_W] * jax.nn.sigmoid(yy[:, MIX_W:])).astype(y_ref.dtype)

    @pl.when(ci == pl.num_programs(1) - 1)
    def _():
        sre_out[...] = sre[valid_last - 1:valid_last, :]
        sim_out[...] = sim[valid_last - 1:valid_last, :]


def _s5(proj, wb, tab, h_re, h_im, wc_re, wc_im, d_skip, w_glu, l, tc, valid_last, slab=512):
    bsz, t, _ = proj.shape
    tc = min(tc, t)
    layer = lambda s: pl.BlockSpec((None,) + s, lambda b, i: (l,) + (0,) * len(s))
    st = pl.BlockSpec((None, 1, S5_WIDTH), lambda b, i: (b, 0, 0))
    kern = functools.partial(_s5_kernel, valid_last=valid_last, slab=slab)
    wid = S5_WIDTH // S5_SUPER
    return pl.pallas_call(
        kern,
        out_shape=(jax.ShapeDtypeStruct((bsz, t, MIX_W), BF16),
                   jax.ShapeDtypeStruct((bsz, 1, S5_WIDTH), F32),
                   jax.ShapeDtypeStruct((bsz, 1, S5_WIDTH), F32)),
        grid=(bsz, t // tc),
        in_specs=[pl.BlockSpec((None, tc, MIX_W), lambda b, i: (b, i, COL_SU)),
                  layer((S5_SUPER, MIX_W // S5_SUPER, 2 * wid)),
                  layer((8, SUBLANES, S5_WIDTH)),
                  st, st,
                  layer((S5_SUPER, wid, MIX_W // S5_SUPER)), layer((S5_SUPER, wid, MIX_W // S5_SUPER)),
                  layer((1, MIX_W)), layer((MIX_W, 2 * MIX_W))],
        out_specs=(pl.BlockSpec((None, tc, MIX_W), lambda b, i: (b, i, 0)), st, st),
        scratch_shapes=[pltpu.VMEM((tc, S5_WIDTH), F32), pltpu.VMEM((tc, S5_WIDTH), F32),
                        pltpu.VMEM((SUBLANES, S5_WIDTH), F32), pltpu.VMEM((SUBLANES, S5_WIDTH), F32)],
        compiler_params=_params(("parallel", "arbitrary"), V7X_VMEM_LIMIT),
        name="s5",
    )(proj, wb, tab, h_re, h_im, wc_re, wc_im, d_skip, w_glu)


def _block_diag_groups(w):
    depth, _, a, b = w.shape
    per = S5_GROUPS // S5_SUPER
    w = w.reshape(depth, S5_SUPER, per, a, b)
    eye = jnp.eye(per, dtype=w.dtype)
    bd = w[:, :, :, :, None, :] * eye[None, None, :, None, :, None]
    return bd.reshape(depth, S5_SUPER, per * a, per * b)


def _conv_kernel(gb_ref, gc_ref, xv_ref, hc_ref, hx_ref, buf_ref, w_ref, y_ref, new_ref, *, valid_last):
    ci = pl.program_id(1)
    tc = gb_ref.shape[0]
    z = gc_ref[...] * xv_ref[...]
    prev = jnp.where(ci == 0, buf_ref[...], hc_ref[...] * hx_ref[...])
    zz = jnp.concatenate([prev, z], axis=0)
    w = w_ref[...]
    y = sum(w[j:j + 1, :] * zz[SUBLANES - (CONV_K - 1) + j:SUBLANES - (CONV_K - 1) + j + tc] for j in range(CONV_K))
    y_ref[...] = (gb_ref[...] * y).astype(y_ref.dtype)

    @pl.when(ci == pl.num_programs(1) - 1)
    def _():
        end = SUBLANES + valid_last
        new_ref[...] = zz[end - (CONV_K - 1):end]


def _conv(proj, buf8, w, l, tc, valid_last):
    bsz, t, _ = proj.shape
    tc = min(tc, t)
    per = tc // SUBLANES
    col = lambda c: pl.BlockSpec((None, tc, MIX_W), lambda b, i: (b, i, c))
    halo = lambda c: pl.BlockSpec((None, SUBLANES, MIX_W), lambda b, i: (b, jnp.maximum(i * per - 1, 0), c))
    kern = functools.partial(_conv_kernel, valid_last=valid_last)
    return pl.pallas_call(
        kern,
        out_shape=(jax.ShapeDtypeStruct((bsz, t, MIX_W), BF16),
                   jax.ShapeDtypeStruct((bsz, CONV_K - 1, MIX_W), F32)),
        grid=(bsz, t // tc),
        in_specs=[col(COL_CB), col(COL_CC), col(COL_CX), halo(COL_CC), halo(COL_CX),
                  pl.BlockSpec((None, SUBLANES, MIX_W), lambda b, i: (b, 0, 0)),
                  pl.BlockSpec((None, CONV_K, MIX_W), lambda b, i: (l, 0, 0))],
        out_specs=(pl.BlockSpec((None, tc, MIX_W), lambda b, i: (b, i, 0)),
                   pl.BlockSpec((None, CONV_K - 1, MIX_W), lambda b, i: (b, 0, 0))),
        compiler_params=_params(("parallel", "arbitrary")),
        name="short_conv",
    )(proj, proj, proj, proj, proj, buf8, w)


def _layer(x, mod, wts, l, past, sizes):
    bsz, t, d = x.shape
    sh1, sc1, g1, sh2, sc2, g2 = mod
    tm = sizes["tm"]
    flat = sizes["flat"]
    as_mm = lambda a: a.reshape(flat + a.shape[2:])
    as_seq = lambda a: a.reshape((bsz, t) + a.shape[2:])
    mm_mod = lambda a: a if a.shape[1] == 1 else as_mm(a)

    h = _norm_mod(x, wts["g_norm1"], l, sc1, sh1)
    hm = as_mm(h)
    proj = as_seq(_linear(hm, wts["w_in"], l, F32, tm, 512, "in_proj"))
    gates = as_seq(_linear(hm, wts["w_if"], l, F32, tm, GATE_LANES, "gate_proj"))
    gates_t = jnp.swapaxes(gates[:, :, :SUBLANES], 1, 2)
    if sizes["chunk"] > t:
        gates_t = jnp.pad(gates_t, ((0, 0), (0, 0), (0, sizes["chunk"] - t)))

    y_a, mc, mn, mm = _mlstm(proj, gates, gates_t, wts["gate_bias_c"], wts["gate_bias_r"], wts["g_head"], l,
                             past["mlstm_c"], past["mlstm_n"], past["mlstm_m"],
                             sizes["chunk"], sizes["rows"], sizes["valid"])
    if "sb_k" in past:
        y_b = _sb_sample(proj, past["sb_k"], past["sb_v"], past["page_table"], wts["b_sb"], l)
    else:
        y_b = _sb_prompt(proj, wts["b_sb"], l)
    y_s, s_re, s_im = _s5(proj, wts["s5_wb"], wts["s5_tab"], past["s5_re"], past["s5_im"],
                          wts["s5_wc_re"], wts["s5_wc_im"], wts["s5_d"], wts["w_s5_glu"], l,
                          sizes["tc"], sizes["valid_last"])
    y_c, conv_new = _conv(proj, past["conv"], wts["conv_w"], l, sizes["tc"], sizes["valid_last"])

    merged = _gate_merge(hm, wts["w_gate"], wts["b_gate"], [as_mm(y) for y in (y_a, y_b, y_s, y_c)],
                         wts["w_branch"], l, tm, 256)
    x = as_seq(_linear_residual(merged, wts["w_out"], l, as_mm(x), mm_mod(g1), tm, 1024, 1, "out_proj"))

    h2 = as_mm(_norm_mod(x, wts["g_norm2"], l, sc2, sh2))
    hidden = _ffn_up(h2, wts["w_ffn_gate"], wts["w_ffn_up"], l, tm, 512)
    x = as_seq(_linear_residual(hidden, wts["w_ffn_down"], l, as_mm(x), mm_mod(g2), tm, 1024, 2, "ffn_down"))

    nv = sizes["real"]
    kb = proj[:, :nv, COL_BK * MIX_W:(COL_BK + 1) * MIX_W].reshape(bsz, nv, N_HEADS, HEAD_DIM)
    vb = proj[:, :nv, COL_BV * MIX_W:(COL_BV + 1) * MIX_W].reshape(bsz, nv, N_HEADS, HEAD_DIM)
    state = {"sb_k": kb, "sb_v": vb, "mlstm_c": mc, "mlstm_n": mn.reshape(bsz, N_HEADS, HEAD_DIM),
             "mlstm_m": mm.reshape(bsz, N_HEADS),
             "s5_re": s_re.reshape(bsz, S5_GROUPS, S5_STATE), "s5_im": s_im.reshape(bsz, S5_GROUPS, S5_STATE),
             "conv": conv_new}
    return x, state


def kernel(x_prompt, x_sample, cache_sb_k, cache_sb_v, state_mlstm_c, state_mlstm_n, state_mlstm_m, state_s5_re, state_s5_im, state_conv, page_table, c_prompt, c_sample, w_ada, b_ada, g_norm1, g_norm2, w_in, b_mlstm_i, b_mlstm_f, g_mlstm_head, b_sb, s5_lambda_re, s5_lambda_im, s5_b_re, s5_b_im, s5_c_re, s5_c_im, s5_d, s5_log_step, w_s5_glu, conv_w, w_gate, b_gate, w_branch, w_out, w_ffn_gate, w_ffn_up, w_ffn_down, g_final):
    depth = w_in.shape[0]
    bp, tp, d = x_prompt.shape
    bs, ts, _ = x_sample.shape
    tpad = SUBLANES
    n_gate = 2 * N_HEADS
    split = 4 * MIX_W

    gate_bias = jnp.concatenate([b_mlstm_i, b_mlstm_f], axis=1)
    wts = {
        "g_norm1": g_norm1.reshape(depth, 1, d), "g_norm2": g_norm2.reshape(depth, 1, d),
        "w_in": jnp.concatenate([w_in[:, :, :split], w_in[:, :, split + n_gate:]], axis=2).astype(BF16),
        "w_if": jnp.pad(w_in[:, :, split:split + n_gate], ((0, 0), (0, 0), (0, GATE_LANES - n_gate))).astype(BF16),
        "gate_bias_c": jnp.pad(gate_bias, ((0, 0), (0, GATE_LANES - n_gate))).reshape(depth, 1, GATE_LANES),
        "gate_bias_r": gate_bias.reshape(depth, n_gate, 1),
        "g_head": g_mlstm_head.reshape(depth, 1, MIX_W),
        "b_sb": b_sb,
        "s5_d": s5_d.reshape(depth, 1, MIX_W),
        "w_s5_glu": w_s5_glu.astype(BF16),
        "conv_w": conv_w,
        "w_gate": w_gate, "b_gate": b_gate.reshape(depth, 1, N_BRANCH * d), "w_branch": w_branch,
        "w_out": w_out.astype(BF16),
        "w_ffn_gate": w_ffn_gate, "w_ffn_up": w_ffn_up,
        "w_ffn_down": w_ffn_down.astype(BF16),
    }
    tab, bb_re, bb_im = _s5_prep(s5_lambda_re, s5_lambda_im, s5_log_step, s5_b_re, s5_b_im)
    wts["s5_tab"] = tab
    to_gcp = lambda a: a.reshape(depth, S5_GROUP_CH, S5_GROUPS, S5_STATE).transpose(0, 2, 1, 3)
    wts["s5_wb"] = jnp.concatenate([_block_diag_groups(to_gcp(bb_re)), _block_diag_groups(to_gcp(bb_im))],
                                   axis=3).astype(BF16)
    wts["s5_wc_re"] = _block_diag_groups(s5_c_re.transpose(0, 1, 3, 2)).astype(BF16)
    wts["s5_wc_im"] = _block_diag_groups(s5_c_im.transpose(0, 1, 3, 2)).astype(BF16)

    c_all = jnp.concatenate([c_prompt, c_sample], axis=0)
    c_all = jnp.pad(c_all, ((0, -c_all.shape[0] % SUBLANES), (0, 0)))
    mod_all = _ada_all(c_all, w_ada, b_ada).reshape(depth, c_all.shape[0], 6, d)

    sizes_p = {"tm": 1024, "flat": (bp, tp), "chunk": 256, "rows": 256, "valid": 256, "tc": 512, "valid_last": 512, "real": tp}
    sizes_s = {"tm": bs * tpad, "flat": (1, bs * tpad), "chunk": 128, "rows": tpad, "valid": ts, "tc": tpad,
               "valid_last": ts, "real": ts}

    xp = x_prompt
    xs = jnp.pad(x_sample, ((0, 0), (0, tpad - ts), (0, 0)))
    zeros_p = {
        "mlstm_c": jnp.zeros((bp, N_HEADS, HEAD_DIM, HEAD_DIM), F32),
        "mlstm_n": jnp.zeros((bp, N_HEADS, 1, HEAD_DIM), F32),
        "mlstm_m": jnp.zeros((bp, N_HEADS, 1, 1), F32),
        "s5_re": jnp.zeros((bp, 1, S5_WIDTH), F32), "s5_im": jnp.zeros((bp, 1, S5_WIDTH), F32),
        "conv": jnp.zeros((bp, SUBLANES, MIX_W), F32),
    }
    new_p, new_s = [], []
    for l in range(depth):
        mod_p = [mod_all[l, :bp, i].reshape(bp, 1, d) for i in range(6)]
        mod_s = [jnp.repeat(mod_all[l, bp:bp + bs, i], tpad, axis=0).reshape(bs, tpad, d) for i in range(6)]
        past_s = {
            "sb_k": cache_sb_k, "sb_v": cache_sb_v, "page_table": page_table,
            "mlstm_c": state_mlstm_c[l], "mlstm_n": state_mlstm_n[l].reshape(bs, N_HEADS, 1, HEAD_DIM),
            "mlstm_m": state_mlstm_m[l].reshape(bs, N_HEADS, 1, 1),
            "s5_re": state_s5_re[l].reshape(bs, 1, S5_WIDTH), "s5_im": state_s5_im[l].reshape(bs, 1, S5_WIDTH),
            "conv": jnp.pad(state_conv[l], ((0, 0), (SUBLANES - (CONV_K - 1), 0), (0, 0))),
        }
        xp, st_p = _layer(xp, mod_p, wts, l, zeros_p, sizes_p)
        xs, st_s = _layer(xs, mod_s, wts, l, past_s, sizes_s)
        new_p.append(st_p)
        new_s.append(st_s)

    y_prompt = _final_norm(xp, g_final.reshape(1, d))
    y_sample = _final_norm(xs, g_final.reshape(1, d))[:, :ts]
    stk = lambda states, name: jnp.stack([s[name] for s in states])
    return (y_prompt, y_sample,
            stk(new_p, "sb_k"), stk(new_p, "sb_v"), stk(new_s, "sb_k"), stk(new_s, "sb_v"),
            stk(new_p, "mlstm_c"), stk(new_p, "mlstm_n"), stk(new_p, "mlstm_m"),
            stk(new_s, "mlstm_c"), stk(new_s, "mlstm_n"), stk(new_s, "mlstm_m"),
            stk(new_p, "s5_re"), stk(new_p, "s5_im"), stk(new_s, "s5_re"), stk(new_s, "s5_im"),
            stk(new_p, "conv"), stk(new_s, "conv"))
```

```python
import functools
import math

import jax
import jax.numpy as jnp
from jax import lax
from jax.experimental import pallas as pl
from jax.experimental.pallas import tpu as pltpu

F32 = jnp.float32
BF16 = jnp.bfloat16

N_HEADS = 4
HEAD_DIM = 128
MIX_W = N_HEADS * HEAD_DIM
N_BRANCH = 4
S5_GROUP_CH = 16
S5_GROUPS = MIX_W // S5_GROUP_CH
S5_STATE = 64
S5_WIDTH = S5_GROUPS * S5_STATE
S5_SUPER = 4
CONV_K = 3
EPS = 1e-6
SUBLANES = 8
LANES = 128
GATE_LANES = 128
V7X_VMEM_LIMIT = 56 * 1024 * 1024

COL_AQ, COL_AK, COL_AV, COL_AO, COL_BQ, COL_BK, COL_BV, COL_SU, COL_CB, COL_CC, COL_CX = range(11)
N_PROJ_BLOCKS = 11

NT_DIMS = (((1,), (1,)), ((), ()))
TN_DIMS = (((0,), (0,)), ((), ()))


def _dot(a, b):
    return jnp.dot(a, b, preferred_element_type=F32)


def _dot_nt(a, b):
    return lax.dot_general(a, b, NT_DIMS, preferred_element_type=F32)


def _dot_exact(a, b):
    return jnp.dot(a, b, precision=lax.Precision.HIGHEST, preferred_element_type=F32)


def _softplus_neg_abs(z):
    return jnp.log(1.0 + jnp.exp(-jnp.abs(z)))


def _log_sigmoid(z):
    return jnp.minimum(z, 0.0) - _softplus_neg_abs(z)


def _pad_rows(x, rows):
    if x.shape[0] == rows:
        return x
    return jnp.concatenate([x, jnp.zeros((rows - x.shape[0],) + x.shape[1:], x.dtype)], axis=0)


def _params(sem, vmem=None):
    return pltpu.CompilerParams(dimension_semantics=sem, vmem_limit_bytes=vmem)


def _ada_kernel(c_ref, w_ref, b_ref, o_ref):
    c = c_ref[...]
    a = (c * jax.nn.sigmoid(c)).astype(BF16)
    o_ref[...] = _dot(a, w_ref[...].astype(BF16)) + b_ref[...]


def _ada_all(c, w_ada, b_ada, tn=1024):
    depth, d, n = w_ada.shape
    rows = c.shape[0]
    return pl.pallas_call(
        _ada_kernel,
        out_shape=jax.ShapeDtypeStruct((depth, rows, n), F32),
        grid=(depth, n // tn),
        in_specs=[pl.BlockSpec((rows, d), lambda l, j: (0, 0)),
                  pl.BlockSpec((None, d, tn), lambda l, j: (l, 0, j)),
                  pl.BlockSpec((None, 1, tn), lambda l, j: (l, 0, j))],
        out_specs=pl.BlockSpec((None, rows, tn), lambda l, j: (l, 0, j)),
        compiler_params=_params(("parallel", "parallel"), V7X_VMEM_LIMIT),
        name="ada_mod",
    )(c, w_ada, b_ada.reshape(depth, 1, n))


def _norm_mod_kernel(x_ref, g_ref, sc_ref, sh_ref, o_ref):
    x = x_ref[...]
    y = x * lax.rsqrt(jnp.mean(x * x, axis=-1, keepdims=True) + EPS) * g_ref[...]
    o_ref[...] = (y * (1.0 + sc_ref[...]) + sh_ref[...]).astype(o_ref.dtype)


def _norm_kernel(x_ref, g_ref, o_ref):
    x = x_ref[...]
    o_ref[...] = x * lax.rsqrt(jnp.mean(x * x, axis=-1, keepdims=True) + EPS) * g_ref[...]


def _row_spec(arr, tm, width, col):
    if arr.shape[1] == 1:
        return pl.BlockSpec((None, 1, width), lambda b, i, *r: (b, 0, col(*r)))
    return pl.BlockSpec((None, tm, width), lambda b, i, *r: (b, i, col(*r)))


def _norm_mod(x, g, l, sc, sh, tm=512):
    bsz, t, d = x.shape
    tm = min(tm, t)
    zero = lambda *r: 0
    return pl.pallas_call(
        _norm_mod_kernel,
        out_shape=jax.ShapeDtypeStruct((bsz, t, d), BF16),
        grid=(bsz, t // tm),
        in_specs=[pl.BlockSpec((None, tm, d), lambda b, i: (b, i, 0)),
                  pl.BlockSpec((None, 1, d), lambda b, i: (l, 0, 0)),
                  _row_spec(sc, tm, d, zero), _row_spec(sh, tm, d, zero)],
        out_specs=pl.BlockSpec((None, tm, d), lambda b, i: (b, i, 0)),
        compiler_params=_params(("parallel", "parallel")),
        name="norm_mod",
    )(x, g, sc, sh)


def _final_norm(x, g, tm=512):
    bsz, t, d = x.shape
    tm = min(tm, t)
    return pl.pallas_call(
        _norm_kernel,
        out_shape=jax.ShapeDtypeStruct((bsz, t, d), F32),
        grid=(bsz, t // tm),
        in_specs=[pl.BlockSpec((None, tm, d), lambda b, i: (b, i, 0)),
                  pl.BlockSpec((1, d), lambda b, i: (0, 0))],
        out_specs=pl.BlockSpec((None, tm, d), lambda b, i: (b, i, 0)),
        compiler_params=_params(("parallel", "parallel")),
        name="final_norm",
    )(x, g)


def _mm_kernel(a_ref, w_ref, o_ref):
    o_ref[...] = _dot(a_ref[...], w_ref[...]).astype(o_ref.dtype)


def _linear(a, w, l, out_dtype, tm, tn, name):
    bsz, t, k = a.shape
    n = w.shape[-1]
    tm, tn = min(tm, t), min(tn, n)
    return pl.pallas_call(
        _mm_kernel,
        out_shape=jax.ShapeDtypeStruct((bsz, t, n), out_dtype),
        grid=(bsz, t // tm, n // tn),
        in_specs=[pl.BlockSpec((None, tm, k), lambda b, i, j: (b, i, 0)),
                  pl.BlockSpec((None, k, tn), lambda b, i, j: (l, 0, j))],
        out_specs=pl.BlockSpec((None, tm, tn), lambda b, i, j: (b, i, j)),
        compiler_params=_params(("parallel", "parallel", "parallel"), V7X_VMEM_LIMIT),
        name=name,
    )(a, w)


def _mm_res_kernel(a_ref, w_ref, x_ref, g_ref, o_ref, *acc, nk):
    part = _dot(a_ref[...], w_ref[...])
    if nk == 1:
        o_ref[...] = x_ref[...] + g_ref[...] * part
        return
    acc_ref, = acc
    kk = pl.program_id(3)

    @pl.when(kk == 0)
    def _():
        acc_ref[...] = part

    @pl.when(kk > 0)
    def _():
        acc_ref[...] += part

    @pl.when(kk == nk - 1)
    def _():
        o_ref[...] = x_ref[...] + g_ref[...] * acc_ref[...]


def _linear_residual(a, w, l, x, g, tm, tn, nk, name):
    bsz, t, k = a.shape
    n = w.shape[-1]
    tm, tn = min(tm, t), min(tn, n)
    tk = k // nk
    return pl.pallas_call(
        functools.partial(_mm_res_kernel, nk=nk),
        out_shape=jax.ShapeDtypeStruct((bsz, t, n), F32),
        grid=(bsz, t // tm, n // tn, nk),
        in_specs=[pl.BlockSpec((None, tm, tk), lambda b, i, j, kk: (b, i, kk)),
                  pl.BlockSpec((None, tk, tn), lambda b, i, j, kk: (l, kk, j)),
                  pl.BlockSpec((None, tm, tn), lambda b, i, j, kk: (b, i, j)),
                  _row_spec(g, tm, tn, lambda j, kk: j)],
        out_specs=pl.BlockSpec((None, tm, tn), lambda b, i, j, kk: (b, i, j)),
        scratch_shapes=[pltpu.VMEM((tm, tn), F32)] if nk > 1 else [],
        compiler_params=_params(("parallel", "parallel", "parallel", "arbitrary"), V7X_VMEM_LIMIT),
        name=name,
    )(a, w, x, g)


def _ffn_up_kernel(a_ref, wg_ref, wu_ref, o_ref):
    a = a_ref[...]
    gate = _dot(a, wg_ref[...].astype(BF16))
    o_ref[...] = (gate * jax.nn.sigmoid(gate) * _dot(a, wu_ref[...].astype(BF16))).astype(o_ref.dtype)


def _ffn_up(a, wg, wu, l, tm, tn):
    bsz, t, k = a.shape
    n = wg.shape[-1]
    tm, tn = min(tm, t), min(tn, n)
    wspec = pl.BlockSpec((None, k, tn), lambda b, i, j: (l, 0, j))
    return pl.pallas_call(
        _ffn_up_kernel,
        out_shape=jax.ShapeDtypeStruct((bsz, t, n), BF16),
        grid=(bsz, t // tm, n // tn),
        in_specs=[pl.BlockSpec((None, tm, k), lambda b, i, j: (b, i, 0)), wspec, wspec],
        out_specs=pl.BlockSpec((None, tm, tn), lambda b, i, j: (b, i, j)),
        compiler_params=_params(("parallel", "parallel", "parallel"), V7X_VMEM_LIMIT),
        name="ffn_up",
    )(a, wg, wu)


def _gate_merge_kernel(h_ref, *refs):
    wg_refs, bg_refs, y_refs = refs[:N_BRANCH], refs[N_BRANCH:2 * N_BRANCH], refs[2 * N_BRANCH:3 * N_BRANCH]
    wb_ref, o_ref = refs[3 * N_BRANCH:]
    h = h_ref[...]
    acc = None
    for g in range(N_BRANCH):
        gate = jax.nn.sigmoid(_dot(h, wg_refs[g][...].astype(BF16)) + bg_refs[g][...])
        term = gate * _dot(y_refs[g][...], wb_ref[g].astype(BF16))
        acc = term if acc is None else acc + term
    o_ref[...] = acc.astype(o_ref.dtype)


def _gate_merge(h, wg, bg, ys, wb, l, tm, tn):
    bsz, t, k = h.shape
    d = wb.shape[-1]
    tm, tn = min(tm, t), min(tn, d)
    per = d // tn
    yspec = pl.BlockSpec((None, tm, MIX_W), lambda b, i, j: (b, i, 0))
    wspec = lambda g: pl.BlockSpec((None, k, tn), lambda b, i, j: (l, 0, g * per + j))
    bspec = lambda g: pl.BlockSpec((None, 1, tn), lambda b, i, j: (l, 0, g * per + j))
    branches = range(N_BRANCH)
    return pl.pallas_call(
        _gate_merge_kernel,
        out_shape=jax.ShapeDtypeStruct((bsz, t, d), BF16),
        grid=(bsz, t // tm, d // tn),
        in_specs=[pl.BlockSpec((None, tm, k), lambda b, i, j: (b, i, 0))]
                 + [wspec(g) for g in branches] + [bspec(g) for g in branches] + [yspec] * N_BRANCH
                 + [pl.BlockSpec((None, N_BRANCH, MIX_W, tn), lambda b, i, j: (l, 0, 0, j))],
        out_specs=pl.BlockSpec((None, tm, tn), lambda b, i, j: (b, i, j)),
        compiler_params=_params(("parallel", "parallel", "parallel"), V7X_VMEM_LIMIT),
        name="gate_merge",
    )(h, *([wg] * N_BRANCH), *([bg] * N_BRANCH), *ys, wb)


def _mlstm_kernel(q_ref, k_ref, v_ref, og_ref, gc_ref, gr_ref, bc_ref, br_ref, gh_ref, c0_ref, n0_ref, m0_ref,
                  y_ref, c_out, n_out, m_out, c_scr, n_scr, m_scr, *, chunk, valid):
    ci = pl.program_id(1)
    rows = q_ref.shape[0]

    @pl.when(ci == 0)
    def _():
        c_scr[...] = c0_ref[...]
        n_scr[...] = n0_ref[...]
        m_scr[...] = m0_ref[...]

    pos_c = lax.broadcasted_iota(jnp.int32, (chunk, 1), 0)
    pos_r = lax.broadcasted_iota(jnp.int32, (1, chunk), 1)
    tri_r = lax.broadcasted_iota(jnp.int32, (chunk, chunk), 0)
    tri_c = lax.broadcasted_iota(jnp.int32, (chunk, chunk), 1)
    causal = tri_c <= tri_r
    lower = causal.astype(F32)
    upper = (tri_r <= tri_c).astype(F32)

    gates_c = _pad_rows(gc_ref[...], chunk) + bc_ref[...]
    gates_r = gr_ref[...] + br_ref[...]
    lf_c = _log_sigmoid(gates_c)
    lf_r = _log_sigmoid(gates_r)
    if valid < chunk:
        lf_c = jnp.where(pos_c < valid, lf_c, 0.0)
        lf_r = jnp.where(pos_r < valid, lf_r, 0.0)
        gates_c = jnp.where(pos_c < valid, gates_c, -jnp.inf)
        gates_r = jnp.where(pos_r < valid, gates_r, -jnp.inf)
    cum_c = _dot_exact(lower, lf_c)
    cum_r = _dot_exact(lf_r, upper)

    heads = range(N_HEADS)
    sls = [slice(h * HEAD_DIM, (h + 1) * HEAD_DIM) for h in heads]
    qs = [_pad_rows(q_ref[:, sl], chunk) for sl in sls]
    ks = [_pad_rows(k_ref[:, sl], chunk) * (HEAD_DIM ** -0.5) for sl in sls]
    vs = [_pad_rows(v_ref[:, sl], chunk) for sl in sls]
    qbs, kbs, vbs = ([x.astype(BF16) for x in xs] for xs in (qs, ks, vs))
    b_cs = [cum_c[:, N_HEADS + h:N_HEADS + h + 1] for h in heads]
    b_rs = [cum_r[N_HEADS + h:N_HEADS + h + 1, :] for h in heads]
    m_prevs = [m_scr[h] for h in heads]
    c_prevs = [c_scr[h] for h in heads]
    n_prevs = [n_scr[h] for h in heads]

    qk = [_dot_nt(qbs[h], kbs[h]) for h in heads]
    cq = [_dot_nt(qbs[h], c_prevs[h].astype(BF16)) for h in heads]

    s_all, w_inters, m_ts = [], [], []
    for h in heads:
        d_intra = jnp.where(causal, b_cs[h] - b_rs[h] + gates_r[h:h + 1, :], -jnp.inf)
        m_inter = b_cs[h] + m_prevs[h]
        m_t = jnp.maximum(m_inter, jnp.max(d_intra, axis=1, keepdims=True))
        s_all.append(jnp.exp(d_intra - m_t) * qk[h])
        w_inters.append(jnp.exp(m_inter - m_t))
        m_ts.append(m_t)
    sv = [_dot(s_all[h].astype(BF16), vbs[h]) for h in heads]

    for h in heads:
        num = w_inters[h] * cq[h] + sv[h]
        den = (w_inters[h] * jnp.sum(qs[h] * n_prevs[h], axis=1, keepdims=True)
               + jnp.sum(s_all[h], axis=1, keepdims=True))
        hh = num / jnp.maximum(jnp.abs(den), jnp.exp(-m_ts[h]))
        hn = hh * lax.rsqrt(jnp.mean(hh * hh, axis=-1, keepdims=True) + EPS) * gh_ref[:, sls[h]]
        y = hn[:rows] * jax.nn.sigmoid(og_ref[:, sls[h]])
        y_ref[:, sls[h]] = y.astype(y_ref.dtype)

    w_ks, decays, m_ends = [], [], []
    for h in heads:
        b_end = b_cs[h][chunk - 1:chunk, :]
        log_w = b_end - b_cs[h] + gates_c[:, h:h + 1]
        m_end = jnp.maximum(b_end + m_prevs[h], jnp.max(log_w, axis=0, keepdims=True))
        w_ks.append(jnp.exp(log_w - m_end))
        decays.append(jnp.exp(b_end + m_prevs[h] - m_end))
        m_ends.append(m_end)
    kv = [lax.dot_general((w_ks[h] * vs[h]).astype(BF16), kbs[h], TN_DIMS, preferred_element_type=F32)
          for h in heads]
    for h in heads:
        c_scr[h] = decays[h] * c_prevs[h] + kv[h]
        n_scr[h] = decays[h] * n_prevs[h] + jnp.sum(w_ks[h] * ks[h], axis=0, keepdims=True)
        m_scr[h] = m_ends[h]

    @pl.when(ci == pl.num_programs(1) - 1)
    def _():
        c_out[...] = c_scr[...]
        n_out[...] = n_scr[...]
        m_out[...] = m_scr[...]


def _mlstm(proj, gates, gates_t, bias_c, bias_r, ghead, l, c0, n0, m0, chunk, rows, valid):
    bsz, t, _ = proj.shape
    nc = t // rows
    col = lambda c: pl.BlockSpec((None, rows, MIX_W), lambda b, i: (b, i, c))
    st4 = lambda s: pl.BlockSpec((None,) + s, lambda b, i: (b, 0, 0, 0))
    kern = functools.partial(_mlstm_kernel, chunk=chunk, valid=valid)
    return pl.pallas_call(
        kern,
        out_shape=(jax.ShapeDtypeStruct((bsz, t, MIX_W), BF16),
                   jax.ShapeDtypeStruct((bsz, N_HEADS, HEAD_DIM, HEAD_DIM), F32),
                   jax.ShapeDtypeStruct((bsz, N_HEADS, 1, HEAD_DIM), F32),
                   jax.ShapeDtypeStruct((bsz, N_HEADS, 1, 1), F32)),
        grid=(bsz, nc),
        in_specs=[col(COL_AQ), col(COL_AK), col(COL_AV), col(COL_AO),
                  pl.BlockSpec((None, rows, GATE_LANES), lambda b, i: (b, i, 0)),
                  pl.BlockSpec((None, SUBLANES, chunk), lambda b, i: (b, 0, i)),
                  pl.BlockSpec((None, 1, GATE_LANES), lambda b, i: (l, 0, 0)),
                  pl.BlockSpec((None, SUBLANES, 1), lambda b, i: (l, 0, 0)),
                  pl.BlockSpec((None, 1, MIX_W), lambda b, i: (l, 0, 0)),
                  st4((N_HEADS, HEAD_DIM, HEAD_DIM)), st4((N_HEADS, 1, HEAD_DIM)), st4((N_HEADS, 1, 1))],
        out_specs=(pl.BlockSpec((None, rows, MIX_W), lambda b, i: (b, i, 0)),
                   st4((N_HEADS, HEAD_DIM, HEAD_DIM)), st4((N_HEADS, 1, HEAD_DIM)), st4((N_HEADS, 1, 1))),
        scratch_shapes=[pltpu.VMEM((N_HEADS, HEAD_DIM, HEAD_DIM), F32),
                        pltpu.VMEM((N_HEADS, 1, HEAD_DIM), F32),
                        pltpu.VMEM((N_HEADS, 1, 1), F32)],
        compiler_params=_params(("parallel", "arbitrary"), V7X_VMEM_LIMIT),
        name="mlstm",
    )(proj, proj, proj, proj, gates, gates_t, bias_c, bias_r, ghead, c0, n0, m0)


def _sb_blocks(zs, carries, suffix, mask, chained):
    rows, n = zs[0].shape
    w = suffix.shape[0]
    nt = n // w
    tiles = [slice(t * w, (t + 1) * w) for t in range(nt)]
    ls_all, stacked, sums = [], [], []
    for z in zs:
        sp = _softplus_neg_abs(z)
        ls_all.append(jnp.minimum(z, 0.0) - sp)
        l1m = -jnp.maximum(z, 0.0) - sp
        if mask is not None:
            l1m = jnp.where(mask, l1m, 0.0)
        hi = l1m.astype(BF16)
        lo = (l1m - hi.astype(F32)).astype(BF16)
        stacked += [hi[:, s] for s in tiles] + [lo[:, s] for s in tiles]
        sums.append([jnp.sum(l1m[:, s], axis=1, keepdims=True) for s in tiles])
    ex = _dot(jnp.concatenate(stacked, axis=0), suffix)
    weights, out = [], []
    later = carries[0]
    for u in range(len(zs)):
        if not chained:
            later = carries[u]
        base = u * 2 * nt * rows
        parts = [None] * nt
        for t in reversed(range(nt)):
            hi_rows = ex[base + t * rows:base + (t + 1) * rows]
            lo_rows = ex[base + (nt + t) * rows:base + (nt + t + 1) * rows]
            parts[t] = hi_rows + lo_rows + later
            later = later + sums[u][t]
        excl = parts[0] if nt == 1 else jnp.concatenate(parts, axis=1)
        a = jnp.exp(ls_all[u] + excl)
        if mask is not None:
            a = jnp.where(mask, a, 0.0)
        weights.append(a)
        out.append(later)
    return weights, out


def _strict_suffix_matrix(n):
    r = lax.broadcasted_iota(jnp.int32, (n, n), 0)
    c = lax.broadcasted_iota(jnp.int32, (n, n), 1)
    return (r > c).astype(BF16)


SB_HEADS_PER_STEP = 2


def _sb_prompt_kernel(bias_ref, q_ref, k_ref, v_ref, o_ref, *, blk):
    hp = pl.program_id(1)
    i = pl.program_id(2)
    scale = HEAD_DIM ** -0.5
    heads = range(SB_HEADS_PER_STEP)
    lanes = [slice(n * HEAD_DIM, (n + 1) * HEAD_DIM) for n in heads]
    bias = [bias_ref[hp * SB_HEADS_PER_STEP + n] for n in heads]
    qb = [q_ref[:, lanes[n]].astype(BF16) for n in heads]
    suffix = _strict_suffix_matrix(blk)
    r = lax.broadcasted_iota(jnp.int32, (blk, blk), 0)
    cidx = lax.broadcasted_iota(jnp.int32, (blk, blk), 1)

    def step(jb, carry, mask):
        start = pl.multiple_of(jb * blk, blk)
        zs = [_dot_nt(qb[n], k_ref[pl.ds(start, blk), lanes[n]].astype(BF16)) * scale + bias[n] for n in heads]
        ws, cs = _sb_blocks(zs, [carry[n][0] for n in heads], suffix, mask, chained=False)
        return tuple((cs[n], carry[n][1] + _dot(ws[n].astype(BF16), v_ref[pl.ds(start, blk), lanes[n]].astype(BF16)))
                     for n in heads)

    zero = (jnp.zeros((blk, 1), F32), jnp.zeros((blk, HEAD_DIM), F32))
    carry = step(i, (zero,) * SB_HEADS_PER_STEP, cidx < r)
    carry = lax.fori_loop(0, i, lambda t, ca: step(i - 1 - t, ca, None), carry)
    for n in heads:
        o_ref[:, lanes[n]] = carry[n][1].astype(o_ref.dtype)


def _sb_prompt(proj, b_sb, l, blk=256):
    bsz, t, _ = proj.shape
    blk = min(blk, t)
    width = SB_HEADS_PER_STEP * HEAD_DIM
    groups = N_HEADS // SB_HEADS_PER_STEP
    kern = functools.partial(_sb_prompt_kernel, blk=blk)
    kv = lambda c0: pl.BlockSpec((None, t, width), lambda b, h, i: (b, 0, c0 * groups + h))
    return pl.pallas_call(
        kern,
        out_shape=jax.ShapeDtypeStruct((bsz, t, MIX_W), BF16),
        grid=(bsz, groups, t // blk),
        in_specs=[pl.BlockSpec(memory_space=pltpu.SMEM),
                  pl.BlockSpec((None, blk, width), lambda b, h, i: (b, i, COL_BQ * groups + h)),
                  kv(COL_BK), kv(COL_BV)],
        out_specs=pl.BlockSpec((None, blk, width), lambda b, h, i: (b, i, h)),
        compiler_params=_params(("parallel", "parallel", "arbitrary")),
        name="sb_prompt",
    )(b_sb[l], proj, proj, proj)


def _sb_sample_kernel(pt_ref, bias_ref, q_ref, kn_ref, vn_ref, suf_ref, *refs, pages_per_step):
    npg = pages_per_step
    k_refs, v_refs = refs[:npg], refs[npg:2 * npg]
    o_ref, c_scr, acc_scr = refs[2 * npg:]
    j = pl.program_id(1)
    tp = q_ref.shape[0]
    rows = N_HEADS * tp
    flat = k_refs[0].shape[0]
    scale = HEAD_DIM ** -0.5
    by_head = lambda ref: jnp.concatenate(
        [ref[:, h * HEAD_DIM:(h + 1) * HEAD_DIM] for h in range(N_HEADS)], axis=0)
    q_all = by_head(q_ref).astype(BF16)
    bias = jnp.concatenate([jnp.full((tp, 1), bias_ref[h], F32) for h in range(N_HEADS)], axis=0)

    def blocks(k_list, v_list, mask, c, acc):
        zs = [_dot_nt(q_all, kb) * scale + bias for kb in k_list]
        ws, cs = _sb_blocks(zs, [c], suf_ref[...], mask, chained=True)
        for a, vb in zip(ws, v_list):
            acc = acc + _dot(a.astype(BF16), vb)
        return cs[-1], acc

    @pl.when(j == 0)
    def _():
        r = lax.broadcasted_iota(jnp.int32, (rows, HEAD_DIM), 0)
        cidx = lax.broadcasted_iota(jnp.int32, (rows, HEAD_DIM), 1)
        mask = (cidx < rows) & (cidx // tp == r // tp) & (cidx % tp < r % tp)
        c, acc = blocks([_pad_rows(by_head(kn_ref), HEAD_DIM).astype(BF16)],
                        [_pad_rows(by_head(vn_ref), HEAD_DIM).astype(BF16)], mask,
                        jnp.zeros(c_scr.shape, F32), jnp.zeros(acc_scr.shape, F32))
        c_scr[...] = c
        acc_scr[...] = acc

    r = lax.broadcasted_iota(jnp.int32, (rows, flat), 0)
    cidx = lax.broadcasted_iota(jnp.int32, (rows, flat), 1)
    same_head = cidx % N_HEADS == r // tp
    c, acc = blocks([k_refs[p][...].astype(BF16) for p in range(npg)],
                    [v_refs[p][...].astype(BF16) for p in range(npg)], same_head, c_scr[...], acc_scr[...])
    c_scr[...] = c
    acc_scr[...] = acc

    @pl.when(j == pl.num_programs(1) - 1)
    def _():
        acc = acc_scr[...]
        for h in range(N_HEADS):
            o_ref[:, h * HEAD_DIM:(h + 1) * HEAD_DIM] = acc[h * tp:(h + 1) * tp].astype(o_ref.dtype)


def _sb_sample(proj, cache_k, cache_v, page_table, b_sb, l, pages_per_step=16):
    bsz, tp, _ = proj.shape
    n_pages = page_table.shape[1]
    depth, n_phys, page = cache_k.shape[:3]
    flat = page * N_HEADS
    npg = pages_per_step
    steps = n_pages // npg
    cache_k = cache_k.reshape(depth, n_phys, flat, HEAD_DIM)
    cache_v = cache_v.reshape(depth, n_phys, flat, HEAD_DIM)
    suffix = jnp.tril(jnp.ones((HEAD_DIM, HEAD_DIM), BF16), -1)

    def page_spec(p):
        def imap(b, j, pt):
            return (l, pt[b * n_pages + (n_pages - 1 - (j * npg + p))], 0, 0)
        return pl.BlockSpec((None, None, flat, HEAD_DIM), imap)

    col = lambda c: pl.BlockSpec((None, tp, MIX_W), lambda b, j, pt: (b, 0, c))
    kern = functools.partial(_sb_sample_kernel, pages_per_step=npg)
    return pl.pallas_call(
        kern,
        out_shape=jax.ShapeDtypeStruct((bsz, tp, MIX_W), BF16),
        grid_spec=pltpu.PrefetchScalarGridSpec(
            num_scalar_prefetch=1,
            grid=(bsz, steps),
            in_specs=[pl.BlockSpec(memory_space=pltpu.SMEM), col(COL_BQ), col(COL_BK), col(COL_BV),
                      pl.BlockSpec((HEAD_DIM, HEAD_DIM), lambda b, j, pt: (0, 0))]
                     + [page_spec(p) for p in range(npg)] * 2,
            out_specs=pl.BlockSpec((None, tp, MIX_W), lambda b, j, pt: (b, 0, 0)),
            scratch_shapes=[pltpu.VMEM((N_HEADS * tp, 1), F32), pltpu.VMEM((N_HEADS * tp, HEAD_DIM), F32)]),
        compiler_params=_params(("parallel", "arbitrary"), V7X_VMEM_LIMIT),
        name="sb_sample",
    )(page_table.reshape(-1), b_sb[l], proj, proj, proj, suffix, *([cache_k] * npg), *([cache_v] * npg))


def _cmul(ar, ai, br, bi):
    return ar * br - ai * bi, ar * bi + ai * br


def _s5_prep_kernel(lre_ref, lim_ref, ls_ref, bre_ref, bim_ref, tab_ref, bbre_ref, bbim_ref):
    lam_re, lam_im = lre_ref[...], lim_ref[...]
    step = jnp.exp(ls_ref[...])
    decay = jnp.exp(lam_re * step)
    a_re = decay * jnp.cos(lam_im * step)
    a_im = decay * jnp.sin(lam_im * step)
    inv = 1.0 / (lam_re * lam_re + lam_im * lam_im)
    f_re = ((a_re - 1.0) * lam_re + a_im * lam_im) * inv
    f_im = (a_im * lam_re - (a_re - 1.0) * lam_im) * inv
    b_re, b_im = bre_ref[...], bim_ref[...]
    bbre_ref[...] = f_re * b_re - f_im * b_im
    bbim_ref[...] = f_re * b_im + f_im * b_re

    pw = {1: (a_re, a_im)}
    pw[2] = _cmul(*pw[1], *pw[1])
    pw[3] = _cmul(*pw[2], *pw[1])
    pw[4] = _cmul(*pw[2], *pw[2])
    pw[5] = _cmul(*pw[4], *pw[1])
    pw[6] = _cmul(*pw[4], *pw[2])
    pw[7] = _cmul(*pw[4], *pw[3])
    pw[8] = _cmul(*pw[4], *pw[4])
    row = lax.broadcasted_iota(jnp.int32, (SUBLANES, lam_re.shape[1]), 0)
    for part in range(2):
        carry = jnp.zeros(row.shape, F32)
        for r in range(SUBLANES):
            carry = jnp.where(row == r, pw[r + 1][part], carry)
        tab_ref[part] = carry
        for idx, k in enumerate((1, 2, 4)):
            tab_ref[2 + 2 * idx + part] = jnp.where(row >= k, pw[k][part], 0.0)


def _s5_prep(lam_re, lam_im, log_step, b_re, b_im):
    depth = lam_re.shape[0]
    flat = lambda a: a.reshape(depth, 1, S5_WIDTH)
    bt = lambda a: a.transpose(0, 3, 1, 2).reshape(depth, S5_GROUP_CH, S5_WIDTH)
    step = jnp.broadcast_to(log_step[:, :, None], (depth, S5_GROUPS, S5_STATE))
    vec = pl.BlockSpec((None, 1, S5_WIDTH), lambda l: (l, 0, 0))
    mat = pl.BlockSpec((None, S5_GROUP_CH, S5_WIDTH), lambda l: (l, 0, 0))
    return pl.pallas_call(
        _s5_prep_kernel,
        out_shape=(jax.ShapeDtypeStruct((depth, 8, SUBLANES, S5_WIDTH), F32),
                   jax.ShapeDtypeStruct((depth, S5_GROUP_CH, S5_WIDTH), F32),
                   jax.ShapeDtypeStruct((depth, S5_GROUP_CH, S5_WIDTH), F32)),
        grid=(depth,),
        in_specs=[vec, vec, vec, mat, mat],
        out_specs=(pl.BlockSpec((None, 8, SUBLANES, S5_WIDTH), lambda l: (l, 0, 0, 0)), mat, mat),
        compiler_params=_params(("parallel",)),
        name="s5_prep",
    )(flat(lam_re), flat(lam_im), flat(step), bt(b_re), bt(b_im))


def _s5_kernel(u_ref, wb_ref, tab_ref, hre_ref, him_ref, wcre_ref, wcim_ref, d_ref, wglu_ref,
               y_ref, sre_out, sim_out, sre, sim, cre, cim, *, valid_last, slab):
    ci = pl.program_id(1)
    tc = u_ref.shape[0]
    sub = MIX_W // S5_SUPER
    wid = S5_WIDTH // S5_SUPER

    @pl.when(ci == 0)
    def _():
        cre[...] = jnp.broadcast_to(hre_ref[...], cre.shape)
        cim[...] = jnp.broadcast_to(him_ref[...], cim.shape)

    u = u_ref[...]
    ub = u.astype(BF16)
    for g in range(S5_SUPER):
        bu = _dot(ub[:, g * sub:(g + 1) * sub], wb_ref[g])
        sre[:, g * wid:(g + 1) * wid] = bu[:, :wid]
        sim[:, g * wid:(g + 1) * wid] = bu[:, wid:]

    for s0 in range(0, S5_WIDTH, slab):
        lanes = slice(s0, s0 + slab)
        pr, pi = tab_ref[0, :, lanes], tab_ref[1, :, lanes]
        levels = [(k, tab_ref[2 + 2 * idx, :, lanes], tab_ref[3 + 2 * idx, :, lanes])
                  for idx, k in enumerate((1, 2, 4))]

        def body(r, carry, lanes=lanes, pr=pr, pi=pi, levels=levels):
            c_re, c_im = carry
            row = pl.multiple_of(r * SUBLANES, SUBLANES)
            xr = sre[pl.ds(row, SUBLANES), lanes]
            xi = sim[pl.ds(row, SUBLANES), lanes]
            for k, mr, mi in levels:
                rr = pltpu.roll(xr, k, axis=0)
                ri = pltpu.roll(xi, k, axis=0)
                xr, xi = xr + mr * rr - mi * ri, xi + mr * ri + mi * rr
            xr, xi = xr + pr * c_re - pi * c_im, xi + pr * c_im + pi * c_re
            sre[pl.ds(row, SUBLANES), lanes] = xr
            sim[pl.ds(row, SUBLANES), lanes] = xi
            last = SUBLANES - 1
            return (jnp.broadcast_to(xr[last:last + 1, :], xr.shape),
                    jnp.broadcast_to(xi[last:last + 1, :], xi.shape))

        nblk = tc // SUBLANES
        c_re, c_im = lax.fori_loop(0, nblk, body, (cre[:, lanes], cim[:, lanes]), unroll=min(2, nblk))
        cre[:, lanes] = c_re
        cim[:, lanes] = c_im

    s_re_b = sre[...].astype(BF16)
    s_im_b = sim[...].astype(BF16)
    y = jnp.concatenate(
        [_dot(s_re_b[:, g * wid:(g + 1) * wid], wcre_ref[g]) - _dot(s_im_b[:, g * wid:(g + 1) * wid], wcim_ref[g])
         for g in range(S5_SUPER)], axis=1)
    y = y + d_ref[...] * u
    y = 0.5 * y * (1.0 + jnp.tanh(math.sqrt(2.0 / math.pi) * (y + 0.044715 * (y * y * y))))
    yy = _dot(y.astype(BF16), wglu_ref[...])
    y_ref[...] = (yy[:, :MIX_W] * jax.nn.sigmoid(yy[:, MIX_W:])).astype(y_ref.dtype)

    @pl.when(ci == pl.num_programs(1) - 1)
    def _():
        sre_out[...] = sre[valid_last - 1:valid_last, :]
        sim_out[...] = sim[valid_last - 1:valid_last, :]


def _s5(proj, wb, tab, h_re, h_im, wc_re, wc_im, d_skip, w_glu, l, tc, valid_last, slab=512):
    bsz, t, _ = proj.shape
    tc = min(tc, t)
    layer = lambda s: pl.BlockSpec((None,) + s, lambda b, i: (l,) + (0,) * len(s))
    st = pl.BlockSpec((None, 1, S5_WIDTH), lambda b, i: (b, 0, 0))
    kern = functools.partial(_s5_kernel, valid_last=valid_last, slab=slab)
    wid = S5_WIDTH // S5_SUPER
    return pl.pallas_call(
        kern,
        out_shape=(jax.ShapeDtypeStruct((bsz, t, MIX_W), BF16),
                   jax.ShapeDtypeStruct((bsz, 1, S5_WIDTH), F32),
                   jax.ShapeDtypeStruct((bsz, 1, S5_WIDTH), F32)),
        grid=(bsz, t // tc),
        in_specs=[pl.BlockSpec((None, tc, MIX_W), lambda b, i: (b, i, COL_SU)),
                  layer((S5_SUPER, MIX_W // S5_SUPER, 2 * wid)),
                  layer((8, SUBLANES, S5_WIDTH)),
                  st, st,
                  layer((S5_SUPER, wid, MIX_W // S5_SUPER)), layer((S5_SUPER, wid, MIX_W // S5_SUPER)),
                  layer((1, MIX_W)), layer((MIX_W, 2 * MIX_W))],
        out_specs=(pl.BlockSpec((None, tc, MIX_W), lambda b, i: (b, i, 0)), st, st),
        scratch_shapes=[pltpu.VMEM((tc, S5_WIDTH), F32), pltpu.VMEM((tc, S5_WIDTH), F32),
                        pltpu.VMEM((SUBLANES, S5_WIDTH), F32), pltpu.VMEM((SUBLANES, S5_WIDTH), F32)],
        compiler_params=_params(("parallel", "arbitrary"), V7X_VMEM_LIMIT),
        name="s5",
    )(proj, wb, tab, h_re, h_im, wc_re, wc_im, d_skip, w_glu)


def _block_diag_groups(w):
    depth, _, a, b = w.shape
    per = S5_GROUPS // S5_SUPER
    w = w.reshape(depth, S5_SUPER, per, a, b)
    eye = jnp.eye(per, dtype=w.dtype)
    bd = w[:, :, :, :, None, :] * eye[None, None, :, None, :, None]
    return bd.reshape(depth, S5_SUPER, per * a, per * b)


def _conv_kernel(gb_ref, gc_ref, xv_ref, hc_ref, hx_ref, buf_ref, w_ref, y_ref, new_ref, *, valid_last):
    ci = pl.program_id(1)
    tc = gb_ref.shape[0]
    z = gc_ref[...] * xv_ref[...]
    prev = jnp.where(ci == 0, buf_ref[...], hc_ref[...] * hx_ref[...])
    zz = jnp.concatenate([prev, z], axis=0)
    w = w_ref[...]
    y = sum(w[j:j + 1, :] * zz[SUBLANES - (CONV_K - 1) + j:SUBLANES - (CONV_K - 1) + j + tc] for j in range(CONV_K))
    y_ref[...] = (gb_ref[...] * y).astype(y_ref.dtype)

    @pl.when(ci == pl.num_programs(1) - 1)
    def _():
        end = SUBLANES + valid_last
        new_ref[...] = zz[end - (CONV_K - 1):end]


def _conv(proj, buf8, w, l, tc, valid_last):
    bsz, t, _ = proj.shape
    tc = min(tc, t)
    per = tc // SUBLANES
    col = lambda c: pl.BlockSpec((None, tc, MIX_W), lambda b, i: (b, i, c))
    halo = lambda c: pl.BlockSpec((None, SUBLANES, MIX_W), lambda b, i: (b, jnp.maximum(i * per - 1, 0), c))
    kern = functools.partial(_conv_kernel, valid_last=valid_last)
    return pl.pallas_call(
        kern,
        out_shape=(jax.ShapeDtypeStruct((bsz, t, MIX_W), BF16),
                   jax.ShapeDtypeStruct((bsz, CONV_K - 1, MIX_W), F32)),
        grid=(bsz, t // tc),
        in_specs=[col(COL_CB), col(COL_CC), col(COL_CX), halo(COL_CC), halo(COL_CX),
                  pl.BlockSpec((None, SUBLANES, MIX_W), lambda b, i: (b, 0, 0)),
                  pl.BlockSpec((None, CONV_K, MIX_W), lambda b, i: (l, 0, 0))],
        out_specs=(pl.BlockSpec((None, tc, MIX_W), lambda b, i: (b, i, 0)),
                   pl.BlockSpec((None, CONV_K - 1, MIX_W), lambda b, i: (b, 0, 0))),
        compiler_params=_params(("parallel", "arbitrary")),
        name="short_conv",
    )(proj, proj, proj, proj, proj, buf8, w)


def _layer(x, mod, wts, l, past, sizes):
    bsz, t, d = x.shape
    sh1, sc1, g1, sh2, sc2, g2 = mod
    tm = sizes["tm"]
    flat = sizes["flat"]
    as_mm = lambda a: a.reshape(flat + a.shape[2:])
    as_seq = lambda a: a.reshape((bsz, t) + a.shape[2:])
    mm_mod = lambda a: a if a.shape[1] == 1 else as_mm(a)

    h = _norm_mod(x, wts["g_norm1"], l, sc1, sh1)
    hm = as_mm(h)
    proj = as_seq(_linear(hm, wts["w_in"], l, F32, tm, 512, "in_proj"))
    gates = as_seq(_linear(hm, wts["w_if"], l, F32, tm, GATE_LANES, "gate_proj"))
    gates_t = jnp.swapaxes(gates[:, :, :SUBLANES], 1, 2)
    if sizes["chunk"] > t:
        gates_t = jnp.pad(gates_t, ((0, 0), (0, 0), (0, sizes["chunk"] - t)))

    y_a, mc, mn, mm = _mlstm(proj, gates, gates_t, wts["gate_bias_c"], wts["gate_bias_r"], wts["g_head"], l,
                             past["mlstm_c"], past["mlstm_n"], past["mlstm_m"],
                             sizes["chunk"], sizes["rows"], sizes["valid"])
    if "sb_k" in past:
        y_b = _sb_sample(proj, past["sb_k"], past["sb_v"], past["page_table"], wts["b_sb"], l)
    else:
        y_b = _sb_prompt(proj, wts["b_sb"], l)
    y_s, s_re, s_im = _s5(proj, wts["s5_wb"], wts["s5_tab"], past["s5_re"], past["s5_im"],
                          wts["s5_wc_re"], wts["s5_wc_im"], wts["s5_d"], wts["w_s5_glu"], l,
                          sizes["tc"], sizes["valid_last"])
    y_c, conv_new = _conv(proj, past["conv"], wts["conv_w"], l, sizes["tc"], sizes["valid_last"])

    merged = _gate_merge(hm, wts["w_gate"], wts["b_gate"], [as_mm(y) for y in (y_a, y_b, y_s, y_c)],
                         wts["w_branch"], l, tm, 256)
    x = as_seq(_linear_residual(merged, wts["w_out"], l, as_mm(x), mm_mod(g1), tm, 1024, 1, "out_proj"))

    h2 = as_mm(_norm_mod(x, wts["g_norm2"], l, sc2, sh2))
    hidden = _ffn_up(h2, wts["w_ffn_gate"], wts["w_ffn_up"], l, tm, 512)
    x = as_seq(_linear_residual(hidden, wts["w_ffn_down"], l, as_mm(x), mm_mod(g2), tm, 1024, 2, "ffn_down"))

    nv = sizes["real"]
    kb = proj[:, :nv, COL_BK * MIX_W:(COL_BK + 1) * MIX_W].reshape(bsz, nv, N_HEADS, HEAD_DIM)
    vb = proj[:, :nv, COL_BV * MIX_W:(COL_BV + 1) * MIX_W].reshape(bsz, nv, N_HEADS, HEAD_DIM)
    state = {"sb_k": kb, "sb_v": vb, "mlstm_c": mc, "mlstm_n": mn.reshape(bsz, N_HEADS, HEAD_DIM),
             "mlstm_m": mm.reshape(bsz, N_HEADS),
             "s5_re": s_re.reshape(bsz, S5_GROUPS, S5_STATE), "s5_im": s_im.reshape(bsz, S5_GROUPS, S5_STATE),
             "conv": conv_new}
    return x, state


def kernel(x_prompt, x_sample, cache_sb_k, cache_sb_v, state_mlstm_c, state_mlstm_n, state_mlstm_m, state_s5_re, state_s5_im, state_conv, page_table, c_prompt, c_sample, w_ada, b_ada, g_norm1, g_norm2, w_in, b_mlstm_i, b_mlstm_f, g_mlstm_head, b_sb, s5_lambda_re, s5_lambda_im, s5_b_re, s5_b_im, s5_c_re, s5_c_im, s5_d, s5_log_step, w_s5_glu, conv_w, w_gate, b_gate, w_branch, w_out, w_ffn_gate, w_ffn_up, w_ffn_down, g_final):
    depth = w_in.shape[0]
    bp, tp, d = x_prompt.shape
    bs, ts, _ = x_sample.shape
    tpad = SUBLANES
    n_gate = 2 * N_HEADS
    split = 4 * MIX_W

    gate_bias = jnp.concatenate([b_mlstm_i, b_mlstm_f], axis=1)
    wts = {
        "g_norm1": g_norm1.reshape(depth, 1, d), "g_norm2": g_norm2.reshape(depth, 1, d),
        "w_in": jnp.concatenate([w_in[:, :, :split], w_in[:, :, split + n_gate:]], axis=2).astype(BF16),
        "w_if": jnp.pad(w_in[:, :, split:split + n_gate], ((0, 0), (0, 0), (0, GATE_LANES - n_gate))).astype(BF16),
        "gate_bias_c": jnp.pad(gate_bias, ((0, 0), (0, GATE_LANES - n_gate))).reshape(depth, 1, GATE_LANES),
        "gate_bias_r": gate_bias.reshape(depth, n_gate, 1),
        "g_head": g_mlstm_head.reshape(depth, 1, MIX_W),
        "b_sb": b_sb,
        "s5_d": s5_d.reshape(depth, 1, MIX_W),
        "w_s5_glu": w_s5_glu.astype(BF16),
        "conv_w": conv_w,
        "w_gate": w_gate, "b_gate": b_gate.reshape(depth, 1, N_BRANCH * d), "w_branch": w_branch,
        "w_out": w_out.astype(BF16),
        "w_ffn_gate": w_ffn_gate, "w_ffn_up": w_ffn_up,
        "w_ffn_down": w_ffn_down.astype(BF16),
    }
    tc_p = 512
    wts["s5_tab"], bb_re, bb_im = _s5_prep(s5_lambda_re, s5_lambda_im, s5_log_step, s5_b_re, s5_b_im)
    to_gcp = lambda a: a.reshape(depth, S5_GROUP_CH, S5_GROUPS, S5_STATE).transpose(0, 2, 1, 3)
    wts["s5_wb"] = jnp.concatenate([_block_diag_groups(to_gcp(bb_re)), _block_diag_groups(to_gcp(bb_im))],
                                   axis=3).astype(BF16)
    wts["s5_wc_re"] = _block_diag_groups(s5_c_re.transpose(0, 1, 3, 2)).astype(BF16)
    wts["s5_wc_im"] = _block_diag_groups(s5_c_im.transpose(0, 1, 3, 2)).astype(BF16)

    c_all = jnp.concatenate([c_prompt, c_sample], axis=0)
    c_all = jnp.pad(c_all, ((0, -c_all.shape[0] % SUBLANES), (0, 0)))
    mod_all = _ada_all(c_all, w_ada, b_ada).reshape(depth, c_all.shape[0], 6, d)

    sizes_p = {"tm": 1024, "flat": (bp, tp), "chunk": 256, "rows": 256, "valid": 256, "tc": tc_p, "valid_last": tc_p,
               "real": tp}
    sizes_s = {"tm": bs * tpad, "flat": (1, bs * tpad), "chunk": 128, "rows": tpad, "valid": ts, "tc": tpad,
               "valid_last": ts, "real": ts}

    xp = x_prompt
    xs = jnp.pad(x_sample, ((0, 0), (0, tpad - ts), (0, 0)))
    zeros_p = {
        "mlstm_c": jnp.zeros((bp, N_HEADS, HEAD_DIM, HEAD_DIM), F32),
        "mlstm_n": jnp.zeros((bp, N_HEADS, 1, HEAD_DIM), F32),
        "mlstm_m": jnp.zeros((bp, N_HEADS, 1, 1), F32),
        "s5_re": jnp.zeros((bp, 1, S5_WIDTH), F32), "s5_im": jnp.zeros((bp, 1, S5_WIDTH), F32),
        "conv": jnp.zeros((bp, SUBLANES, MIX_W), F32),
    }
    new_p, new_s = [], []
    for l in range(depth):
        mod_p = [mod_all[l, :bp, i].reshape(bp, 1, d) for i in range(6)]
        mod_s = [jnp.repeat(mod_all[l, bp:bp + bs, i], tpad, axis=0).reshape(bs, tpad, d) for i in range(6)]
        past_s = {
            "sb_k": cache_sb_k, "sb_v": cache_sb_v, "page_table": page_table,
            "mlstm_c": state_mlstm_c[l], "mlstm_n": state_mlstm_n[l].reshape(bs, N_HEADS, 1, HEAD_DIM),
            "mlstm_m": state_mlstm_m[l].reshape(bs, N_HEADS, 1, 1),
            "s5_re": state_s5_re[l].reshape(bs, 1, S5_WIDTH), "s5_im": state_s5_im[l].reshape(bs, 1, S5_WIDTH),
            "conv": jnp.pad(state_conv[l], ((0, 0), (SUBLANES - (CONV_K - 1), 0), (0, 0))),
        }
        xp, st_p = _layer(xp, mod_p, wts, l, zeros_p, sizes_p)
        xs, st_s = _layer(xs, mod_s, wts, l, past_s, sizes_s)
        new_p.append(st_p)
        new_s.append(st_s)

    y_prompt = _final_norm(xp, g_final.reshape(1, d))
    y_sample = _final_norm(xs, g_final.reshape(1, d))[:, :ts]
    stk = lambda states, name: jnp.stack([s[name] for s in states])
    return (y_prompt, y_sample,
            stk(new_p, "sb_k"), stk(new_p, "sb_v"), stk(new_s, "sb_k"), stk(new_s, "sb_v"),
            stk(new_p, "mlstm_c"), stk(new_p, "mlstm_n"), stk(new_p, "mlstm_m"),
            stk(new_s, "mlstm_c"), stk(new_s, "mlstm_n"), stk(new_s, "mlstm_m"),
            stk(new_p, "s5_re"), stk(new_p, "s5_im"), stk(new_s, "s5_re"), stk(new_s, "s5_im"),
            stk(new_p, "conv"), stk(new_s, "conv"))
```

```python
import functools
import math

import jax
import jax.numpy as jnp
from jax import lax
from jax.experimental import pallas as pl
from jax.experimental.pallas import tpu as pltpu

F32 = jnp.float32
BF16 = jnp.bfloat16

N_HEADS = 4
HEAD_DIM = 128
MIX_W = N_HEADS * HEAD_DIM
N_BRANCH = 4
S5_GROUP_CH = 16
S5_GROUPS = MIX_W // S5_GROUP_CH
S5_STATE = 64
S5_WIDTH = S5_GROUPS * S5_STATE
S5_SUPER = 4
CONV_K = 3
EPS = 1e-6
LOG2_E = 1.0 / math.log(2.0)
SUBLANES = 8
LANES = 128
GATE_LANES = 128
V7X_VMEM_LIMIT = 56 * 1024 * 1024

COL_AQ, COL_AK, COL_AV, COL_AO, COL_BQ, COL_BK, COL_BV, COL_SU, COL_CB, COL_CC, COL_CX = range(11)
N_PROJ_BLOCKS = 11

NT_DIMS = (((1,), (1,)), ((), ()))
TN_DIMS = (((0,), (0,)), ((), ()))


def _dot(a, b):
    return jnp.dot(a, b, preferred_element_type=F32)


def _dot_nt(a, b):
    return lax.dot_general(a, b, NT_DIMS, preferred_element_type=F32)


def _dot_exact(a, b):
    return jnp.dot(a, b, precision=lax.Precision.HIGHEST, preferred_element_type=F32)


def _softplus_neg_abs(z):
    return jnp.log(1.0 + jnp.exp(-jnp.abs(z)))


def _log_sigmoid(z):
    return jnp.minimum(z, 0.0) - _softplus_neg_abs(z)


def _pad_rows(x, rows):
    if x.shape[0] == rows:
        return x
    return jnp.concatenate([x, jnp.zeros((rows - x.shape[0],) + x.shape[1:], x.dtype)], axis=0)


def _params(sem, vmem=None):
    return pltpu.CompilerParams(dimension_semantics=sem, vmem_limit_bytes=vmem)


def _ada_kernel(c_ref, w_ref, b_ref, o_ref):
    c = c_ref[...]
    a = (c * jax.nn.sigmoid(c)).astype(BF16)
    o_ref[...] = _dot(a, w_ref[...].astype(BF16)) + b_ref[...]


def _ada_all(c, w_ada, b_ada, tn=1024):
    depth, d, n = w_ada.shape
    rows = c.shape[0]
    return pl.pallas_call(
        _ada_kernel,
        out_shape=jax.ShapeDtypeStruct((depth, rows, n), F32),
        grid=(depth, n // tn),
        in_specs=[pl.BlockSpec((rows, d), lambda l, j: (0, 0)),
                  pl.BlockSpec((None, d, tn), lambda l, j: (l, 0, j)),
                  pl.BlockSpec((None, 1, tn), lambda l, j: (l, 0, j))],
        out_specs=pl.BlockSpec((None, rows, tn), lambda l, j: (l, 0, j)),
        compiler_params=_params(("parallel", "parallel"), V7X_VMEM_LIMIT),
        name="ada_mod",
    )(c, w_ada, b_ada.reshape(depth, 1, n))


def _norm_mod_kernel(x_ref, g_ref, sc_ref, sh_ref, o_ref):
    x = x_ref[...]
    y = x * lax.rsqrt(jnp.mean(x * x, axis=-1, keepdims=True) + EPS) * g_ref[...]
    o_ref[...] = (y * (1.0 + sc_ref[...]) + sh_ref[...]).astype(o_ref.dtype)


def _norm_kernel(x_ref, g_ref, o_ref):
    x = x_ref[...]
    o_ref[...] = x * lax.rsqrt(jnp.mean(x * x, axis=-1, keepdims=True) + EPS) * g_ref[...]


def _row_spec(arr, tm, width, col):
    if arr.shape[1] == 1:
        return pl.BlockSpec((None, 1, width), lambda b, i, *r: (b, 0, col(*r)))
    return pl.BlockSpec((None, tm, width), lambda b, i, *r: (b, i, col(*r)))


def _norm_mod(x, g, l, sc, sh, tm=512):
    bsz, t, d = x.shape
    tm = min(tm, t)
    zero = lambda *r: 0
    return pl.pallas_call(
        _norm_mod_kernel,
        out_shape=jax.ShapeDtypeStruct((bsz, t, d), BF16),
        grid=(bsz, t // tm),
        in_specs=[pl.BlockSpec((None, tm, d), lambda b, i: (b, i, 0)),
                  pl.BlockSpec((None, 1, d), lambda b, i: (l, 0, 0)),
                  _row_spec(sc, tm, d, zero), _row_spec(sh, tm, d, zero)],
        out_specs=pl.BlockSpec((None, tm, d), lambda b, i: (b, i, 0)),
        compiler_params=_params(("parallel", "parallel")),
        name="norm_mod",
    )(x, g, sc, sh)


def _final_norm(x, g, tm=512):
    bsz, t, d = x.shape
    tm = min(tm, t)
    return pl.pallas_call(
        _norm_kernel,
        out_shape=jax.ShapeDtypeStruct((bsz, t, d), F32),
        grid=(bsz, t // tm),
        in_specs=[pl.BlockSpec((None, tm, d), lambda b, i: (b, i, 0)),
                  pl.BlockSpec((1, d), lambda b, i: (0, 0))],
        out_specs=pl.BlockSpec((None, tm, d), lambda b, i: (b, i, 0)),
        compiler_params=_params(("parallel", "parallel")),
        name="final_norm",
    )(x, g)


def _in_proj_kernel(a_ref, w_ref, wif_ref, *refs):
    o_ref, gates_ref, kb_ref, vb_ref = refs[-4:]
    j = pl.program_id(2)
    a = a_ref[...]
    res = _dot(a, w_ref[...])
    o_ref[...] = res

    @pl.when(j == 0)
    def _():
        gates_ref[...] = _dot(a, wif_ref[...])

    @pl.when(j == COL_BK)
    def _():
        kb_ref[...] = res

    @pl.when(j == COL_BV)
    def _():
        vb_ref[...] = res


def _in_proj(a, w, w_if, l, depth, kv_bufs, tm):
    bsz, t, k = a.shape
    n = w.shape[-1]
    tm = min(tm, t)
    buf = jax.ShapeDtypeStruct((depth, bsz, t, MIX_W), F32)
    buf_spec = pl.BlockSpec((None, None, tm, MIX_W), lambda b, i, j: (l, b, i, 0))
    carried = [] if kv_bufs is None else list(kv_bufs)
    proj, gates, kbuf, vbuf = pl.pallas_call(
        _in_proj_kernel,
        out_shape=(jax.ShapeDtypeStruct((bsz, t, n), F32), jax.ShapeDtypeStruct((bsz, t, GATE_LANES), F32), buf, buf),
        grid=(bsz, t // tm, n // MIX_W),
        in_specs=[pl.BlockSpec((None, tm, k), lambda b, i, j: (b, i, 0)),
                  pl.BlockSpec((None, k, MIX_W), lambda b, i, j: (l, 0, j)),
                  pl.BlockSpec((None, k, GATE_LANES), lambda b, i, j: (l, 0, 0))]
                 + [pl.BlockSpec(memory_space=pl.ANY)] * len(carried),
        out_specs=(pl.BlockSpec((None, tm, MIX_W), lambda b, i, j: (b, i, j)),
                   pl.BlockSpec((None, tm, GATE_LANES), lambda b, i, j: (b, i, 0)), buf_spec, buf_spec),
        input_output_aliases={3 + n_: 2 + n_ for n_ in range(len(carried))},
        compiler_params=_params(("parallel", "parallel", "arbitrary"), V7X_VMEM_LIMIT),
        name="in_proj",
    )(a, w, w_if, *carried)
    return proj, gates, (kbuf, vbuf)


def _mm_res_kernel(a_ref, w_ref, x_ref, g_ref, o_ref, *acc, nk):
    part = _dot(a_ref[...], w_ref[...])
    if nk == 1:
        o_ref[...] = x_ref[...] + g_ref[...] * part
        return
    acc_ref, = acc
    kk = pl.program_id(3)

    @pl.when(kk == 0)
    def _():
        acc_ref[...] = part

    @pl.when(kk > 0)
    def _():
        acc_ref[...] += part

    @pl.when(kk == nk - 1)
    def _():
        o_ref[...] = x_ref[...] + g_ref[...] * acc_ref[...]


def _linear_residual(a, w, l, x, g, tm, tn, nk, name):
    bsz, t, k = a.shape
    n = w.shape[-1]
    tm, tn = min(tm, t), min(tn, n)
    tk = k // nk
    return pl.pallas_call(
        functools.partial(_mm_res_kernel, nk=nk),
        out_shape=jax.ShapeDtypeStruct((bsz, t, n), F32),
        grid=(bsz, t // tm, n // tn, nk),
        in_specs=[pl.BlockSpec((None, tm, tk), lambda b, i, j, kk: (b, i, kk)),
                  pl.BlockSpec((None, tk, tn), lambda b, i, j, kk: (l, kk, j)),
                  pl.BlockSpec((None, tm, tn), lambda b, i, j, kk: (b, i, j)),
                  _row_spec(g, tm, tn, lambda j, kk: j)],
        out_specs=pl.BlockSpec((None, tm, tn), lambda b, i, j, kk: (b, i, j)),
        scratch_shapes=[pltpu.VMEM((tm, tn), F32)] if nk > 1 else [],
        compiler_params=_params(("parallel", "parallel", "parallel", "arbitrary"), V7X_VMEM_LIMIT),
        name=name,
    )(a, w, x, g)


def _ffn_up_kernel(a_ref, wg_ref, wu_ref, o_ref):
    a = a_ref[...]
    gate = _dot(a, wg_ref[...].astype(BF16))
    o_ref[...] = (gate * jax.nn.sigmoid(gate) * _dot(a, wu_ref[...].astype(BF16))).astype(o_ref.dtype)


def _ffn_up(a, wg, wu, l, tm, tn):
    bsz, t, k = a.shape
    n = wg.shape[-1]
    tm, tn = min(tm, t), min(tn, n)
    wspec = pl.BlockSpec((None, k, tn), lambda b, i, j: (l, 0, j))
    return pl.pallas_call(
        _ffn_up_kernel,
        out_shape=jax.ShapeDtypeStruct((bsz, t, n), BF16),
        grid=(bsz, t // tm, n // tn),
        in_specs=[pl.BlockSpec((None, tm, k), lambda b, i, j: (b, i, 0)), wspec, wspec],
        out_specs=pl.BlockSpec((None, tm, tn), lambda b, i, j: (b, i, j)),
        compiler_params=_params(("parallel", "parallel", "parallel"), V7X_VMEM_LIMIT),
        name="ffn_up",
    )(a, wg, wu)


def _gate_merge_kernel(h_ref, *refs):
    wg_refs, bg_refs, y_refs = refs[:N_BRANCH], refs[N_BRANCH:2 * N_BRANCH], refs[2 * N_BRANCH:3 * N_BRANCH]
    wb_ref, o_ref = refs[3 * N_BRANCH:]
    h = h_ref[...]
    acc = None
    for g in range(N_BRANCH):
        gate = jax.nn.sigmoid(_dot(h, wg_refs[g][...].astype(BF16)) + bg_refs[g][...])
        term = gate * _dot(y_refs[g][...], wb_ref[g].astype(BF16))
        acc = term if acc is None else acc + term
    o_ref[...] = acc.astype(o_ref.dtype)


def _gate_merge(h, wg, bg, ys, wb, l, tm, tn):
    bsz, t, k = h.shape
    d = wb.shape[-1]
    tm, tn = min(tm, t), min(tn, d)
    per = d // tn
    yspec = pl.BlockSpec((None, tm, MIX_W), lambda b, i, j: (b, i, 0))
    wspec = lambda g: pl.BlockSpec((None, k, tn), lambda b, i, j: (l, 0, g * per + j))
    bspec = lambda g: pl.BlockSpec((None, 1, tn), lambda b, i, j: (l, 0, g * per + j))
    branches = range(N_BRANCH)
    return pl.pallas_call(
        _gate_merge_kernel,
        out_shape=jax.ShapeDtypeStruct((bsz, t, d), BF16),
        grid=(bsz, t // tm, d // tn),
        in_specs=[pl.BlockSpec((None, tm, k), lambda b, i, j: (b, i, 0))]
                 + [wspec(g) for g in branches] + [bspec(g) for g in branches] + [yspec] * N_BRANCH
                 + [pl.BlockSpec((None, N_BRANCH, MIX_W, tn), lambda b, i, j: (l, 0, 0, j))],
        out_specs=pl.BlockSpec((None, tm, tn), lambda b, i, j: (b, i, j)),
        compiler_params=_params(("parallel", "parallel", "parallel"), V7X_VMEM_LIMIT),
        name="gate_merge",
    )(h, *([wg] * N_BRANCH), *([bg] * N_BRANCH), *ys, wb)


def _mlstm_kernel(q_ref, k_ref, v_ref, og_ref, gc_ref, gr_ref, bc_ref, br_ref, gh_ref, c0_ref, n0_ref, m0_ref,
                  y_ref, c_out, n_out, m_out, c_scr, n_scr, m_scr, *, chunk, valid):
    ci = pl.program_id(1)
    rows = q_ref.shape[0]

    @pl.when(ci == 0)
    def _():
        c_scr[...] = c0_ref[...]
        n_scr[...] = n0_ref[...]
        m_scr[...] = m0_ref[...]

    pos_c = lax.broadcasted_iota(jnp.int32, (chunk, 1), 0)
    pos_r = lax.broadcasted_iota(jnp.int32, (1, chunk), 1)
    tri_r = lax.broadcasted_iota(jnp.int32, (chunk, chunk), 0)
    tri_c = lax.broadcasted_iota(jnp.int32, (chunk, chunk), 1)
    causal = tri_c <= tri_r
    lower = causal.astype(F32)
    upper = (tri_r <= tri_c).astype(F32)

    gates_c = _pad_rows(gc_ref[...], chunk) + bc_ref[...]
    gates_r = gr_ref[...] + br_ref[...]
    lf_c = _log_sigmoid(gates_c)
    lf_r = _log_sigmoid(gates_r)
    if valid < chunk:
        lf_c = jnp.where(pos_c < valid, lf_c, 0.0)
        lf_r = jnp.where(pos_r < valid, lf_r, 0.0)
        gates_c = jnp.where(pos_c < valid, gates_c, -jnp.inf)
        gates_r = jnp.where(pos_r < valid, gates_r, -jnp.inf)
    cum_c = _dot_exact(lower, lf_c)
    cum_r = _dot_exact(lf_r, upper)

    heads = range(N_HEADS)
    sls = [slice(h * HEAD_DIM, (h + 1) * HEAD_DIM) for h in heads]
    qs = [_pad_rows(q_ref[:, sl], chunk) for sl in sls]
    ks = [_pad_rows(k_ref[:, sl], chunk) * (HEAD_DIM ** -0.5) for sl in sls]
    vs = [_pad_rows(v_ref[:, sl], chunk) for sl in sls]
    qbs, kbs, vbs = ([x.astype(BF16) for x in xs] for xs in (qs, ks, vs))
    b_cs = [cum_c[:, N_HEADS + h:N_HEADS + h + 1] for h in heads]
    b_rs = [cum_r[N_HEADS + h:N_HEADS + h + 1, :] for h in heads]
    m_prevs = [m_scr[h] for h in heads]
    c_prevs = [c_scr[h] for h in heads]
    n_prevs = [n_scr[h] for h in heads]

    qk = [_dot_nt(qbs[h], kbs[h]) for h in heads]
    cq = [_dot_nt(qbs[h], c_prevs[h].astype(BF16)) for h in heads]

    s_all, w_inters, m_ts = [], [], []
    for h in heads:
        d_intra = jnp.where(causal, b_cs[h] - b_rs[h] + gates_r[h:h + 1, :], -jnp.inf)
        m_inter = b_cs[h] + m_prevs[h]
        m_t = jnp.maximum(m_inter, jnp.max(d_intra, axis=1, keepdims=True))
        s_all.append(jnp.exp(d_intra - m_t) * qk[h])
        w_inters.append(jnp.exp(m_inter - m_t))
        m_ts.append(m_t)
    sv = [_dot(s_all[h].astype(BF16), vbs[h]) for h in heads]

    for h in heads:
        num = w_inters[h] * cq[h] + sv[h]
        den = (w_inters[h] * jnp.sum(qs[h] * n_prevs[h], axis=1, keepdims=True)
               + jnp.sum(s_all[h], axis=1, keepdims=True))
        hh = num / jnp.maximum(jnp.abs(den), jnp.exp(-m_ts[h]))
        hn = hh * lax.rsqrt(jnp.mean(hh * hh, axis=-1, keepdims=True) + EPS) * gh_ref[:, sls[h]]
        y = hn[:rows] * jax.nn.sigmoid(og_ref[:, sls[h]])
        y_ref[:, sls[h]] = y.astype(y_ref.dtype)

    w_ks, decays, m_ends = [], [], []
    for h in heads:
        b_end = b_cs[h][chunk - 1:chunk, :]
        log_w = b_end - b_cs[h] + gates_c[:, h:h + 1]
        m_end = jnp.maximum(b_end + m_prevs[h], jnp.max(log_w, axis=0, keepdims=True))
        w_ks.append(jnp.exp(log_w - m_end))
        decays.append(jnp.exp(b_end + m_prevs[h] - m_end))
        m_ends.append(m_end)
    kv = [lax.dot_general((w_ks[h] * vs[h]).astype(BF16), kbs[h], TN_DIMS, preferred_element_type=F32)
          for h in heads]
    for h in heads:
        c_scr[h] = decays[h] * c_prevs[h] + kv[h]
        n_scr[h] = decays[h] * n_prevs[h] + jnp.sum(w_ks[h] * ks[h], axis=0, keepdims=True)
        m_scr[h] = m_ends[h]

    @pl.when(ci == pl.num_programs(1) - 1)
    def _():
        c_out[...] = c_scr[...]
        n_out[...] = n_scr[...]
        m_out[...] = m_scr[...]


def _mlstm(proj, gates, gates_t, bias_c, bias_r, ghead, l, c0, n0, m0, chunk, rows, valid):
    bsz, t, _ = proj.shape
    nc = t // rows
    col = lambda c: pl.BlockSpec((None, rows, MIX_W), lambda b, i: (b, i, c))
    st4 = lambda s: pl.BlockSpec((None,) + s, lambda b, i: (b, 0, 0, 0))
    kern = functools.partial(_mlstm_kernel, chunk=chunk, valid=valid)
    return pl.pallas_call(
        kern,
        out_shape=(jax.ShapeDtypeStruct((bsz, t, MIX_W), BF16),
                   jax.ShapeDtypeStruct((bsz, N_HEADS, HEAD_DIM, HEAD_DIM), F32),
                   jax.ShapeDtypeStruct((bsz, N_HEADS, 1, HEAD_DIM), F32),
                   jax.ShapeDtypeStruct((bsz, N_HEADS, 1, 1), F32)),
        grid=(bsz, nc),
        in_specs=[col(COL_AQ), col(COL_AK), col(COL_AV), col(COL_AO),
                  pl.BlockSpec((None, rows, GATE_LANES), lambda b, i: (b, i, 0)),
                  pl.BlockSpec((None, SUBLANES, chunk), lambda b, i: (b, 0, i)),
                  pl.BlockSpec((None, 1, GATE_LANES), lambda b, i: (l, 0, 0)),
                  pl.BlockSpec((None, SUBLANES, 1), lambda b, i: (l, 0, 0)),
                  pl.BlockSpec((None, 1, MIX_W), lambda b, i: (l, 0, 0)),
                  st4((N_HEADS, HEAD_DIM, HEAD_DIM)), st4((N_HEADS, 1, HEAD_DIM)), st4((N_HEADS, 1, 1))],
        out_specs=(pl.BlockSpec((None, rows, MIX_W), lambda b, i: (b, i, 0)),
                   st4((N_HEADS, HEAD_DIM, HEAD_DIM)), st4((N_HEADS, 1, HEAD_DIM)), st4((N_HEADS, 1, 1))),
        scratch_shapes=[pltpu.VMEM((N_HEADS, HEAD_DIM, HEAD_DIM), F32),
                        pltpu.VMEM((N_HEADS, 1, HEAD_DIM), F32),
                        pltpu.VMEM((N_HEADS, 1, 1), F32)],
        compiler_params=_params(("parallel", "arbitrary"), V7X_VMEM_LIMIT),
        name="mlstm",
    )(proj, proj, proj, proj, gates, gates_t, bias_c, bias_r, ghead, c0, n0, m0)


def _sb_blocks(zs, carries, suffix, mask, chained):
    rows, n = zs[0].shape
    w = suffix.shape[0]
    nt = n // w
    tiles = [slice(t * w, (t + 1) * w) for t in range(nt)]
    ls_all, stacked, sums = [], [], []
    for z in zs:
        sp = jnp.log(1.0 + jnp.exp2(-jnp.abs(z))) * LOG2_E
        ls = jnp.minimum(z, 0.0) - sp
        ls_all.append(ls)
        l1m = ls - z
        if mask is not None:
            l1m = jnp.where(mask, l1m, 0.0)
        hi = l1m.astype(BF16)
        lo = (l1m - hi.astype(F32)).astype(BF16)
        stacked += [hi[:, s] for s in tiles] + [lo[:, s] for s in tiles]
        sums.append([jnp.sum(l1m[:, s], axis=1, keepdims=True) for s in tiles])
    ex = _dot(jnp.concatenate(stacked, axis=0), suffix)
    weights, out = [], []
    later = carries[0]
    for u in range(len(zs)):
        if not chained:
            later = carries[u]
        base = u * 2 * nt * rows
        parts = [None] * nt
        for t in reversed(range(nt)):
            hi_rows = ex[base + t * rows:base + (t + 1) * rows]
            lo_rows = ex[base + (nt + t) * rows:base + (nt + t + 1) * rows]
            parts[t] = hi_rows + lo_rows + later
            later = later + sums[u][t]
        excl = parts[0] if nt == 1 else jnp.concatenate(parts, axis=1)
        a = jnp.exp2(ls_all[u] + excl)
        if mask is not None:
            a = jnp.where(mask, a, 0.0)
        weights.append(a)
        out.append(later)
    return weights, out


def _strict_suffix_matrix(n):
    r = lax.broadcasted_iota(jnp.int32, (n, n), 0)
    c = lax.broadcasted_iota(jnp.int32, (n, n), 1)
    return (r > c).astype(BF16)


SB_HEADS_PER_STEP = 4
SB_HEADS_PER_MATMUL = 2


def _sb_prompt_kernel(bias_ref, q_ref, k_ref, v_ref, o_ref, *, blk):
    hp = pl.program_id(1)
    i = pl.program_id(2)
    scale = HEAD_DIM ** -0.5 * LOG2_E
    heads = range(SB_HEADS_PER_STEP)
    lanes = [slice(n * HEAD_DIM, (n + 1) * HEAD_DIM) for n in heads]
    bias = [bias_ref[hp * SB_HEADS_PER_STEP + n] * LOG2_E for n in heads]
    qb = [q_ref[:, lanes[n]].astype(BF16) for n in heads]
    suffix = _strict_suffix_matrix(blk)
    r = lax.broadcasted_iota(jnp.int32, (blk, blk), 0)
    cidx = lax.broadcasted_iota(jnp.int32, (blk, blk), 1)

    def step(jb, carry, mask):
        start = pl.multiple_of(jb * blk, blk)
        zs = [_dot_nt(qb[n], k_ref[pl.ds(start, blk), lanes[n]].astype(BF16)) * scale + bias[n] for n in heads]
        ws, cs = [], []
        for g in range(0, SB_HEADS_PER_STEP, SB_HEADS_PER_MATMUL):
            grp = range(g, g + SB_HEADS_PER_MATMUL)
            w_g, c_g = _sb_blocks([zs[n] for n in grp], [carry[n][0] for n in grp], suffix, mask, chained=False)
            ws += w_g
            cs += c_g
        return tuple((cs[n], carry[n][1] + _dot(ws[n].astype(BF16), v_ref[pl.ds(start, blk), lanes[n]].astype(BF16)))
                     for n in heads)

    zero = (jnp.zeros((blk, 1), F32), jnp.zeros((blk, HEAD_DIM), F32))
    carry = step(i, (zero,) * SB_HEADS_PER_STEP, cidx < r)
    carry = lax.fori_loop(0, i, lambda t, ca: step(i - 1 - t, ca, None), carry)
    for n in heads:
        o_ref[:, lanes[n]] = carry[n][1].astype(o_ref.dtype)


def _sb_prompt(proj, b_sb, l, blk=256):
    bsz, t, _ = proj.shape
    blk = min(blk, t)
    width = SB_HEADS_PER_STEP * HEAD_DIM
    groups = N_HEADS // SB_HEADS_PER_STEP
    kern = functools.partial(_sb_prompt_kernel, blk=blk)
    kv = lambda c0: pl.BlockSpec((None, t, width), lambda b, h, i: (b, 0, c0 * groups + h))
    return pl.pallas_call(
        kern,
        out_shape=jax.ShapeDtypeStruct((bsz, t, MIX_W), BF16),
        grid=(bsz, groups, t // blk),
        in_specs=[pl.BlockSpec(memory_space=pltpu.SMEM),
                  pl.BlockSpec((None, blk, width), lambda b, h, i: (b, i, COL_BQ * groups + h)),
                  kv(COL_BK), kv(COL_BV)],
        out_specs=pl.BlockSpec((None, blk, width), lambda b, h, i: (b, i, h)),
        compiler_params=_params(("parallel", "parallel", "arbitrary")),
        name="sb_prompt",
    )(b_sb[l], proj, proj, proj)


def _sb_sample_kernel(pt_ref, bias_ref, q_ref, kn_ref, vn_ref, suf_ref, *refs, pages_per_step):
    npg = pages_per_step
    k_refs, v_refs = refs[:npg], refs[npg:2 * npg]
    o_ref, c_scr, acc_scr = refs[2 * npg:]
    j = pl.program_id(1)
    tp = q_ref.shape[0]
    rows = N_HEADS * tp
    flat = k_refs[0].shape[0]
    scale = HEAD_DIM ** -0.5 * LOG2_E
    by_head = lambda ref: jnp.concatenate(
        [ref[:, h * HEAD_DIM:(h + 1) * HEAD_DIM] for h in range(N_HEADS)], axis=0)
    q_all = by_head(q_ref).astype(BF16)
    bias = jnp.concatenate([jnp.full((tp, 1), bias_ref[h] * LOG2_E, F32) for h in range(N_HEADS)], axis=0)

    def blocks(k_list, v_list, mask, c, acc):
        zs = [_dot_nt(q_all, kb) * scale + bias for kb in k_list]
        ws, cs = _sb_blocks(zs, [c], suf_ref[...], mask, chained=True)
        for a, vb in zip(ws, v_list):
            acc = acc + _dot(a.astype(BF16), vb)
        return cs[-1], acc

    @pl.when(j == 0)
    def _():
        r = lax.broadcasted_iota(jnp.int32, (rows, HEAD_DIM), 0)
        cidx = lax.broadcasted_iota(jnp.int32, (rows, HEAD_DIM), 1)
        mask = (cidx < rows) & (cidx // tp == r // tp) & (cidx % tp < r % tp)
        c, acc = blocks([_pad_rows(by_head(kn_ref), HEAD_DIM).astype(BF16)],
                        [_pad_rows(by_head(vn_ref), HEAD_DIM).astype(BF16)], mask,
                        jnp.zeros(c_scr.shape, F32), jnp.zeros(acc_scr.shape, F32))
        c_scr[...] = c
        acc_scr[...] = acc

    r = lax.broadcasted_iota(jnp.int32, (rows, flat), 0)
    cidx = lax.broadcasted_iota(jnp.int32, (rows, flat), 1)
    same_head = cidx % N_HEADS == r // tp
    c, acc = blocks([k_refs[p][...].astype(BF16) for p in range(npg)],
                    [v_refs[p][...].astype(BF16) for p in range(npg)], same_head, c_scr[...], acc_scr[...])
    c_scr[...] = c
    acc_scr[...] = acc

    @pl.when(j == pl.num_programs(1) - 1)
    def _():
        acc = acc_scr[...]
        for h in range(N_HEADS):
            o_ref[:, h * HEAD_DIM:(h + 1) * HEAD_DIM] = acc[h * tp:(h + 1) * tp].astype(o_ref.dtype)


def _sb_sample(proj, cache_k, cache_v, page_table, b_sb, l, pages_per_step=16):
    bsz, tp, _ = proj.shape
    n_pages = page_table.shape[1]
    depth, n_phys, page = cache_k.shape[:3]
    flat = page * N_HEADS
    npg = pages_per_step
    steps = n_pages // npg
    cache_k = cache_k.reshape(depth, n_phys, flat, HEAD_DIM)
    cache_v = cache_v.reshape(depth, n_phys, flat, HEAD_DIM)
    suffix = jnp.tril(jnp.ones((HEAD_DIM, HEAD_DIM), BF16), -1)

    def page_spec(p):
        def imap(b, j, pt):
            return (l, pt[b * n_pages + (n_pages - 1 - (j * npg + p))], 0, 0)
        return pl.BlockSpec((None, None, flat, HEAD_DIM), imap)

    col = lambda c: pl.BlockSpec((None, tp, MIX_W), lambda b, j, pt: (b, 0, c))
    kern = functools.partial(_sb_sample_kernel, pages_per_step=npg)
    return pl.pallas_call(
        kern,
        out_shape=jax.ShapeDtypeStruct((bsz, tp, MIX_W), BF16),
        grid_spec=pltpu.PrefetchScalarGridSpec(
            num_scalar_prefetch=1,
            grid=(bsz, steps),
            in_specs=[pl.BlockSpec(memory_space=pltpu.SMEM), col(COL_BQ), col(COL_BK), col(COL_BV),
                      pl.BlockSpec((HEAD_DIM, HEAD_DIM), lambda b, j, pt: (0, 0))]
                     + [page_spec(p) for p in range(npg)] * 2,
            out_specs=pl.BlockSpec((None, tp, MIX_W), lambda b, j, pt: (b, 0, 0)),
            scratch_shapes=[pltpu.VMEM((N_HEADS * tp, 1), F32), pltpu.VMEM((N_HEADS * tp, HEAD_DIM), F32)]),
        compiler_params=_params(("parallel", "arbitrary"), V7X_VMEM_LIMIT),
        name="sb_sample",
    )(page_table.reshape(-1), b_sb[l], proj, proj, proj, suffix, *([cache_k] * npg), *([cache_v] * npg))


def _cmul(ar, ai, br, bi):
    return ar * br - ai * bi, ar * bi + ai * br


def _s5_prep_kernel(lre_ref, lim_ref, ls_ref, bre_ref, bim_ref, tab_ref, bbre_ref, bbim_ref):
    lam_re, lam_im = lre_ref[...], lim_ref[...]
    step = jnp.exp(ls_ref[...])
    decay = jnp.exp(lam_re * step)
    a_re = decay * jnp.cos(lam_im * step)
    a_im = decay * jnp.sin(lam_im * step)
    inv = 1.0 / (lam_re * lam_re + lam_im * lam_im)
    f_re = ((a_re - 1.0) * lam_re + a_im * lam_im) * inv
    f_im = (a_im * lam_re - (a_re - 1.0) * lam_im) * inv
    b_re, b_im = bre_ref[...], bim_ref[...]
    bbre_ref[...] = f_re * b_re - f_im * b_im
    bbim_ref[...] = f_re * b_im + f_im * b_re

    pw = {1: (a_re, a_im)}
    pw[2] = _cmul(*pw[1], *pw[1])
    pw[3] = _cmul(*pw[2], *pw[1])
    pw[4] = _cmul(*pw[2], *pw[2])
    pw[5] = _cmul(*pw[4], *pw[1])
    pw[6] = _cmul(*pw[4], *pw[2])
    pw[7] = _cmul(*pw[4], *pw[3])
    pw[8] = _cmul(*pw[4], *pw[4])
    row = lax.broadcasted_iota(jnp.int32, (SUBLANES, lam_re.shape[1]), 0)
    for part in range(2):
        carry = jnp.zeros(row.shape, F32)
        for r in range(SUBLANES):
            carry = jnp.where(row == r, pw[r + 1][part], carry)
        tab_ref[part] = carry
        for idx, k in enumerate((1, 2, 4)):
            tab_ref[2 + 2 * idx + part] = jnp.where(row >= k, pw[k][part], 0.0)


def _s5_prep(lam_re, lam_im, log_step, b_re, b_im):
    depth = lam_re.shape[0]
    flat = lambda a: a.reshape(depth, 1, S5_WIDTH)
    bt = lambda a: a.transpose(0, 3, 1, 2).reshape(depth, S5_GROUP_CH, S5_WIDTH)
    step = jnp.broadcast_to(log_step[:, :, None], (depth, S5_GROUPS, S5_STATE))
    vec = pl.BlockSpec((None, 1, S5_WIDTH), lambda l: (l, 0, 0))
    mat = pl.BlockSpec((None, S5_GROUP_CH, S5_WIDTH), lambda l: (l, 0, 0))
    return pl.pallas_call(
        _s5_prep_kernel,
        out_shape=(jax.ShapeDtypeStruct((depth, 8, SUBLANES, S5_WIDTH), F32),
                   jax.ShapeDtypeStruct((depth, S5_GROUP_CH, S5_WIDTH), F32),
                   jax.ShapeDtypeStruct((depth, S5_GROUP_CH, S5_WIDTH), F32)),
        grid=(depth,),
        in_specs=[vec, vec, vec, mat, mat],
        out_specs=(pl.BlockSpec((None, 8, SUBLANES, S5_WIDTH), lambda l: (l, 0, 0, 0)), mat, mat),
        compiler_params=_params(("parallel",)),
        name="s5_prep",
    )(flat(lam_re), flat(lam_im), flat(step), bt(b_re), bt(b_im))


def _s5_kernel(u_ref, wb_ref, tab_ref, hre_ref, him_ref, wcre_ref, wcim_ref, d_ref, wglu_ref,
               y_ref, sre_out, sim_out, sre, sim, cre, cim, *, valid_last, slab):
    ci = pl.program_id(1)
    tc = u_ref.shape[0]
    sub = MIX_W // S5_SUPER
    wid = S5_WIDTH // S5_SUPER

    @pl.when(ci == 0)
    def _():
        cre[...] = jnp.broadcast_to(hre_ref[...], cre.shape)
        cim[...] = jnp.broadcast_to(him_ref[...], cim.shape)

    u = u_ref[...]
    ub = u.astype(BF16)
    for g in range(S5_SUPER):
        bu = _dot(ub[:, g * sub:(g + 1) * sub], wb_ref[g])
        sre[:, g * wid:(g + 1) * wid] = bu[:, :wid]
        sim[:, g * wid:(g + 1) * wid] = bu[:, wid:]

    for s0 in range(0, S5_WIDTH, slab):
        lanes = slice(s0, s0 + slab)
        pr, pi = tab_ref[0, :, lanes], tab_ref[1, :, lanes]
        levels = [(k, tab_ref[2 + 2 * idx, :, lanes], tab_ref[3 + 2 * idx, :, lanes])
                  for idx, k in enumerate((1, 2, 4))]

        def body(r, carry, lanes=lanes, pr=pr, pi=pi, levels=levels):
            c_re, c_im = carry
            row = pl.multiple_of(r * SUBLANES, SUBLANES)
            xr = sre[pl.ds(row, SUBLANES), lanes]
            xi = sim[pl.ds(row, SUBLANES), lanes]
            for k, mr, mi in levels:
                rr = pltpu.roll(xr, k, axis=0)
                ri = pltpu.roll(xi, k, axis=0)
                xr, xi = xr + mr * rr - mi * ri, xi + mr * ri + mi * rr
            xr, xi = xr + pr * c_re - pi * c_im, xi + pr * c_im + pi * c_re
            sre[pl.ds(row, SUBLANES), lanes] = xr
            sim[pl.ds(row, SUBLANES), lanes] = xi
            last = SUBLANES - 1
            return (jnp.broadcast_to(xr[last:last + 1, :], xr.shape),
                    jnp.broadcast_to(xi[last:last + 1, :], xi.shape))

        nblk = tc // SUBLANES
        c_re, c_im = lax.fori_loop(0, nblk, body, (cre[:, lanes], cim[:, lanes]), unroll=min(2, nblk))
        cre[:, lanes] = c_re
        cim[:, lanes] = c_im

    s_re_b = sre[...].astype(BF16)
    s_im_b = sim[...].astype(BF16)
    y = jnp.concatenate(
        [_dot(s_re_b[:, g * wid:(g + 1) * wid], wcre_ref[g]) - _dot(s_im_b[:, g * wid:(g + 1) * wid], wcim_ref[g])
         for g in range(S5_SUPER)], axis=1)
    y = y + d_ref[...] * u
    y = 0.5 * y * (1.0 + jnp.tanh(math.sqrt(2.0 / math.pi) * (y + 0.044715 * (y * y * y))))
    yy = _dot(y.astype(BF16), wglu_ref[...])
    y_ref[...] = (yy[:, :MIX_W] * jax.nn.sigmoid(yy[:, MIX_W:])).astype(y_ref.dtype)

    @pl.when(ci == pl.num_programs(1) - 1)
    def _():
        sre_out[...] = sre[valid_last - 1:valid_last, :]
        sim_out[...] = sim[valid_last - 1:valid_last, :]


def _s5(proj, wb, tab, h_re, h_im, wc_re, wc_im, d_skip, w_glu, l, tc, valid_last, slab=512):
    bsz, t, _ = proj.shape
    tc = min(tc, t)
    layer = lambda s: pl.BlockSpec((None,) + s, lambda b, i: (l,) + (0,) * len(s))
    st = pl.BlockSpec((None, 1, S5_WIDTH), lambda b, i: (b, 0, 0))
    kern = functools.partial(_s5_kernel, valid_last=valid_last, slab=slab)
    wid = S5_WIDTH // S5_SUPER
    return pl.pallas_call(
        kern,
        out_shape=(jax.ShapeDtypeStruct((bsz, t, MIX_W), BF16),
                   jax.ShapeDtypeStruct((bsz, 1, S5_WIDTH), F32),
                   jax.ShapeDtypeStruct((bsz, 1, S5_WIDTH), F32)),
        grid=(bsz, t // tc),
        in_specs=[pl.BlockSpec((None, tc, MIX_W), lambda b, i: (b, i, COL_SU)),
                  layer((S5_SUPER, MIX_W // S5_SUPER, 2 * wid)),
                  layer((8, SUBLANES, S5_WIDTH)),
                  st, st,
                  layer((S5_SUPER, wid, MIX_W // S5_SUPER)), layer((S5_SUPER, wid, MIX_W // S5_SUPER)),
                  layer((1, MIX_W)), layer((MIX_W, 2 * MIX_W))],
        out_specs=(pl.BlockSpec((None, tc, MIX_W), lambda b, i: (b, i, 0)), st, st),
        scratch_shapes=[pltpu.VMEM((tc, S5_WIDTH), F32), pltpu.VMEM((tc, S5_WIDTH), F32),
                        pltpu.VMEM((SUBLANES, S5_WIDTH), F32), pltpu.VMEM((SUBLANES, S5_WIDTH), F32)],
        compiler_params=_params(("parallel", "arbitrary"), V7X_VMEM_LIMIT),
        name="s5",
    )(proj, wb, tab, h_re, h_im, wc_re, wc_im, d_skip, w_glu)


def _block_diag_groups(w):
    depth, _, a, b = w.shape
    per = S5_GROUPS // S5_SUPER
    w = w.reshape(depth, S5_SUPER, per, a, b)
    eye = jnp.eye(per, dtype=w.dtype)
    bd = w[:, :, :, :, None, :] * eye[None, None, :, None, :, None]
    return bd.reshape(depth, S5_SUPER, per * a, per * b)


def _conv_kernel(gb_ref, gc_ref, xv_ref, hc_ref, hx_ref, buf_ref, w_ref, y_ref, new_ref, *, valid_last):
    ci = pl.program_id(1)
    tc = gb_ref.shape[0]
    z = gc_ref[...] * xv_ref[...]
    prev = jnp.where(ci == 0, buf_ref[...], hc_ref[...] * hx_ref[...])
    zz = jnp.concatenate([prev, z], axis=0)
    w = w_ref[...]
    y = sum(w[j:j + 1, :] * zz[SUBLANES - (CONV_K - 1) + j:SUBLANES - (CONV_K - 1) + j + tc] for j in range(CONV_K))
    y_ref[...] = (gb_ref[...] * y).astype(y_ref.dtype)

    @pl.when(ci == pl.num_programs(1) - 1)
    def _():
        end = SUBLANES + valid_last
        new_ref[...] = zz[end - (CONV_K - 1):end]


def _conv(proj, buf8, w, l, tc, valid_last):
    bsz, t, _ = proj.shape
    tc = min(tc, t)
    per = tc // SUBLANES
    col = lambda c: pl.BlockSpec((None, tc, MIX_W), lambda b, i: (b, i, c))
    halo = lambda c: pl.BlockSpec((None, SUBLANES, MIX_W), lambda b, i: (b, jnp.maximum(i * per - 1, 0), c))
    kern = functools.partial(_conv_kernel, valid_last=valid_last)
    return pl.pallas_call(
        kern,
        out_shape=(jax.ShapeDtypeStruct((bsz, t, MIX_W), BF16),
                   jax.ShapeDtypeStruct((bsz, CONV_K - 1, MIX_W), F32)),
        grid=(bsz, t // tc),
        in_specs=[col(COL_CB), col(COL_CC), col(COL_CX), halo(COL_CC), halo(COL_CX),
                  pl.BlockSpec((None, SUBLANES, MIX_W), lambda b, i: (b, 0, 0)),
                  pl.BlockSpec((None, CONV_K, MIX_W), lambda b, i: (l, 0, 0))],
        out_specs=(pl.BlockSpec((None, tc, MIX_W), lambda b, i: (b, i, 0)),
                   pl.BlockSpec((None, CONV_K - 1, MIX_W), lambda b, i: (b, 0, 0))),
        compiler_params=_params(("parallel", "arbitrary")),
        name="short_conv",
    )(proj, proj, proj, proj, proj, buf8, w)


def _layer(x, mod, wts, l, past, sizes, kv_bufs):
    bsz, t, d = x.shape
    sh1, sc1, g1, sh2, sc2, g2 = mod
    tm = sizes["tm"]
    flat = sizes["flat"]
    as_mm = lambda a: a.reshape(flat + a.shape[2:])
    as_seq = lambda a: a.reshape((bsz, t) + a.shape[2:])
    mm_mod = lambda a: a if a.shape[1] == 1 else as_mm(a)

    h = _norm_mod(x, wts["g_norm1"], l, sc1, sh1)
    hm = as_mm(h)
    proj, gates, kv_bufs = _in_proj(hm, wts["w_in"], wts["w_if"], l, wts["w_in"].shape[0], kv_bufs, tm)
    proj, gates = as_seq(proj), as_seq(gates)
    gates_t = jnp.swapaxes(gates[:, :, :SUBLANES], 1, 2)
    if sizes["chunk"] > t:
        gates_t = jnp.pad(gates_t, ((0, 0), (0, 0), (0, sizes["chunk"] - t)))

    y_a, mc, mn, mm = _mlstm(proj, gates, gates_t, wts["gate_bias_c"], wts["gate_bias_r"], wts["g_head"], l,
                             past["mlstm_c"], past["mlstm_n"], past["mlstm_m"],
                             sizes["chunk"], sizes["rows"], sizes["valid"])
    if "sb_k" in past:
        y_b = _sb_sample(proj, past["sb_k"], past["sb_v"], past["page_table"], wts["b_sb"], l)
    else:
        y_b = _sb_prompt(proj, wts["b_sb"], l)
    y_s, s_re, s_im = _s5(proj, wts["s5_wb"], wts["s5_tab"], past["s5_re"], past["s5_im"],
                          wts["s5_wc_re"], wts["s5_wc_im"], wts["s5_d"], wts["w_s5_glu"], l,
                          sizes["tc"], sizes["valid_last"])
    y_c, conv_new = _conv(proj, past["conv"], wts["conv_w"], l, sizes["tc"], sizes["valid_last"])

    merged = _gate_merge(hm, wts["w_gate"], wts["b_gate"], [as_mm(y) for y in (y_a, y_b, y_s, y_c)],
                         wts["w_branch"], l, tm, 256)
    x = as_seq(_linear_residual(merged, wts["w_out"], l, as_mm(x), mm_mod(g1), tm, 1024, 1, "out_proj"))

    h2 = as_mm(_norm_mod(x, wts["g_norm2"], l, sc2, sh2))
    hidden = _ffn_up(h2, wts["w_ffn_gate"], wts["w_ffn_up"], l, tm, 512)
    x = as_seq(_linear_residual(hidden, wts["w_ffn_down"], l, as_mm(x), mm_mod(g2), tm, 1024, 2, "ffn_down"))

    state = {"mlstm_c": mc, "mlstm_n": mn.reshape(bsz, N_HEADS, HEAD_DIM), "mlstm_m": mm.reshape(bsz, N_HEADS),
             "s5_re": s_re.reshape(bsz, S5_GROUPS, S5_STATE), "s5_im": s_im.reshape(bsz, S5_GROUPS, S5_STATE),
             "conv": conv_new}
    return x, state, kv_bufs


def kernel(x_prompt, x_sample, cache_sb_k, cache_sb_v, state_mlstm_c, state_mlstm_n, state_mlstm_m, state_s5_re, state_s5_im, state_conv, page_table, c_prompt, c_sample, w_ada, b_ada, g_norm1, g_norm2, w_in, b_mlstm_i, b_mlstm_f, g_mlstm_head, b_sb, s5_lambda_re, s5_lambda_im, s5_b_re, s5_b_im, s5_c_re, s5_c_im, s5_d, s5_log_step, w_s5_glu, conv_w, w_gate, b_gate, w_branch, w_out, w_ffn_gate, w_ffn_up, w_ffn_down, g_final):
    depth = w_in.shape[0]
    bp, tp, d = x_prompt.shape
    bs, ts, _ = x_sample.shape
    tpad = SUBLANES
    n_gate = 2 * N_HEADS
    split = 4 * MIX_W

    gate_bias = jnp.concatenate([b_mlstm_i, b_mlstm_f], axis=1)
    wts = {
        "g_norm1": g_norm1.reshape(depth, 1, d), "g_norm2": g_norm2.reshape(depth, 1, d),
        "w_in": jnp.concatenate([w_in[:, :, :split], w_in[:, :, split + n_gate:]], axis=2).astype(BF16),
        "w_if": jnp.pad(w_in[:, :, split:split + n_gate], ((0, 0), (0, 0), (0, GATE_LANES - n_gate))).astype(BF16),
        "gate_bias_c": jnp.pad(gate_bias, ((0, 0), (0, GATE_LANES - n_gate))).reshape(depth, 1, GATE_LANES),
        "gate_bias_r": gate_bias.reshape(depth, n_gate, 1),
        "g_head": g_mlstm_head.reshape(depth, 1, MIX_W),
        "b_sb": b_sb,
        "s5_d": s5_d.reshape(depth, 1, MIX_W),
        "w_s5_glu": w_s5_glu.astype(BF16),
        "conv_w": conv_w,
        "w_gate": w_gate, "b_gate": b_gate.reshape(depth, 1, N_BRANCH * d), "w_branch": w_branch,
        "w_out": w_out.astype(BF16),
        "w_ffn_gate": w_ffn_gate, "w_ffn_up": w_ffn_up,
        "w_ffn_down": w_ffn_down.astype(BF16),
    }
    tc_p = 512
    wts["s5_tab"], bb_re, bb_im = _s5_prep(s5_lambda_re, s5_lambda_im, s5_log_step, s5_b_re, s5_b_im)
    to_gcp = lambda a: a.reshape(depth, S5_GROUP_CH, S5_GROUPS, S5_STATE).transpose(0, 2, 1, 3)
    wts["s5_wb"] = jnp.concatenate([_block_diag_groups(to_gcp(bb_re)), _block_diag_groups(to_gcp(bb_im))],
                                   axis=3).astype(BF16)
    wts["s5_wc_re"] = _block_diag_groups(s5_c_re.transpose(0, 1, 3, 2)).astype(BF16)
    wts["s5_wc_im"] = _block_diag_groups(s5_c_im.transpose(0, 1, 3, 2)).astype(BF16)

    c_all = jnp.concatenate([c_prompt, c_sample], axis=0)
    c_all = jnp.pad(c_all, ((0, -c_all.shape[0] % SUBLANES), (0, 0)))
    mod_all = _ada_all(c_all, w_ada, b_ada).reshape(depth, c_all.shape[0], 6, d)

    sizes_p = {"tm": 1024, "flat": (bp, tp), "chunk": 256, "rows": 256, "valid": 256, "tc": tc_p, "valid_last": tc_p,
               "real": tp}
    sizes_s = {"tm": bs * tpad, "flat": (1, bs * tpad), "chunk": 128, "rows": tpad, "valid": ts, "tc": tpad,
               "valid_last": ts, "real": ts}

    xp = x_prompt
    xs = jnp.pad(x_sample, ((0, 0), (0, tpad - ts), (0, 0)))
    zeros_p = {
        "mlstm_c": jnp.zeros((bp, N_HEADS, HEAD_DIM, HEAD_DIM), F32),
        "mlstm_n": jnp.zeros((bp, N_HEADS, 1, HEAD_DIM), F32),
        "mlstm_m": jnp.zeros((bp, N_HEADS, 1, 1), F32),
        "s5_re": jnp.zeros((bp, 1, S5_WIDTH), F32), "s5_im": jnp.zeros((bp, 1, S5_WIDTH), F32),
        "conv": jnp.zeros((bp, SUBLANES, MIX_W), F32),
    }
    new_p, new_s = [], []
    kv_p = kv_s = None
    for l in range(depth):
        mod_p = [mod_all[l, :bp, i].reshape(bp, 1, d) for i in range(6)]
        mod_s = [jnp.repeat(mod_all[l, bp:bp + bs, i], tpad, axis=0).reshape(bs, tpad, d) for i in range(6)]
        past_s = {
            "sb_k": cache_sb_k, "sb_v": cache_sb_v, "page_table": page_table,
            "mlstm_c": state_mlstm_c[l], "mlstm_n": state_mlstm_n[l].reshape(bs, N_HEADS, 1, HEAD_DIM),
            "mlstm_m": state_mlstm_m[l].reshape(bs, N_HEADS, 1, 1),
            "s5_re": state_s5_re[l].reshape(bs, 1, S5_WIDTH), "s5_im": state_s5_im[l].reshape(bs, 1, S5_WIDTH),
            "conv": jnp.pad(state_conv[l], ((0, 0), (SUBLANES - (CONV_K - 1), 0), (0, 0))),
        }
        xp, st_p, kv_p = _layer(xp, mod_p, wts, l, zeros_p, sizes_p, kv_p)
        xs, st_s, kv_s = _layer(xs, mod_s, wts, l, past_s, sizes_s, kv_s)
        new_p.append(st_p)
        new_s.append(st_s)

    y_prompt = _final_norm(xp, g_final.reshape(1, d))
    y_sample = _final_norm(xs, g_final.reshape(1, d))[:, :ts]
    stk = lambda states, name: jnp.stack([s[name] for s in states])
    heads_p = lambda a: a.reshape(depth, bp, tp, N_HEADS, HEAD_DIM)
    heads_s = lambda a: a.reshape(depth, bs, tpad, N_HEADS, HEAD_DIM)[:, :, :ts]
    return (y_prompt, y_sample,
            heads_p(kv_p[0]), heads_p(kv_p[1]), heads_s(kv_s[0]), heads_s(kv_s[1]),
            stk(new_p, "mlstm_c"), stk(new_p, "mlstm_n"), stk(new_p, "mlstm_m"),
            stk(new_s, "mlstm_c"), stk(new_s, "mlstm_n"), stk(new_s, "mlstm_m"),
            stk(new_p, "s5_re"), stk(new_p, "s5_im"), stk(new_s, "s5_re"), stk(new_s, "s5_im"),
            stk(new_p, "conv"), stk(new_s, "conv"))
```

```python
import functools
import math

import jax
import jax.numpy as jnp
from jax import lax
from jax.experimental import pallas as pl
from jax.experimental.pallas import tpu as pltpu

F32 = jnp.float32
BF16 = jnp.bfloat16

N_HEADS = 4
HEAD_DIM = 128
MIX_W = N_HEADS * HEAD_DIM
N_BRANCH = 4
S5_GROUP_CH = 16
S5_GROUPS = MIX_W // S5_GROUP_CH
S5_STATE = 64
S5_WIDTH = S5_GROUPS * S5_STATE
S5_SUPER = 4
CONV_K = 3
EPS = 1e-6
LOG2_E = 1.0 / math.log(2.0)
SUBLANES = 8
LANES = 128
GATE_LANES = 128
V7X_VMEM_LIMIT = 56 * 1024 * 1024

COL_AQ, COL_AK, COL_AV, COL_AO, COL_BQ, COL_BK, COL_BV, COL_SU, COL_CB, COL_CC, COL_CX = range(11)
N_PROJ_BLOCKS = 11

NT_DIMS = (((1,), (1,)), ((), ()))
TN_DIMS = (((0,), (0,)), ((), ()))


def _dot(a, b):
    return jnp.dot(a, b, preferred_element_type=F32)


def _dot_nt(a, b):
    return lax.dot_general(a, b, NT_DIMS, preferred_element_type=F32)


def _dot_exact(a, b):
    return jnp.dot(a, b, precision=lax.Precision.HIGHEST, preferred_element_type=F32)


def _softplus_neg_abs(z):
    return jnp.log(1.0 + jnp.exp(-jnp.abs(z)))


def _log_sigmoid(z):
    return jnp.minimum(z, 0.0) - _softplus_neg_abs(z)


def _pad_rows(x, rows):
    if x.shape[0] == rows:
        return x
    return jnp.concatenate([x, jnp.zeros((rows - x.shape[0],) + x.shape[1:], x.dtype)], axis=0)


def _params(sem, vmem=None):
    return pltpu.CompilerParams(dimension_semantics=sem, vmem_limit_bytes=vmem)


def _ada_kernel(c_ref, w_ref, b_ref, o_ref):
    c = c_ref[...]
    a = (c * jax.nn.sigmoid(c)).astype(BF16)
    o_ref[...] = _dot(a, w_ref[...].astype(BF16)) + b_ref[...]


def _ada_all(c, w_ada, b_ada, tn=1024):
    depth, d, n = w_ada.shape
    rows = c.shape[0]
    return pl.pallas_call(
        _ada_kernel,
        out_shape=jax.ShapeDtypeStruct((depth, rows, n), F32),
        grid=(depth, n // tn),
        in_specs=[pl.BlockSpec((rows, d), lambda l, j: (0, 0)),
                  pl.BlockSpec((None, d, tn), lambda l, j: (l, 0, j)),
                  pl.BlockSpec((None, 1, tn), lambda l, j: (l, 0, j))],
        out_specs=pl.BlockSpec((None, rows, tn), lambda l, j: (l, 0, j)),
        compiler_params=_params(("parallel", "parallel"), V7X_VMEM_LIMIT),
        name="ada_mod",
    )(c, w_ada, b_ada.reshape(depth, 1, n))


def _norm_mod_kernel(x_ref, g_ref, sc_ref, sh_ref, o_ref):
    x = x_ref[...]
    y = x * lax.rsqrt(jnp.mean(x * x, axis=-1, keepdims=True) + EPS) * g_ref[...]
    o_ref[...] = (y * (1.0 + sc_ref[...]) + sh_ref[...]).astype(o_ref.dtype)


def _row_spec(arr, tm, width, col):
    if arr.shape[1] == 1:
        return pl.BlockSpec((None, 1, width), lambda b, i, *r: (b, 0, col(*r)))
    return pl.BlockSpec((None, tm, width), lambda b, i, *r: (b, i, col(*r)))


def _norm_mod(x, g, l, sc, sh, tm=512):
    bsz, t, d = x.shape
    tm = min(tm, t)
    zero = lambda *r: 0
    return pl.pallas_call(
        _norm_mod_kernel,
        out_shape=jax.ShapeDtypeStruct((bsz, t, d), BF16),
        grid=(bsz, t // tm),
        in_specs=[pl.BlockSpec((None, tm, d), lambda b, i: (b, i, 0)),
                  pl.BlockSpec((None, 1, d), lambda b, i: (l, 0, 0)),
                  _row_spec(sc, tm, d, zero), _row_spec(sh, tm, d, zero)],
        out_specs=pl.BlockSpec((None, tm, d), lambda b, i: (b, i, 0)),
        compiler_params=_params(("parallel", "parallel")),
        name="norm_mod",
    )(x, g, sc, sh)


N_GATE_COLS = 2 * N_HEADS


def _in_proj_kernel(a_ref, w_ref, wn_ref, *refs):
    o_ref, gates_ref, kb_ref, vb_ref = refs[-4:]
    j = pl.program_id(2)
    a = a_ref[...]

    @pl.when(j < COL_BQ)
    def _():
        o_ref[...] = _dot(a, w_ref[...].astype(BF16))

    @pl.when(j >= COL_BQ)
    def _():
        w = jnp.concatenate([w_ref[:, N_GATE_COLS:], wn_ref[:, :N_GATE_COLS]], axis=1).astype(BF16)
        res = _dot(a, w)
        o_ref[...] = res

        @pl.when(j == COL_BK)
        def _():
            kb_ref[...] = res

        @pl.when(j == COL_BV)
        def _():
            vb_ref[...] = res

    @pl.when(j == COL_BQ)
    def _():
        lane = lax.broadcasted_iota(jnp.int32, (w_ref.shape[0], GATE_LANES), 1)
        w_gate = jnp.where(lane < N_GATE_COLS, w_ref[:, :GATE_LANES], 0.0).astype(BF16)
        gates_ref[...] = _dot(a, w_gate)


def _in_proj(a, w, l, kv_bufs, tm):
    bsz, t, k = a.shape
    depth = w.shape[0]
    n = w.shape[-1] - N_GATE_COLS
    tm = min(tm, t)
    per = MIX_W // LANES
    buf = jax.ShapeDtypeStruct((depth, bsz, t, MIX_W), F32)
    buf_spec = pl.BlockSpec((None, None, tm, MIX_W), lambda b, i, j: (l, b, i, 0))
    carried = [] if kv_bufs is None else list(kv_bufs)
    proj, gates, kbuf, vbuf = pl.pallas_call(
        _in_proj_kernel,
        out_shape=(jax.ShapeDtypeStruct((bsz, t, n), F32), jax.ShapeDtypeStruct((bsz, t, GATE_LANES), F32), buf, buf),
        grid=(bsz, t // tm, n // MIX_W),
        in_specs=[pl.BlockSpec((None, tm, k), lambda b, i, j: (b, i, 0)),
                  pl.BlockSpec((None, k, MIX_W), lambda b, i, j: (l, 0, j)),
                  pl.BlockSpec((None, k, LANES), lambda b, i, j: (l, 0, (j + 1) * per))]
                 + [pl.BlockSpec(memory_space=pl.ANY)] * len(carried),
        out_specs=(pl.BlockSpec((None, tm, MIX_W), lambda b, i, j: (b, i, j)),
                   pl.BlockSpec((None, tm, GATE_LANES), lambda b, i, j: (b, i, 0)), buf_spec, buf_spec),
        input_output_aliases={3 + n_: 2 + n_ for n_ in range(len(carried))},
        compiler_params=_params(("parallel", "parallel", "arbitrary"), V7X_VMEM_LIMIT),
        name="in_proj",
    )(a, w, w, *carried)
    return proj, gates, (kbuf, vbuf)


def _mm_res_norm_kernel(a_ref, w_ref, x_ref, g_ref, gn_ref, sc_ref, sh_ref, xo_ref, ho_ref, *acc, nk):
    part = _dot(a_ref[...], w_ref[...])

    def finish(total):
        x = x_ref[...] + g_ref[...] * total
        xo_ref[...] = x
        y = x * lax.rsqrt(jnp.mean(x * x, axis=-1, keepdims=True) + EPS) * gn_ref[...]
        ho_ref[...] = (y * (1.0 + sc_ref[...]) + sh_ref[...]).astype(ho_ref.dtype)

    if nk == 1:
        finish(part)
        return
    acc_ref, = acc
    kk = pl.program_id(2)

    @pl.when(kk == 0)
    def _():
        acc_ref[...] = part

    @pl.when((kk > 0) & (kk < nk - 1))
    def _():
        acc_ref[...] += part

    @pl.when(kk == nk - 1)
    def _():
        finish(acc_ref[...] + part)


def _linear_residual_norm(a, w, l, x, g, norm, tm, nk, name):
    bsz, t, k = a.shape
    n = w.shape[-1]
    gn, ln, sc, sh, h_dtype = norm
    tm = min(tm, t)
    tk = k // nk
    zero = lambda kk: 0
    row = pl.BlockSpec((None, tm, n), lambda b, i, kk: (b, i, 0))
    return pl.pallas_call(
        functools.partial(_mm_res_norm_kernel, nk=nk),
        out_shape=(jax.ShapeDtypeStruct((bsz, t, n), F32), jax.ShapeDtypeStruct((bsz, t, n), h_dtype)),
        grid=(bsz, t // tm, nk),
        in_specs=[pl.BlockSpec((None, tm, tk), lambda b, i, kk: (b, i, kk)),
                  pl.BlockSpec((None, tk, n), lambda b, i, kk: (l, kk, 0)),
                  row, _row_spec(g, tm, n, zero),
                  pl.BlockSpec((None, 1, n), lambda b, i, kk: (ln, 0, 0)),
                  _row_spec(sc, tm, n, zero), _row_spec(sh, tm, n, zero)],
        out_specs=(row, row),
        scratch_shapes=[pltpu.VMEM((tm, n), F32)] if nk > 1 else [],
        compiler_params=_params(("parallel", "parallel", "arbitrary"), V7X_VMEM_LIMIT),
        name=name,
    )(a, w, x, g, gn, sc, sh)


def _ffn_up_kernel(a_ref, wg_ref, wu_ref, o_ref):
    a = a_ref[...]
    gate = _dot(a, wg_ref[...].astype(BF16))
    o_ref[...] = (gate * jax.nn.sigmoid(gate) * _dot(a, wu_ref[...].astype(BF16))).astype(o_ref.dtype)


def _ffn_up(a, wg, wu, l, tm, tn):
    bsz, t, k = a.shape
    n = wg.shape[-1]
    tm, tn = min(tm, t), min(tn, n)
    wspec = pl.BlockSpec((None, k, tn), lambda b, i, j: (l, 0, j))
    return pl.pallas_call(
        _ffn_up_kernel,
        out_shape=jax.ShapeDtypeStruct((bsz, t, n), BF16),
        grid=(bsz, t // tm, n // tn),
        in_specs=[pl.BlockSpec((None, tm, k), lambda b, i, j: (b, i, 0)), wspec, wspec],
        out_specs=pl.BlockSpec((None, tm, tn), lambda b, i, j: (b, i, j)),
        compiler_params=_params(("parallel", "parallel", "parallel"), V7X_VMEM_LIMIT),
        name="ffn_up",
    )(a, wg, wu)


def _gate_merge_kernel(h_ref, *refs):
    wg_refs, bg_refs, y_refs = refs[:N_BRANCH], refs[N_BRANCH:2 * N_BRANCH], refs[2 * N_BRANCH:3 * N_BRANCH]
    wb_ref, o_ref = refs[3 * N_BRANCH:]
    h = h_ref[...]
    acc = None
    for g in range(N_BRANCH):
        gate = jax.nn.sigmoid(_dot(h, wg_refs[g][...].astype(BF16)) + bg_refs[g][...])
        term = gate * _dot(y_refs[g][...], wb_ref[g].astype(BF16))
        acc = term if acc is None else acc + term
    o_ref[...] = acc.astype(o_ref.dtype)


def _gate_merge(h, wg, bg, ys, wb, l, tm, tn):
    bsz, t, k = h.shape
    d = wb.shape[-1]
    tm, tn = min(tm, t), min(tn, d)
    per = d // tn
    yspec = pl.BlockSpec((None, tm, MIX_W), lambda b, i, j: (b, i, 0))
    wspec = lambda g: pl.BlockSpec((None, k, tn), lambda b, i, j: (l, 0, g * per + j))
    bspec = lambda g: pl.BlockSpec((None, 1, tn), lambda b, i, j: (l, 0, g * per + j))
    branches = range(N_BRANCH)
    return pl.pallas_call(
        _gate_merge_kernel,
        out_shape=jax.ShapeDtypeStruct((bsz, t, d), BF16),
        grid=(bsz, t // tm, d // tn),
        in_specs=[pl.BlockSpec((None, tm, k), lambda b, i, j: (b, i, 0))]
                 + [wspec(g) for g in branches] + [bspec(g) for g in branches] + [yspec] * N_BRANCH
                 + [pl.BlockSpec((None, N_BRANCH, MIX_W, tn), lambda b, i, j: (l, 0, 0, j))],
        out_specs=pl.BlockSpec((None, tm, tn), lambda b, i, j: (b, i, j)),
        compiler_params=_params(("parallel", "parallel", "parallel"), V7X_VMEM_LIMIT),
        name="gate_merge",
    )(h, *([wg] * N_BRANCH), *([bg] * N_BRANCH), *ys, wb)


def _mlstm_kernel(q_ref, k_ref, v_ref, og_ref, gc_ref, gr_ref, bc_ref, br_ref, gh_ref, c0_ref, n0_ref, m0_ref,
                  y_ref, c_out, n_out, m_out, c_scr, n_scr, m_scr, *, chunk, valid):
    ci = pl.program_id(1)
    rows = q_ref.shape[0]

    @pl.when(ci == 0)
    def _():
        c_scr[...] = c0_ref[...]
        n_scr[...] = n0_ref[...]
        m_scr[...] = m0_ref[...]

    pos_c = lax.broadcasted_iota(jnp.int32, (chunk, 1), 0)
    pos_r = lax.broadcasted_iota(jnp.int32, (1, chunk), 1)
    tri_r = lax.broadcasted_iota(jnp.int32, (chunk, chunk), 0)
    tri_c = lax.broadcasted_iota(jnp.int32, (chunk, chunk), 1)
    causal = tri_c <= tri_r
    lower = causal.astype(F32)
    upper = (tri_r <= tri_c).astype(F32)

    gates_c = _pad_rows(gc_ref[...], chunk) + bc_ref[...]
    gates_r = gr_ref[...] + br_ref[...]
    lf_c = _log_sigmoid(gates_c)
    lf_r = _log_sigmoid(gates_r)
    if valid < chunk:
        lf_c = jnp.where(pos_c < valid, lf_c, 0.0)
        lf_r = jnp.where(pos_r < valid, lf_r, 0.0)
        gates_c = jnp.where(pos_c < valid, gates_c, -jnp.inf)
        gates_r = jnp.where(pos_r < valid, gates_r, -jnp.inf)
    cum_c = _dot_exact(lower, lf_c)
    cum_r = _dot_exact(lf_r, upper)

    heads = range(N_HEADS)
    sls = [slice(h * HEAD_DIM, (h + 1) * HEAD_DIM) for h in heads]
    qs = [_pad_rows(q_ref[:, sl], chunk) for sl in sls]
    ks = [_pad_rows(k_ref[:, sl], chunk) * (HEAD_DIM ** -0.5) for sl in sls]
    vs = [_pad_rows(v_ref[:, sl], chunk) for sl in sls]
    qbs, kbs, vbs = ([x.astype(BF16) for x in xs] for xs in (qs, ks, vs))
    b_cs = [cum_c[:, N_HEADS + h:N_HEADS + h + 1] for h in heads]
    b_rs = [cum_r[N_HEADS + h:N_HEADS + h + 1, :] for h in heads]
    m_prevs = [m_scr[h] for h in heads]
    c_prevs = [c_scr[h] for h in heads]
    n_prevs = [n_scr[h] for h in heads]

    qk = [_dot_nt(qbs[h], kbs[h]) for h in heads]
    cq = [_dot_nt(qbs[h], c_prevs[h].astype(BF16)) for h in heads]

    s_all, w_inters, m_ts = [], [], []
    for h in heads:
        d_intra = jnp.where(causal, b_cs[h] - b_rs[h] + gates_r[h:h + 1, :], -jnp.inf)
        m_inter = b_cs[h] + m_prevs[h]
        m_t = jnp.maximum(m_inter, jnp.max(d_intra, axis=1, keepdims=True))
        s_all.append(jnp.exp(d_intra - m_t) * qk[h])
        w_inters.append(jnp.exp(m_inter - m_t))
        m_ts.append(m_t)
    sv = [_dot(s_all[h].astype(BF16), vbs[h]) for h in heads]

    for h in heads:
        num = w_inters[h] * cq[h] + sv[h]
        den = (w_inters[h] * jnp.sum(qs[h] * n_prevs[h], axis=1, keepdims=True)
               + jnp.sum(s_all[h], axis=1, keepdims=True))
        hh = num / jnp.maximum(jnp.abs(den), jnp.exp(-m_ts[h]))
        hn = hh * lax.rsqrt(jnp.mean(hh * hh, axis=-1, keepdims=True) + EPS) * gh_ref[:, sls[h]]
        y = hn[:rows] * jax.nn.sigmoid(og_ref[:, sls[h]])
        y_ref[:, sls[h]] = y.astype(y_ref.dtype)

    w_ks, decays, m_ends = [], [], []
    for h in heads:
        b_end = b_cs[h][chunk - 1:chunk, :]
        log_w = b_end - b_cs[h] + gates_c[:, h:h + 1]
        m_end = jnp.maximum(b_end + m_prevs[h], jnp.max(log_w, axis=0, keepdims=True))
        w_ks.append(jnp.exp(log_w - m_end))
        decays.append(jnp.exp(b_end + m_prevs[h] - m_end))
        m_ends.append(m_end)
    kv = [lax.dot_general((w_ks[h] * vs[h]).astype(BF16), kbs[h], TN_DIMS, preferred_element_type=F32)
          for h in heads]
    for h in heads:
        c_scr[h] = decays[h] * c_prevs[h] + kv[h]
        n_scr[h] = decays[h] * n_prevs[h] + jnp.sum(w_ks[h] * ks[h], axis=0, keepdims=True)
        m_scr[h] = m_ends[h]

    @pl.when(ci == pl.num_programs(1) - 1)
    def _():
        c_out[...] = c_scr[...]
        n_out[...] = n_scr[...]
        m_out[...] = m_scr[...]


def _mlstm(proj, gates, gates_t, bias_c, bias_r, ghead, l, c0, n0, m0, chunk, rows, valid):
    bsz, t, _ = proj.shape
    nc = t // rows
    col = lambda c: pl.BlockSpec((None, rows, MIX_W), lambda b, i: (b, i, c))
    st4 = lambda s: pl.BlockSpec((None,) + s, lambda b, i: (b, 0, 0, 0))
    kern = functools.partial(_mlstm_kernel, chunk=chunk, valid=valid)
    return pl.pallas_call(
        kern,
        out_shape=(jax.ShapeDtypeStruct((bsz, t, MIX_W), BF16),
                   jax.ShapeDtypeStruct((bsz, N_HEADS, HEAD_DIM, HEAD_DIM), F32),
                   jax.ShapeDtypeStruct((bsz, N_HEADS, 1, HEAD_DIM), F32),
                   jax.ShapeDtypeStruct((bsz, N_HEADS, 1, 1), F32)),
        grid=(bsz, nc),
        in_specs=[col(COL_AQ), col(COL_AK), col(COL_AV), col(COL_AO),
                  pl.BlockSpec((None, rows, GATE_LANES), lambda b, i: (b, i, 0)),
                  pl.BlockSpec((None, SUBLANES, chunk), lambda b, i: (b, 0, i)),
                  pl.BlockSpec((None, 1, GATE_LANES), lambda b, i: (l, 0, 0)),
                  pl.BlockSpec((None, SUBLANES, 1), lambda b, i: (l, 0, 0)),
                  pl.BlockSpec((None, 1, MIX_W), lambda b, i: (l, 0, 0)),
                  st4((N_HEADS, HEAD_DIM, HEAD_DIM)), st4((N_HEADS, 1, HEAD_DIM)), st4((N_HEADS, 1, 1))],
        out_specs=(pl.BlockSpec((None, rows, MIX_W), lambda b, i: (b, i, 0)),
                   st4((N_HEADS, HEAD_DIM, HEAD_DIM)), st4((N_HEADS, 1, HEAD_DIM)), st4((N_HEADS, 1, 1))),
        scratch_shapes=[pltpu.VMEM((N_HEADS, HEAD_DIM, HEAD_DIM), F32),
                        pltpu.VMEM((N_HEADS, 1, HEAD_DIM), F32),
                        pltpu.VMEM((N_HEADS, 1, 1), F32)],
        compiler_params=_params(("parallel", "arbitrary"), V7X_VMEM_LIMIT),
        name="mlstm",
    )(proj, proj, proj, proj, gates, gates_t, bias_c, bias_r, ghead, c0, n0, m0)


def _sb_blocks(zs, carries, suffix, mask, chained):
    rows, n = zs[0].shape
    w = suffix.shape[0]
    nt = n // w
    tiles = [slice(t * w, (t + 1) * w) for t in range(nt)]
    ls_all, stacked, sums = [], [], []
    for z in zs:
        sp = jnp.log(1.0 + jnp.exp2(-jnp.abs(z))) * LOG2_E
        ls = jnp.minimum(z, 0.0) - sp
        ls_all.append(ls)
        l1m = ls - z
        if mask is not None:
            l1m = jnp.where(mask, l1m, 0.0)
        hi = l1m.astype(BF16)
        lo = (l1m - hi.astype(F32)).astype(BF16)
        stacked += [hi[:, s] for s in tiles] + [lo[:, s] for s in tiles]
        sums.append([jnp.sum(l1m[:, s], axis=1, keepdims=True) for s in tiles])
    ex = _dot(jnp.concatenate(stacked, axis=0), suffix)
    weights, out = [], []
    later = carries[0]
    for u in range(len(zs)):
        if not chained:
            later = carries[u]
        base = u * 2 * nt * rows
        parts = [None] * nt
        for t in reversed(range(nt)):
            hi_rows = ex[base + t * rows:base + (t + 1) * rows]
            lo_rows = ex[base + (nt + t) * rows:base + (nt + t + 1) * rows]
            parts[t] = hi_rows + lo_rows + later
            later = later + sums[u][t]
        excl = parts[0] if nt == 1 else jnp.concatenate(parts, axis=1)
        a = jnp.exp2(ls_all[u] + excl)
        if mask is not None:
            a = jnp.where(mask, a, 0.0)
        weights.append(a)
        out.append(later)
    return weights, out


def _strict_suffix_matrix(n):
    r = lax.broadcasted_iota(jnp.int32, (n, n), 0)
    c = lax.broadcasted_iota(jnp.int32, (n, n), 1)
    return (r > c).astype(BF16)


SB_HEADS_PER_STEP = 4
SB_HEADS_PER_MATMUL = 2


def _sb_prompt_kernel(bias_ref, q_ref, k_ref, v_ref, o_ref, *, blk):
    hp = pl.program_id(1)
    i = pl.program_id(2)
    scale = HEAD_DIM ** -0.5 * LOG2_E
    heads = range(SB_HEADS_PER_STEP)
    lanes = [slice(n * HEAD_DIM, (n + 1) * HEAD_DIM) for n in heads]
    bias = [bias_ref[hp * SB_HEADS_PER_STEP + n] * LOG2_E for n in heads]
    qb = [q_ref[:, lanes[n]].astype(BF16) for n in heads]
    suffix = _strict_suffix_matrix(blk)
    r = lax.broadcasted_iota(jnp.int32, (blk, blk), 0)
    cidx = lax.broadcasted_iota(jnp.int32, (blk, blk), 1)

    def step(jb, carry, mask):
        start = pl.multiple_of(jb * blk, blk)
        zs = [_dot_nt(qb[n], k_ref[pl.ds(start, blk), lanes[n]].astype(BF16)) * scale + bias[n] for n in heads]
        ws, cs = [], []
        for g in range(0, SB_HEADS_PER_STEP, SB_HEADS_PER_MATMUL):
            grp = range(g, g + SB_HEADS_PER_MATMUL)
            w_g, c_g = _sb_blocks([zs[n] for n in grp], [carry[n][0] for n in grp], suffix, mask, chained=False)
            ws += w_g
            cs += c_g
        return tuple((cs[n], carry[n][1] + _dot(ws[n].astype(BF16), v_ref[pl.ds(start, blk), lanes[n]].astype(BF16)))
                     for n in heads)

    zero = (jnp.zeros((blk, 1), F32), jnp.zeros((blk, HEAD_DIM), F32))
    carry = step(i, (zero,) * SB_HEADS_PER_STEP, cidx < r)
    carry = lax.fori_loop(0, i, lambda t, ca: step(i - 1 - t, ca, None), carry)
    for n in heads:
        o_ref[:, lanes[n]] = carry[n][1].astype(o_ref.dtype)


def _sb_prompt(proj, b_sb, l, blk=256):
    bsz, t, _ = proj.shape
    blk = min(blk, t)
    width = SB_HEADS_PER_STEP * HEAD_DIM
    groups = N_HEADS // SB_HEADS_PER_STEP
    kern = functools.partial(_sb_prompt_kernel, blk=blk)
    kv = lambda c0: pl.BlockSpec((None, t, width), lambda b, h, i: (b, 0, c0 * groups + h))
    return pl.pallas_call(
        kern,
        out_shape=jax.ShapeDtypeStruct((bsz, t, MIX_W), BF16),
        grid=(bsz, groups, t // blk),
        in_specs=[pl.BlockSpec(memory_space=pltpu.SMEM),
                  pl.BlockSpec((None, blk, width), lambda b, h, i: (b, i, COL_BQ * groups + h)),
                  kv(COL_BK), kv(COL_BV)],
        out_specs=pl.BlockSpec((None, blk, width), lambda b, h, i: (b, i, h)),
        compiler_params=_params(("parallel", "parallel", "arbitrary")),
        name="sb_prompt",
    )(b_sb[l], proj, proj, proj)


def _sb_sample_kernel(pt_ref, bias_ref, q_ref, kn_ref, vn_ref, suf_ref, *refs, pages_per_step):
    npg = pages_per_step
    k_refs, v_refs = refs[:npg], refs[npg:2 * npg]
    o_ref, c_scr, acc_scr = refs[2 * npg:]
    j = pl.program_id(1)
    tp = q_ref.shape[0]
    rows = N_HEADS * tp
    flat = k_refs[0].shape[0]
    scale = HEAD_DIM ** -0.5 * LOG2_E
    by_head = lambda ref: jnp.concatenate(
        [ref[:, h * HEAD_DIM:(h + 1) * HEAD_DIM] for h in range(N_HEADS)], axis=0)
    q_all = by_head(q_ref).astype(BF16)
    bias = jnp.concatenate([jnp.full((tp, 1), bias_ref[h] * LOG2_E, F32) for h in range(N_HEADS)], axis=0)

    def blocks(k_list, v_list, mask, c, acc):
        zs = [_dot_nt(q_all, kb) * scale + bias for kb in k_list]
        ws, cs = _sb_blocks(zs, [c], suf_ref[...], mask, chained=True)
        for a, vb in zip(ws, v_list):
            acc = acc + _dot(a.astype(BF16), vb)
        return cs[-1], acc

    @pl.when(j == 0)
    def _():
        r = lax.broadcasted_iota(jnp.int32, (rows, HEAD_DIM), 0)
        cidx = lax.broadcasted_iota(jnp.int32, (rows, HEAD_DIM), 1)
        mask = (cidx < rows) & (cidx // tp == r // tp) & (cidx % tp < r % tp)
        c, acc = blocks([_pad_rows(by_head(kn_ref), HEAD_DIM).astype(BF16)],
                        [_pad_rows(by_head(vn_ref), HEAD_DIM).astype(BF16)], mask,
                        jnp.zeros(c_scr.shape, F32), jnp.zeros(acc_scr.shape, F32))
        c_scr[...] = c
        acc_scr[...] = acc

    r = lax.broadcasted_iota(jnp.int32, (rows, flat), 0)
    cidx = lax.broadcasted_iota(jnp.int32, (rows, flat), 1)
    same_head = cidx % N_HEADS == r // tp
    c, acc = blocks([k_refs[p][...].astype(BF16) for p in range(npg)],
                    [v_refs[p][...].astype(BF16) for p in range(npg)], same_head, c_scr[...], acc_scr[...])
    c_scr[...] = c
    acc_scr[...] = acc

    @pl.when(j == pl.num_programs(1) - 1)
    def _():
        acc = acc_scr[...]
        for h in range(N_HEADS):
            o_ref[:, h * HEAD_DIM:(h + 1) * HEAD_DIM] = acc[h * tp:(h + 1) * tp].astype(o_ref.dtype)


def _sb_sample(proj, cache_k, cache_v, page_table, b_sb, l, pages_per_step=16):
    bsz, tp, _ = proj.shape
    n_pages = page_table.shape[1]
    depth, n_phys, page = cache_k.shape[:3]
    flat = page * N_HEADS
    npg = pages_per_step
    steps = n_pages // npg
    cache_k = cache_k.reshape(depth, n_phys, flat, HEAD_DIM)
    cache_v = cache_v.reshape(depth, n_phys, flat, HEAD_DIM)
    suffix = jnp.tril(jnp.ones((HEAD_DIM, HEAD_DIM), BF16), -1)

    def page_spec(p):
        def imap(b, j, pt):
            return (l, pt[b * n_pages + (n_pages - 1 - (j * npg + p))], 0, 0)
        return pl.BlockSpec((None, None, flat, HEAD_DIM), imap)

    col = lambda c: pl.BlockSpec((None, tp, MIX_W), lambda b, j, pt: (b, 0, c))
    kern = functools.partial(_sb_sample_kernel, pages_per_step=npg)
    return pl.pallas_call(
        kern,
        out_shape=jax.ShapeDtypeStruct((bsz, tp, MIX_W), BF16),
        grid_spec=pltpu.PrefetchScalarGridSpec(
            num_scalar_prefetch=1,
            grid=(bsz, steps),
            in_specs=[pl.BlockSpec(memory_space=pltpu.SMEM), col(COL_BQ), col(COL_BK), col(COL_BV),
                      pl.BlockSpec((HEAD_DIM, HEAD_DIM), lambda b, j, pt: (0, 0))]
                     + [page_spec(p) for p in range(npg)] * 2,
            out_specs=pl.BlockSpec((None, tp, MIX_W), lambda b, j, pt: (b, 0, 0)),
            scratch_shapes=[pltpu.VMEM((N_HEADS * tp, 1), F32), pltpu.VMEM((N_HEADS * tp, HEAD_DIM), F32)]),
        compiler_params=_params(("parallel", "arbitrary"), V7X_VMEM_LIMIT),
        name="sb_sample",
    )(page_table.reshape(-1), b_sb[l], proj, proj, proj, suffix, *([cache_k] * npg), *([cache_v] * npg))


def _cmul(ar, ai, br, bi):
    return ar * br - ai * bi, ar * bi + ai * br


def _s5_prep_kernel(lre_ref, lim_ref, ls_ref, bre_ref, bim_ref, tab_ref, bbre_ref, bbim_ref):
    lam_re, lam_im = lre_ref[...], lim_ref[...]
    step = jnp.exp(ls_ref[...])
    decay = jnp.exp(lam_re * step)
    a_re = decay * jnp.cos(lam_im * step)
    a_im = decay * jnp.sin(lam_im * step)
    inv = 1.0 / (lam_re * lam_re + lam_im * lam_im)
    f_re = ((a_re - 1.0) * lam_re + a_im * lam_im) * inv
    f_im = (a_im * lam_re - (a_re - 1.0) * lam_im) * inv
    b_re, b_im = bre_ref[...], bim_ref[...]
    bbre_ref[...] = f_re * b_re - f_im * b_im
    bbim_ref[...] = f_re * b_im + f_im * b_re

    pw = {1: (a_re, a_im)}
    pw[2] = _cmul(*pw[1], *pw[1])
    pw[3] = _cmul(*pw[2], *pw[1])
    pw[4] = _cmul(*pw[2], *pw[2])
    pw[5] = _cmul(*pw[4], *pw[1])
    pw[6] = _cmul(*pw[4], *pw[2])
    pw[7] = _cmul(*pw[4], *pw[3])
    pw[8] = _cmul(*pw[4], *pw[4])
    row = lax.broadcasted_iota(jnp.int32, (SUBLANES, lam_re.shape[1]), 0)
    for part in range(2):
        carry = jnp.zeros(row.shape, F32)
        for r in range(SUBLANES):
            carry = jnp.where(row == r, pw[r + 1][part], carry)
        tab_ref[part] = carry
        for idx, k in enumerate((1, 2, 4)):
            tab_ref[2 + 2 * idx + part] = jnp.where(row >= k, pw[k][part], 0.0)


def _s5_prep(lam_re, lam_im, log_step, b_re, b_im):
    depth = lam_re.shape[0]
    flat = lambda a: a.reshape(depth, 1, S5_WIDTH)
    bt = lambda a: a.transpose(0, 3, 1, 2).reshape(depth, S5_GROUP_CH, S5_WIDTH)
    step = jnp.broadcast_to(log_step[:, :, None], (depth, S5_GROUPS, S5_STATE))
    vec = pl.BlockSpec((None, 1, S5_WIDTH), lambda l: (l, 0, 0))
    mat = pl.BlockSpec((None, S5_GROUP_CH, S5_WIDTH), lambda l: (l, 0, 0))
    return pl.pallas_call(
        _s5_prep_kernel,
        out_shape=(jax.ShapeDtypeStruct((depth, 8, SUBLANES, S5_WIDTH), F32),
                   jax.ShapeDtypeStruct((depth, S5_GROUP_CH, S5_WIDTH), F32),
                   jax.ShapeDtypeStruct((depth, S5_GROUP_CH, S5_WIDTH), F32)),
        grid=(depth,),
        in_specs=[vec, vec, vec, mat, mat],
        out_specs=(pl.BlockSpec((None, 8, SUBLANES, S5_WIDTH), lambda l: (l, 0, 0, 0)), mat, mat),
        compiler_params=_params(("parallel",)),
        name="s5_prep",
    )(flat(lam_re), flat(lam_im), flat(step), bt(b_re), bt(b_im))


def _s5_kernel(u_ref, wb_ref, tab_ref, hre_ref, him_ref, wcre_ref, wcim_ref, d_ref, wglu_ref,
               y_ref, sre_out, sim_out, sre, sim, cre, cim, *, valid_last, slab):
    ci = pl.program_id(1)
    tc = u_ref.shape[0]
    sub = MIX_W // S5_SUPER
    wid = S5_WIDTH // S5_SUPER

    @pl.when(ci == 0)
    def _():
        cre[...] = jnp.broadcast_to(hre_ref[...], cre.shape)
        cim[...] = jnp.broadcast_to(him_ref[...], cim.shape)

    u = u_ref[...]
    ub = u.astype(BF16)
    for g in range(S5_SUPER):
        bu = _dot(ub[:, g * sub:(g + 1) * sub], wb_ref[g])
        sre[:, g * wid:(g + 1) * wid] = bu[:, :wid]
        sim[:, g * wid:(g + 1) * wid] = bu[:, wid:]

    for s0 in range(0, S5_WIDTH, slab):
        lanes = slice(s0, s0 + slab)
        pr, pi = tab_ref[0, :, lanes], tab_ref[1, :, lanes]
        levels = [(k, tab_ref[2 + 2 * idx, :, lanes], tab_ref[3 + 2 * idx, :, lanes])
                  for idx, k in enumerate((1, 2, 4))]

        def body(r, carry, lanes=lanes, pr=pr, pi=pi, levels=levels):
            c_re, c_im = carry
            row = pl.multiple_of(r * SUBLANES, SUBLANES)
            xr = sre[pl.ds(row, SUBLANES), lanes]
            xi = sim[pl.ds(row, SUBLANES), lanes]
            for k, mr, mi in levels:
                rr = pltpu.roll(xr, k, axis=0)
                ri = pltpu.roll(xi, k, axis=0)
                xr, xi = xr + mr * rr - mi * ri, xi + mr * ri + mi * rr
            xr, xi = xr + pr * c_re - pi * c_im, xi + pr * c_im + pi * c_re
            sre[pl.ds(row, SUBLANES), lanes] = xr
            sim[pl.ds(row, SUBLANES), lanes] = xi
            last = SUBLANES - 1
            return (jnp.broadcast_to(xr[last:last + 1, :], xr.shape),
                    jnp.broadcast_to(xi[last:last + 1, :], xi.shape))

        nblk = tc // SUBLANES
        c_re, c_im = lax.fori_loop(0, nblk, body, (cre[:, lanes], cim[:, lanes]), unroll=min(2, nblk))
        cre[:, lanes] = c_re
        cim[:, lanes] = c_im

    s_re_b = sre[...].astype(BF16)
    s_im_b = sim[...].astype(BF16)
    y = jnp.concatenate(
        [_dot(s_re_b[:, g * wid:(g + 1) * wid], wcre_ref[g]) - _dot(s_im_b[:, g * wid:(g + 1) * wid], wcim_ref[g])
         for g in range(S5_SUPER)], axis=1)
    y = y + d_ref[...] * u
    y = 0.5 * y * (1.0 + jnp.tanh(math.sqrt(2.0 / math.pi) * (y + 0.044715 * (y * y * y))))
    yy = _dot(y.astype(BF16), wglu_ref[...])
    y_ref[...] = (yy[:, :MIX_W] * jax.nn.sigmoid(yy[:, MIX_W:])).astype(y_ref.dtype)

    @pl.when(ci == pl.num_programs(1) - 1)
    def _():
        sre_out[...] = sre[valid_last - 1:valid_last, :]
        sim_out[...] = sim[valid_last - 1:valid_last, :]


def _s5(proj, wb, tab, h_re, h_im, wc_re, wc_im, d_skip, w_glu, l, tc, valid_last, slab=512):
    bsz, t, _ = proj.shape
    tc = min(tc, t)
    layer = lambda s: pl.BlockSpec((None,) + s, lambda b, i: (l,) + (0,) * len(s))
    st = pl.BlockSpec((None, 1, S5_WIDTH), lambda b, i: (b, 0, 0))
    kern = functools.partial(_s5_kernel, valid_last=valid_last, slab=slab)
    wid = S5_WIDTH // S5_SUPER
    return pl.pallas_call(
        kern,
        out_shape=(jax.ShapeDtypeStruct((bsz, t, MIX_W), BF16),
                   jax.ShapeDtypeStruct((bsz, 1, S5_WIDTH), F32),
                   jax.ShapeDtypeStruct((bsz, 1, S5_WIDTH), F32)),
        grid=(bsz, t // tc),
        in_specs=[pl.BlockSpec((None, tc, MIX_W), lambda b, i: (b, i, COL_SU)),
                  layer((S5_SUPER, MIX_W // S5_SUPER, 2 * wid)),
                  layer((8, SUBLANES, S5_WIDTH)),
                  st, st,
                  layer((S5_SUPER, wid, MIX_W // S5_SUPER)), layer((S5_SUPER, wid, MIX_W // S5_SUPER)),
                  layer((1, MIX_W)), layer((MIX_W, 2 * MIX_W))],
        out_specs=(pl.BlockSpec((None, tc, MIX_W), lambda b, i: (b, i, 0)), st, st),
        scratch_shapes=[pltpu.VMEM((tc, S5_WIDTH), F32), pltpu.VMEM((tc, S5_WIDTH), F32),
                        pltpu.VMEM((SUBLANES, S5_WIDTH), F32), pltpu.VMEM((SUBLANES, S5_WIDTH), F32)],
        compiler_params=_params(("parallel", "arbitrary"), V7X_VMEM_LIMIT),
        name="s5",
    )(proj, wb, tab, h_re, h_im, wc_re, wc_im, d_skip, w_glu)


def _block_diag_groups(w):
    depth, _, a, b = w.shape
    per = S5_GROUPS // S5_SUPER
    w = w.reshape(depth, S5_SUPER, per, a, b)
    eye = jnp.eye(per, dtype=w.dtype)
    bd = w[:, :, :, :, None, :] * eye[None, None, :, None, :, None]
    return bd.reshape(depth, S5_SUPER, per * a, per * b)


def _conv_kernel(gb_ref, gc_ref, xv_ref, hc_ref, hx_ref, buf_ref, w_ref, y_ref, new_ref, *, valid_last):
    ci = pl.program_id(1)
    tc = gb_ref.shape[0]
    z = gc_ref[...] * xv_ref[...]
    prev = jnp.where(ci == 0, buf_ref[...], hc_ref[...] * hx_ref[...])
    zz = jnp.concatenate([prev, z], axis=0)
    w = w_ref[...]
    y = sum(w[j:j + 1, :] * zz[SUBLANES - (CONV_K - 1) + j:SUBLANES - (CONV_K - 1) + j + tc] for j in range(CONV_K))
    y_ref[...] = (gb_ref[...] * y).astype(y_ref.dtype)

    @pl.when(ci == pl.num_programs(1) - 1)
    def _():
        end = SUBLANES + valid_last
        new_ref[...] = zz[end - (CONV_K - 1):end]


def _conv(proj, buf8, w, l, tc, valid_last):
    bsz, t, _ = proj.shape
    tc = min(tc, t)
    per = tc // SUBLANES
    col = lambda c: pl.BlockSpec((None, tc, MIX_W), lambda b, i: (b, i, c))
    halo = lambda c: pl.BlockSpec((None, SUBLANES, MIX_W), lambda b, i: (b, jnp.maximum(i * per - 1, 0), c))
    kern = functools.partial(_conv_kernel, valid_last=valid_last)
    return pl.pallas_call(
        kern,
        out_shape=(jax.ShapeDtypeStruct((bsz, t, MIX_W), BF16),
                   jax.ShapeDtypeStruct((bsz, CONV_K - 1, MIX_W), F32)),
        grid=(bsz, t // tc),
        in_specs=[col(COL_CB), col(COL_CC), col(COL_CX), halo(COL_CC), halo(COL_CX),
                  pl.BlockSpec((None, SUBLANES, MIX_W), lambda b, i: (b, 0, 0)),
                  pl.BlockSpec((None, CONV_K, MIX_W), lambda b, i: (l, 0, 0))],
        out_specs=(pl.BlockSpec((None, tc, MIX_W), lambda b, i: (b, i, 0)),
                   pl.BlockSpec((None, CONV_K - 1, MIX_W), lambda b, i: (b, 0, 0))),
        compiler_params=_params(("parallel", "arbitrary")),
        name="short_conv",
    )(proj, proj, proj, proj, proj, buf8, w)


def _layer(x, h, mod, next_norm, wts, l, past, sizes, kv_bufs):
    bsz, t, d = x.shape
    sh1, sc1, g1, sh2, sc2, g2 = mod
    tm = sizes["tm"]
    flat = sizes["flat"]
    as_mm = lambda a: a.reshape(flat + a.shape[2:])
    as_seq = lambda a: a.reshape((bsz, t) + a.shape[2:])
    mm_mod = lambda a: a if a.shape[1] == 1 else as_mm(a)
    mm_norm = lambda nrm: (nrm[0], nrm[1], mm_mod(nrm[2]), mm_mod(nrm[3]), nrm[4])

    hm = as_mm(h)
    proj, gates, kv_bufs = _in_proj(hm, wts["w_in"], l, kv_bufs, tm)
    proj, gates = as_seq(proj), as_seq(gates)
    gates_t = jnp.swapaxes(gates[:, :, :SUBLANES], 1, 2)
    if sizes["chunk"] > t:
        gates_t = jnp.pad(gates_t, ((0, 0), (0, 0), (0, sizes["chunk"] - t)))

    y_a, mc, mn, mm = _mlstm(proj, gates, gates_t, wts["gate_bias_c"], wts["gate_bias_r"], wts["g_head"], l,
                             past["mlstm_c"], past["mlstm_n"], past["mlstm_m"],
                             sizes["chunk"], sizes["rows"], sizes["valid"])
    if "sb_k" in past:
        y_b = _sb_sample(proj, past["sb_k"], past["sb_v"], past["page_table"], wts["b_sb"], l)
    else:
        y_b = _sb_prompt(proj, wts["b_sb"], l)
    y_s, s_re, s_im = _s5(proj, wts["s5_wb"], wts["s5_tab"], past["s5_re"], past["s5_im"],
                          wts["s5_wc_re"], wts["s5_wc_im"], wts["s5_d"], wts["w_s5_glu"], l,
                          sizes["tc"], sizes["valid_last"])
    y_c, conv_new = _conv(proj, past["conv"], wts["conv_w"], l, sizes["tc"], sizes["valid_last"])

    merged = _gate_merge(hm, wts["w_gate"], wts["b_gate"], [as_mm(y) for y in (y_a, y_b, y_s, y_c)],
                         wts["w_branch"], l, tm, 256)
    tmr = sizes["tm_res"]
    x_mm, h2 = _linear_residual_norm(merged, wts["w_out"], l, as_mm(x), mm_mod(g1),
                                     mm_norm((wts["g_norm2"], l, sc2, sh2, BF16)), tmr, 1, "out_proj")
    hidden = _ffn_up(h2, wts["w_ffn_gate"], wts["w_ffn_up"], l, tm, 512)
    x_mm, h_next = _linear_residual_norm(hidden, wts["w_ffn_down"], l, x_mm, mm_mod(g2), mm_norm(next_norm),
                                         tmr, 4, "ffn_down")

    state = {"mlstm_c": mc, "mlstm_n": mn.reshape(bsz, N_HEADS, HEAD_DIM), "mlstm_m": mm.reshape(bsz, N_HEADS),
             "s5_re": s_re.reshape(bsz, S5_GROUPS, S5_STATE), "s5_im": s_im.reshape(bsz, S5_GROUPS, S5_STATE),
             "conv": conv_new}
    return as_seq(x_mm), as_seq(h_next), state, kv_bufs


def kernel(x_prompt, x_sample, cache_sb_k, cache_sb_v, state_mlstm_c, state_mlstm_n, state_mlstm_m, state_s5_re, state_s5_im, state_conv, page_table, c_prompt, c_sample, w_ada, b_ada, g_norm1, g_norm2, w_in, b_mlstm_i, b_mlstm_f, g_mlstm_head, b_sb, s5_lambda_re, s5_lambda_im, s5_b_re, s5_b_im, s5_c_re, s5_c_im, s5_d, s5_log_step, w_s5_glu, conv_w, w_gate, b_gate, w_branch, w_out, w_ffn_gate, w_ffn_up, w_ffn_down, g_final):
    depth = w_in.shape[0]
    bp, tp, d = x_prompt.shape
    bs, ts, _ = x_sample.shape
    tpad = SUBLANES
    n_gate = N_GATE_COLS

    gate_bias = jnp.concatenate([b_mlstm_i, b_mlstm_f], axis=1)
    wts = {
        "g_norm1": g_norm1.reshape(depth, 1, d), "g_norm2": g_norm2.reshape(depth, 1, d),
        "w_in": w_in,
        "gate_bias_c": jnp.pad(gate_bias, ((0, 0), (0, GATE_LANES - n_gate))).reshape(depth, 1, GATE_LANES),
        "gate_bias_r": gate_bias.reshape(depth, n_gate, 1),
        "g_head": g_mlstm_head.reshape(depth, 1, MIX_W),
        "b_sb": b_sb,
        "s5_d": s5_d.reshape(depth, 1, MIX_W),
        "w_s5_glu": w_s5_glu.astype(BF16),
        "conv_w": conv_w,
        "w_gate": w_gate, "b_gate": b_gate.reshape(depth, 1, N_BRANCH * d), "w_branch": w_branch,
        "w_out": w_out.astype(BF16),
        "w_ffn_gate": w_ffn_gate, "w_ffn_up": w_ffn_up,
        "w_ffn_down": w_ffn_down.astype(BF16),
    }
    tc_p = 512
    wts["s5_tab"], bb_re, bb_im = _s5_prep(s5_lambda_re, s5_lambda_im, s5_log_step, s5_b_re, s5_b_im)
    to_gcp = lambda a: a.reshape(depth, S5_GROUP_CH, S5_GROUPS, S5_STATE).transpose(0, 2, 1, 3)
    wts["s5_wb"] = jnp.concatenate([_block_diag_groups(to_gcp(bb_re)), _block_diag_groups(to_gcp(bb_im))],
                                   axis=3).astype(BF16)
    wts["s5_wc_re"] = _block_diag_groups(s5_c_re.transpose(0, 1, 3, 2)).astype(BF16)
    wts["s5_wc_im"] = _block_diag_groups(s5_c_im.transpose(0, 1, 3, 2)).astype(BF16)

    c_all = jnp.concatenate([c_prompt, c_sample], axis=0)
    c_all = jnp.pad(c_all, ((0, -c_all.shape[0] % SUBLANES), (0, 0)))
    mod_all = _ada_all(c_all, w_ada, b_ada).reshape(depth, c_all.shape[0], 6, d)

    sizes_p = {"tm": 1024, "tm_res": 512, "flat": (bp, tp), "chunk": 256, "rows": 256, "valid": 256, "tc": tc_p,
               "valid_last": tc_p}
    sizes_s = {"tm": bs * tpad, "tm_res": bs * tpad, "flat": (1, bs * tpad), "chunk": 128, "rows": tpad, "valid": ts,
               "tc": tpad, "valid_last": ts}

    xp = x_prompt
    xs = jnp.pad(x_sample, ((0, 0), (0, tpad - ts), (0, 0)))
    zeros_p = {
        "mlstm_c": jnp.zeros((bp, N_HEADS, HEAD_DIM, HEAD_DIM), F32),
        "mlstm_n": jnp.zeros((bp, N_HEADS, 1, HEAD_DIM), F32),
        "mlstm_m": jnp.zeros((bp, N_HEADS, 1, 1), F32),
        "s5_re": jnp.zeros((bp, 1, S5_WIDTH), F32), "s5_im": jnp.zeros((bp, 1, S5_WIDTH), F32),
        "conv": jnp.zeros((bp, SUBLANES, MIX_W), F32),
    }
    new_p, new_s = [], []
    kv_p = kv_s = None
    mods_p = [[mod_all[l, :bp, i].reshape(bp, 1, d) for i in range(6)] for l in range(depth)]
    mods_s = [[jnp.repeat(mod_all[l, bp:bp + bs, i], tpad, axis=0).reshape(bs, tpad, d) for i in range(6)]
              for l in range(depth)]
    hp = _norm_mod(xp, wts["g_norm1"], 0, mods_p[0][1], mods_p[0][0])
    hs = _norm_mod(xs, wts["g_norm1"], 0, mods_s[0][1], mods_s[0][0])
    g_last = g_final.reshape(1, 1, d)

    def following_norm(mods, l, like):
        if l + 1 < depth:
            return wts["g_norm1"], l + 1, mods[l + 1][1], mods[l + 1][0], BF16
        return g_last, 0, jnp.zeros_like(like), jnp.zeros_like(like), F32

    for l in range(depth):
        mod_p, mod_s = mods_p[l], mods_s[l]
        past_s = {
            "sb_k": cache_sb_k, "sb_v": cache_sb_v, "page_table": page_table,
            "mlstm_c": state_mlstm_c[l], "mlstm_n": state_mlstm_n[l].reshape(bs, N_HEADS, 1, HEAD_DIM),
            "mlstm_m": state_mlstm_m[l].reshape(bs, N_HEADS, 1, 1),
            "s5_re": state_s5_re[l].reshape(bs, 1, S5_WIDTH), "s5_im": state_s5_im[l].reshape(bs, 1, S5_WIDTH),
            "conv": jnp.pad(state_conv[l], ((0, 0), (SUBLANES - (CONV_K - 1), 0), (0, 0))),
        }
        xp, hp, st_p, kv_p = _layer(xp, hp, mod_p, following_norm(mods_p, l, mod_p[0]), wts, l, zeros_p, sizes_p, kv_p)
        xs, hs, st_s, kv_s = _layer(xs, hs, mod_s, following_norm(mods_s, l, mod_s[0]), wts, l, past_s, sizes_s, kv_s)
        new_p.append(st_p)
        new_s.append(st_s)

    y_prompt = hp
    y_sample = hs[:, :ts]
    stk = lambda states, name: jnp.stack([s[name] for s in states])
    heads_p = lambda a: a.reshape(depth, bp, tp, N_HEADS, HEAD_DIM)
    heads_s = lambda a: a.reshape(depth, bs, tpad, N_HEADS, HEAD_DIM)[:, :, :ts]
    return (y_prompt, y_sample,
            heads_p(kv_p[0]), heads_p(kv_p[1]), heads_s(kv_s[0]), heads_s(kv_s[1]),
            stk(new_p, "mlstm_c"), stk(new_p, "mlstm_n"), stk(new_p, "mlstm_m"),
            stk(new_s, "mlstm_c"), stk(new_s, "mlstm_n"), stk(new_s, "mlstm_m"),
            stk(new_p, "s5_re"), stk(new_p, "s5_im"), stk(new_s, "s5_re"), stk(new_s, "s5_im"),
            stk(new_p, "conv"), stk(new_s, "conv"))
```

```python
import functools
import math

import jax
import jax.numpy as jnp
from jax import lax
from jax.experimental import pallas as pl
from jax.experimental.pallas import tpu as pltpu

F32 = jnp.float32
BF16 = jnp.bfloat16

N_HEADS = 4
HEAD_DIM = 128
MIX_W = N_HEADS * HEAD_DIM
N_BRANCH = 4
S5_GROUP_CH = 16
S5_GROUPS = MIX_W // S5_GROUP_CH
S5_STATE = 64
S5_WIDTH = S5_GROUPS * S5_STATE
S5_SUPER = 4
CONV_K = 3
EPS = 1e-6
LOG2_E = 1.0 / math.log(2.0)
SUBLANES = 8
LANES = 128
GATE_LANES = 128
V7X_VMEM_LIMIT = 56 * 1024 * 1024

COL_AQ, COL_AK, COL_AV, COL_AO, COL_BQ, COL_BK, COL_BV, COL_SU, COL_CB, COL_CC, COL_CX = range(11)
N_PROJ_BLOCKS = 11

NT_DIMS = (((1,), (1,)), ((), ()))
TN_DIMS = (((0,), (0,)), ((), ()))


def _dot(a, b):
    return jnp.dot(a, b, preferred_element_type=F32)


def _dot_nt(a, b):
    return lax.dot_general(a, b, NT_DIMS, preferred_element_type=F32)


def _dot_exact(a, b):
    return jnp.dot(a, b, precision=lax.Precision.HIGHEST, preferred_element_type=F32)


def _softplus_neg_abs(z):
    return jnp.log(1.0 + jnp.exp(-jnp.abs(z)))


def _log_sigmoid(z):
    return jnp.minimum(z, 0.0) - _softplus_neg_abs(z)


def _pad_rows(x, rows):
    if x.shape[0] == rows:
        return x
    return jnp.concatenate([x, jnp.zeros((rows - x.shape[0],) + x.shape[1:], x.dtype)], axis=0)


def _params(sem, vmem=None):
    return pltpu.CompilerParams(dimension_semantics=sem, vmem_limit_bytes=vmem)


def _ada_kernel(c_ref, w_ref, b_ref, o_ref):
    c = c_ref[...]
    a = (c * jax.nn.sigmoid(c)).astype(BF16)
    o_ref[...] = _dot(a, w_ref[...].astype(BF16)) + b_ref[...]


def _ada_all(c, w_ada, b_ada, tn=1024):
    depth, d, n = w_ada.shape
    rows = c.shape[0]
    return pl.pallas_call(
        _ada_kernel,
        out_shape=jax.ShapeDtypeStruct((depth, rows, n), F32),
        grid=(depth, n // tn),
        in_specs=[pl.BlockSpec((rows, d), lambda l, j: (0, 0)),
                  pl.BlockSpec((None, d, tn), lambda l, j: (l, 0, j)),
                  pl.BlockSpec((None, 1, tn), lambda l, j: (l, 0, j))],
        out_specs=pl.BlockSpec((None, rows, tn), lambda l, j: (l, 0, j)),
        compiler_params=_params(("parallel", "parallel"), V7X_VMEM_LIMIT),
        name="ada_mod",
    )(c, w_ada, b_ada.reshape(depth, 1, n))


def _norm_mod_kernel(x_ref, g_ref, sc_ref, sh_ref, o_ref):
    x = x_ref[...]
    y = x * lax.rsqrt(jnp.mean(x * x, axis=-1, keepdims=True) + EPS) * g_ref[...]
    o_ref[...] = (y * (1.0 + sc_ref[...]) + sh_ref[...]).astype(o_ref.dtype)


def _row_spec(arr, tm, width, col):
    if arr.shape[1] == 1:
        return pl.BlockSpec((None, 1, width), lambda b, i, *r: (b, 0, col(*r)))
    return pl.BlockSpec((None, tm, width), lambda b, i, *r: (b, i, col(*r)))


def _norm_mod(x, g, l, sc, sh, out_dtype=BF16, tm=512):
    bsz, t, d = x.shape
    tm = min(tm, t)
    zero = lambda *r: 0
    return pl.pallas_call(
        _norm_mod_kernel,
        out_shape=jax.ShapeDtypeStruct((bsz, t, d), out_dtype),
        grid=(bsz, t // tm),
        in_specs=[pl.BlockSpec((None, tm, d), lambda b, i: (b, i, 0)),
                  pl.BlockSpec((None, 1, d), lambda b, i: (l, 0, 0)),
                  _row_spec(sc, tm, d, zero), _row_spec(sh, tm, d, zero)],
        out_specs=pl.BlockSpec((None, tm, d), lambda b, i: (b, i, 0)),
        compiler_params=_params(("parallel", "parallel")),
        name="norm_mod",
    )(x, g, sc, sh)


N_GATE_COLS = 2 * N_HEADS


def _in_proj_kernel(a_ref, w_ref, wif_ref, *refs):
    o_ref, gates_ref, kb_ref, vb_ref = refs[-4:]
    j = pl.program_id(2)
    a = a_ref[...]
    res = _dot(a, w_ref[...])
    o_ref[...] = res

    @pl.when(j == 0)
    def _():
        gates_ref[...] = _dot(a, wif_ref[...])

    @pl.when(j == COL_BK)
    def _():
        kb_ref[...] = res

    @pl.when(j == COL_BV)
    def _():
        vb_ref[...] = res


def _in_proj(a, w, w_if, l, kv_bufs, tm):
    bsz, t, k = a.shape
    depth = w.shape[0]
    n = w.shape[-1]
    tm = min(tm, t)
    buf = jax.ShapeDtypeStruct((depth, bsz, t, MIX_W), F32)
    buf_spec = pl.BlockSpec((None, None, tm, MIX_W), lambda b, i, j: (l, b, i, 0))
    carried = [] if kv_bufs is None else list(kv_bufs)
    proj, gates, kbuf, vbuf = pl.pallas_call(
        _in_proj_kernel,
        out_shape=(jax.ShapeDtypeStruct((bsz, t, n), F32), jax.ShapeDtypeStruct((bsz, t, GATE_LANES), F32), buf, buf),
        grid=(bsz, t // tm, n // MIX_W),
        in_specs=[pl.BlockSpec((None, tm, k), lambda b, i, j: (b, i, 0)),
                  pl.BlockSpec((None, k, MIX_W), lambda b, i, j: (l, 0, j)),
                  pl.BlockSpec((None, k, GATE_LANES), lambda b, i, j: (l, 0, 0))]
                 + [pl.BlockSpec(memory_space=pl.ANY)] * len(carried),
        out_specs=(pl.BlockSpec((None, tm, MIX_W), lambda b, i, j: (b, i, j)),
                   pl.BlockSpec((None, tm, GATE_LANES), lambda b, i, j: (b, i, 0)), buf_spec, buf_spec),
        input_output_aliases={3 + n_: 2 + n_ for n_ in range(len(carried))},
        compiler_params=_params(("parallel", "parallel", "arbitrary"), V7X_VMEM_LIMIT),
        name="in_proj",
    )(a, w, w_if, *carried)
    return proj, gates, (kbuf, vbuf)


def _mm_res_kernel(a_ref, w_ref, x_ref, g_ref, o_ref, acc_ref, *, nk):
    part = _dot(a_ref[...], w_ref[...])
    kk = pl.program_id(3)

    @pl.when(kk == 0)
    def _():
        acc_ref[...] = part

    @pl.when((kk > 0) & (kk < nk - 1))
    def _():
        acc_ref[...] += part

    @pl.when(kk == nk - 1)
    def _():
        o_ref[...] = x_ref[...] + g_ref[...] * (acc_ref[...] + part)


def _linear_residual(a, w, l, x, g, tm, tn, nk, name):
    bsz, t, k = a.shape
    n = w.shape[-1]
    tm, tn = min(tm, t), min(tn, n)
    tk = k // nk
    return pl.pallas_call(
        functools.partial(_mm_res_kernel, nk=nk),
        out_shape=jax.ShapeDtypeStruct((bsz, t, n), F32),
        grid=(bsz, t // tm, n // tn, nk),
        in_specs=[pl.BlockSpec((None, tm, tk), lambda b, i, j, kk: (b, i, kk)),
                  pl.BlockSpec((None, tk, tn), lambda b, i, j, kk: (l, kk, j)),
                  pl.BlockSpec((None, tm, tn), lambda b, i, j, kk: (b, i, j)),
                  _row_spec(g, tm, tn, lambda j, kk: j)],
        out_specs=pl.BlockSpec((None, tm, tn), lambda b, i, j, kk: (b, i, j)),
        scratch_shapes=[pltpu.VMEM((tm, tn), F32)],
        compiler_params=_params(("parallel", "parallel", "parallel", "arbitrary"), V7X_VMEM_LIMIT),
        name=name,
    )(a, w, x, g)


def _mm_res_norm_kernel(a_ref, w_ref, x_ref, g_ref, gn_ref, sc_ref, sh_ref, xo_ref, ho_ref, *acc, nk):
    part = _dot(a_ref[...], w_ref[...])

    def finish(total):
        x = x_ref[...] + g_ref[...] * total
        xo_ref[...] = x
        y = x * lax.rsqrt(jnp.mean(x * x, axis=-1, keepdims=True) + EPS) * gn_ref[...]
        ho_ref[...] = (y * (1.0 + sc_ref[...]) + sh_ref[...]).astype(ho_ref.dtype)

    if nk == 1:
        finish(part)
        return
    acc_ref, = acc
    kk = pl.program_id(2)

    @pl.when(kk == 0)
    def _():
        acc_ref[...] = part

    @pl.when((kk > 0) & (kk < nk - 1))
    def _():
        acc_ref[...] += part

    @pl.when(kk == nk - 1)
    def _():
        finish(acc_ref[...] + part)


def _linear_residual_norm(a, w, l, x, g, norm, tm, nk, name):
    bsz, t, k = a.shape
    n = w.shape[-1]
    gn, ln, sc, sh, h_dtype = norm
    tm = min(tm, t)
    tk = k // nk
    zero = lambda kk: 0
    row = pl.BlockSpec((None, tm, n), lambda b, i, kk: (b, i, 0))
    return pl.pallas_call(
        functools.partial(_mm_res_norm_kernel, nk=nk),
        out_shape=(jax.ShapeDtypeStruct((bsz, t, n), F32), jax.ShapeDtypeStruct((bsz, t, n), h_dtype)),
        grid=(bsz, t // tm, nk),
        in_specs=[pl.BlockSpec((None, tm, tk), lambda b, i, kk: (b, i, kk)),
                  pl.BlockSpec((None, tk, n), lambda b, i, kk: (l, kk, 0)),
                  row, _row_spec(g, tm, n, zero),
                  pl.BlockSpec((None, 1, n), lambda b, i, kk: (ln, 0, 0)),
                  _row_spec(sc, tm, n, zero), _row_spec(sh, tm, n, zero)],
        out_specs=(row, row),
        scratch_shapes=[pltpu.VMEM((tm, n), F32)] if nk > 1 else [],
        compiler_params=_params(("parallel", "parallel", "arbitrary"), V7X_VMEM_LIMIT),
        name=name,
    )(a, w, x, g, gn, sc, sh)


def _first_row_tile():
    return (pl.program_id(1) == 0) & (pl.program_id(2) == 0)


def _ffn_up_kernel(a_ref, wg_ref, wu_ref, o_ref, wg_bf, wu_bf):
    @pl.when(_first_row_tile())
    def _():
        wg_bf[...] = wg_ref[...].astype(BF16)
        wu_bf[...] = wu_ref[...].astype(BF16)

    a = a_ref[...]
    gate = _dot(a, wg_bf[...])
    o_ref[...] = (gate * jax.nn.sigmoid(gate) * _dot(a, wu_bf[...])).astype(o_ref.dtype)


def _ffn_up(a, wg, wu, l, tm, tn):
    bsz, t, k = a.shape
    n = wg.shape[-1]
    tm, tn = min(tm, t), min(tn, n)
    wspec = pl.BlockSpec((None, k, tn), lambda j, b, i: (l, 0, j))
    return pl.pallas_call(
        _ffn_up_kernel,
        out_shape=jax.ShapeDtypeStruct((bsz, t, n), BF16),
        grid=(n // tn, bsz, t // tm),
        in_specs=[pl.BlockSpec((None, tm, k), lambda j, b, i: (b, i, 0)), wspec, wspec],
        out_specs=pl.BlockSpec((None, tm, tn), lambda j, b, i: (b, i, j)),
        scratch_shapes=[pltpu.VMEM((k, tn), BF16), pltpu.VMEM((k, tn), BF16)],
        compiler_params=_params(("arbitrary", "arbitrary", "arbitrary"), V7X_VMEM_LIMIT),
        name="ffn_up",
    )(a, wg, wu)


def _gate_merge_kernel(h_ref, *refs):
    wg_refs, bg_refs, y_refs = refs[:N_BRANCH], refs[N_BRANCH:2 * N_BRANCH], refs[2 * N_BRANCH:3 * N_BRANCH]
    wb_ref, o_ref, wg_bf, wb_bf = refs[3 * N_BRANCH:]

    @pl.when(_first_row_tile())
    def _():
        for g in range(N_BRANCH):
            wg_bf[g] = wg_refs[g][...].astype(BF16)
        wb_bf[...] = wb_ref[...].astype(BF16)

    h = h_ref[...]
    acc = None
    for g in range(N_BRANCH):
        gate = jax.nn.sigmoid(_dot(h, wg_bf[g]) + bg_refs[g][...])
        term = gate * _dot(y_refs[g][...], wb_bf[g])
        acc = term if acc is None else acc + term
    o_ref[...] = acc.astype(o_ref.dtype)


def _gate_merge(h, wg, bg, ys, wb, l, tm, tn):
    bsz, t, k = h.shape
    d = wb.shape[-1]
    tm, tn = min(tm, t), min(tn, d)
    per = d // tn
    yspec = pl.BlockSpec((None, tm, MIX_W), lambda j, b, i: (b, i, 0))
    wspec = lambda g: pl.BlockSpec((None, k, tn), lambda j, b, i: (l, 0, g * per + j))
    bspec = lambda g: pl.BlockSpec((None, 1, tn), lambda j, b, i: (l, 0, g * per + j))
    branches = range(N_BRANCH)
    return pl.pallas_call(
        _gate_merge_kernel,
        out_shape=jax.ShapeDtypeStruct((bsz, t, d), BF16),
        grid=(d // tn, bsz, t // tm),
        in_specs=[pl.BlockSpec((None, tm, k), lambda j, b, i: (b, i, 0))]
                 + [wspec(g) for g in branches] + [bspec(g) for g in branches] + [yspec] * N_BRANCH
                 + [pl.BlockSpec((None, N_BRANCH, MIX_W, tn), lambda j, b, i: (l, 0, 0, j))],
        out_specs=pl.BlockSpec((None, tm, tn), lambda j, b, i: (b, i, j)),
        scratch_shapes=[pltpu.VMEM((N_BRANCH, k, tn), BF16), pltpu.VMEM((N_BRANCH, MIX_W, tn), BF16)],
        compiler_params=_params(("arbitrary", "arbitrary", "arbitrary"), V7X_VMEM_LIMIT),
        name="gate_merge",
    )(h, *([wg] * N_BRANCH), *([bg] * N_BRANCH), *ys, wb)


def _mlstm_kernel(q_ref, k_ref, v_ref, og_ref, gc_ref, gr_ref, bc_ref, br_ref, gh_ref, c0_ref, n0_ref, m0_ref,
                  y_ref, c_out, n_out, m_out, c_scr, n_scr, m_scr, *, chunk, valid):
    ci = pl.program_id(1)
    rows = q_ref.shape[0]

    @pl.when(ci == 0)
    def _():
        c_scr[...] = c0_ref[...]
        n_scr[...] = n0_ref[...]
        m_scr[...] = m0_ref[...]

    pos_c = lax.broadcasted_iota(jnp.int32, (chunk, 1), 0)
    pos_r = lax.broadcasted_iota(jnp.int32, (1, chunk), 1)
    tri_r = lax.broadcasted_iota(jnp.int32, (chunk, chunk), 0)
    tri_c = lax.broadcasted_iota(jnp.int32, (chunk, chunk), 1)
    causal = tri_c <= tri_r
    lower = causal.astype(F32)
    upper = (tri_r <= tri_c).astype(F32)

    gates_c = _pad_rows(gc_ref[...], chunk) + bc_ref[...]
    gates_r = gr_ref[...] + br_ref[...]
    lf_c = _log_sigmoid(gates_c)
    lf_r = _log_sigmoid(gates_r)
    if valid < chunk:
        lf_c = jnp.where(pos_c < valid, lf_c, 0.0)
        lf_r = jnp.where(pos_r < valid, lf_r, 0.0)
        gates_c = jnp.where(pos_c < valid, gates_c, -jnp.inf)
        gates_r = jnp.where(pos_r < valid, gates_r, -jnp.inf)
    cum_c = _dot_exact(lower, lf_c)
    cum_r = _dot_exact(lf_r, upper)

    heads = range(N_HEADS)
    sls = [slice(h * HEAD_DIM, (h + 1) * HEAD_DIM) for h in heads]
    qs = [_pad_rows(q_ref[:, sl], chunk) for sl in sls]
    ks = [_pad_rows(k_ref[:, sl], chunk) * (HEAD_DIM ** -0.5) for sl in sls]
    vs = [_pad_rows(v_ref[:, sl], chunk) for sl in sls]
    qbs, kbs, vbs = ([x.astype(BF16) for x in xs] for xs in (qs, ks, vs))
    b_cs = [cum_c[:, N_HEADS + h:N_HEADS + h + 1] for h in heads]
    b_rs = [cum_r[N_HEADS + h:N_HEADS + h + 1, :] for h in heads]
    m_prevs = [m_scr[h] for h in heads]
    c_prevs = [c_scr[h] for h in heads]
    n_prevs = [n_scr[h] for h in heads]

    qk = [_dot_nt(qbs[h], kbs[h]) for h in heads]
    cq = [_dot_nt(qbs[h], c_prevs[h].astype(BF16)) for h in heads]

    s_all, w_inters, m_ts = [], [], []
    for h in heads:
        d_intra = jnp.where(causal, b_cs[h] - b_rs[h] + gates_r[h:h + 1, :], -jnp.inf)
        m_inter = b_cs[h] + m_prevs[h]
        m_t = jnp.maximum(m_inter, jnp.max(d_intra, axis=1, keepdims=True))
        s_all.append(jnp.exp(d_intra - m_t) * qk[h])
        w_inters.append(jnp.exp(m_inter - m_t))
        m_ts.append(m_t)
    sv = [_dot(s_all[h].astype(BF16), vbs[h]) for h in heads]

    for h in heads:
        num = w_inters[h] * cq[h] + sv[h]
        den = (w_inters[h] * jnp.sum(qs[h] * n_prevs[h], axis=1, keepdims=True)
               + jnp.sum(s_all[h], axis=1, keepdims=True))
        hh = num / jnp.maximum(jnp.abs(den), jnp.exp(-m_ts[h]))
        hn = hh * lax.rsqrt(jnp.mean(hh * hh, axis=-1, keepdims=True) + EPS) * gh_ref[:, sls[h]]
        y = hn[:rows] * jax.nn.sigmoid(og_ref[:, sls[h]])
        y_ref[:, sls[h]] = y.astype(y_ref.dtype)

    w_ks, decays, m_ends = [], [], []
    for h in heads:
        b_end = b_cs[h][chunk - 1:chunk, :]
        log_w = b_end - b_cs[h] + gates_c[:, h:h + 1]
        m_end = jnp.maximum(b_end + m_prevs[h], jnp.max(log_w, axis=0, keepdims=True))
        w_ks.append(jnp.exp(log_w - m_end))
        decays.append(jnp.exp(b_end + m_prevs[h] - m_end))
        m_ends.append(m_end)
    kv = [lax.dot_general((w_ks[h] * vs[h]).astype(BF16), kbs[h], TN_DIMS, preferred_element_type=F32)
          for h in heads]
    for h in heads:
        c_scr[h] = decays[h] * c_prevs[h] + kv[h]
        n_scr[h] = decays[h] * n_prevs[h] + jnp.sum(w_ks[h] * ks[h], axis=0, keepdims=True)
        m_scr[h] = m_ends[h]

    @pl.when(ci == pl.num_programs(1) - 1)
    def _():
        c_out[...] = c_scr[...]
        n_out[...] = n_scr[...]
        m_out[...] = m_scr[...]


def _mlstm(proj, gates, gates_t, bias_c, bias_r, ghead, l, c0, n0, m0, chunk, rows, valid):
    bsz, t, _ = proj.shape
    nc = t // rows
    col = lambda c: pl.BlockSpec((None, rows, MIX_W), lambda b, i: (b, i, c))
    st4 = lambda s: pl.BlockSpec((None,) + s, lambda b, i: (b, 0, 0, 0))
    kern = functools.partial(_mlstm_kernel, chunk=chunk, valid=valid)
    return pl.pallas_call(
        kern,
        out_shape=(jax.ShapeDtypeStruct((bsz, t, MIX_W), BF16),
                   jax.ShapeDtypeStruct((bsz, N_HEADS, HEAD_DIM, HEAD_DIM), F32),
                   jax.ShapeDtypeStruct((bsz, N_HEADS, 1, HEAD_DIM), F32),
                   jax.ShapeDtypeStruct((bsz, N_HEADS, 1, 1), F32)),
        grid=(bsz, nc),
        in_specs=[col(COL_AQ), col(COL_AK), col(COL_AV), col(COL_AO),
                  pl.BlockSpec((None, rows, GATE_LANES), lambda b, i: (b, i, 0)),
                  pl.BlockSpec((None, SUBLANES, chunk), lambda b, i: (b, 0, i)),
                  pl.BlockSpec((None, 1, GATE_LANES), lambda b, i: (l, 0, 0)),
                  pl.BlockSpec((None, SUBLANES, 1), lambda b, i: (l, 0, 0)),
                  pl.BlockSpec((None, 1, MIX_W), lambda b, i: (l, 0, 0)),
                  st4((N_HEADS, HEAD_DIM, HEAD_DIM)), st4((N_HEADS, 1, HEAD_DIM)), st4((N_HEADS, 1, 1))],
        out_specs=(pl.BlockSpec((None, rows, MIX_W), lambda b, i: (b, i, 0)),
                   st4((N_HEADS, HEAD_DIM, HEAD_DIM)), st4((N_HEADS, 1, HEAD_DIM)), st4((N_HEADS, 1, 1))),
        scratch_shapes=[pltpu.VMEM((N_HEADS, HEAD_DIM, HEAD_DIM), F32),
                        pltpu.VMEM((N_HEADS, 1, HEAD_DIM), F32),
                        pltpu.VMEM((N_HEADS, 1, 1), F32)],
        compiler_params=_params(("parallel", "arbitrary"), V7X_VMEM_LIMIT),
        name="mlstm",
    )(proj, proj, proj, proj, gates, gates_t, bias_c, bias_r, ghead, c0, n0, m0)


def _sb_blocks(zs, carries, suffix, mask, chained):
    rows, n = zs[0].shape
    w = suffix.shape[0]
    nt = n // w
    tiles = [slice(t * w, (t + 1) * w) for t in range(nt)]
    ls_all, stacked, sums = [], [], []
    for z in zs:
        sp = jnp.log(1.0 + jnp.exp2(-jnp.abs(z))) * LOG2_E
        ls = jnp.minimum(z, 0.0) - sp
        ls_all.append(ls)
        l1m = ls - z
        if mask is not None:
            l1m = jnp.where(mask, l1m, 0.0)
        hi = l1m.astype(BF16)
        lo = (l1m - hi.astype(F32)).astype(BF16)
        stacked += [hi[:, s] for s in tiles] + [lo[:, s] for s in tiles]
        sums.append([jnp.sum(l1m[:, s], axis=1, keepdims=True) for s in tiles])
    ex = _dot(jnp.concatenate(stacked, axis=0), suffix)
    weights, out = [], []
    later = carries[0]
    for u in range(len(zs)):
        if not chained:
            later = carries[u]
        base = u * 2 * nt * rows
        parts = [None] * nt
        for t in reversed(range(nt)):
            hi_rows = ex[base + t * rows:base + (t + 1) * rows]
            lo_rows = ex[base + (nt + t) * rows:base + (nt + t + 1) * rows]
            parts[t] = hi_rows + lo_rows + later
            later = later + sums[u][t]
        excl = parts[0] if nt == 1 else jnp.concatenate(parts, axis=1)
        a = jnp.exp2(ls_all[u] + excl)
        if mask is not None:
            a = jnp.where(mask, a, 0.0)
        weights.append(a)
        out.append(later)
    return weights, out


def _strict_suffix_matrix(n):
    r = lax.broadcasted_iota(jnp.int32, (n, n), 0)
    c = lax.broadcasted_iota(jnp.int32, (n, n), 1)
    return (r > c).astype(BF16)


SB_HEADS_PER_STEP = 4
SB_HEADS_PER_MATMUL = 2


def _sb_prompt_kernel(bias_ref, q_ref, k_ref, v_ref, o_ref, *, blk):
    hp = pl.program_id(1)
    i = pl.program_id(2)
    scale = HEAD_DIM ** -0.5 * LOG2_E
    heads = range(SB_HEADS_PER_STEP)
    lanes = [slice(n * HEAD_DIM, (n + 1) * HEAD_DIM) for n in heads]
    bias = [bias_ref[hp * SB_HEADS_PER_STEP + n] * LOG2_E for n in heads]
    qb = [q_ref[:, lanes[n]].astype(BF16) for n in heads]
    suffix = _strict_suffix_matrix(blk)
    r = lax.broadcasted_iota(jnp.int32, (blk, blk), 0)
    cidx = lax.broadcasted_iota(jnp.int32, (blk, blk), 1)

    def step(jb, carry, mask):
        start = pl.multiple_of(jb * blk, blk)
        zs = [_dot_nt(qb[n], k_ref[pl.ds(start, blk), lanes[n]].astype(BF16)) * scale + bias[n] for n in heads]
        ws, cs = [], []
        for g in range(0, SB_HEADS_PER_STEP, SB_HEADS_PER_MATMUL):
            grp = range(g, g + SB_HEADS_PER_MATMUL)
            w_g, c_g = _sb_blocks([zs[n] for n in grp], [carry[n][0] for n in grp], suffix, mask, chained=False)
            ws += w_g
            cs += c_g
        return tuple((cs[n], carry[n][1] + _dot(ws[n].astype(BF16), v_ref[pl.ds(start, blk), lanes[n]].astype(BF16)))
                     for n in heads)

    zero = (jnp.zeros((blk, 1), F32), jnp.zeros((blk, HEAD_DIM), F32))
    carry = step(i, (zero,) * SB_HEADS_PER_STEP, cidx < r)
    carry = lax.fori_loop(0, i, lambda t, ca: step(i - 1 - t, ca, None), carry)
    for n in heads:
        o_ref[:, lanes[n]] = carry[n][1].astype(o_ref.dtype)


def _sb_prompt(proj, b_sb, l, blk=256):
    bsz, t, _ = proj.shape
    blk = min(blk, t)
    width = SB_HEADS_PER_STEP * HEAD_DIM
    groups = N_HEADS // SB_HEADS_PER_STEP
    kern = functools.partial(_sb_prompt_kernel, blk=blk)
    kv = lambda c0: pl.BlockSpec((None, t, width), lambda b, h, i: (b, 0, c0 * groups + h))
    return pl.pallas_call(
        kern,
        out_shape=jax.ShapeDtypeStruct((bsz, t, MIX_W), BF16),
        grid=(bsz, groups, t // blk),
        in_specs=[pl.BlockSpec(memory_space=pltpu.SMEM),
                  pl.BlockSpec((None, blk, width), lambda b, h, i: (b, i, COL_BQ * groups + h)),
                  kv(COL_BK), kv(COL_BV)],
        out_specs=pl.BlockSpec((None, blk, width), lambda b, h, i: (b, i, h)),
        compiler_params=_params(("parallel", "parallel", "arbitrary")),
        name="sb_prompt",
    )(b_sb[l], proj, proj, proj)


def _sb_sample_kernel(pt_ref, bias_ref, q_ref, kn_ref, vn_ref, suf_ref, *refs, pages_per_step):
    npg = pages_per_step
    k_refs, v_refs = refs[:npg], refs[npg:2 * npg]
    o_ref, c_scr, acc_scr = refs[2 * npg:]
    j = pl.program_id(1)
    tp = q_ref.shape[0]
    rows = N_HEADS * tp
    flat = k_refs[0].shape[0]
    scale = HEAD_DIM ** -0.5 * LOG2_E
    by_head = lambda ref: jnp.concatenate(
        [ref[:, h * HEAD_DIM:(h + 1) * HEAD_DIM] for h in range(N_HEADS)], axis=0)
    q_all = by_head(q_ref).astype(BF16)
    bias = jnp.concatenate([jnp.full((tp, 1), bias_ref[h] * LOG2_E, F32) for h in range(N_HEADS)], axis=0)

    def blocks(k_list, v_list, mask, c, acc):
        zs = [_dot_nt(q_all, kb) * scale + bias for kb in k_list]
        ws, cs = _sb_blocks(zs, [c], suf_ref[...], mask, chained=True)
        for a, vb in zip(ws, v_list):
            acc = acc + _dot(a.astype(BF16), vb)
        return cs[-1], acc

    @pl.when(j == 0)
    def _():
        r = lax.broadcasted_iota(jnp.int32, (rows, HEAD_DIM), 0)
        cidx = lax.broadcasted_iota(jnp.int32, (rows, HEAD_DIM), 1)
        mask = (cidx < rows) & (cidx // tp == r // tp) & (cidx % tp < r % tp)
        c, acc = blocks([_pad_rows(by_head(kn_ref), HEAD_DIM).astype(BF16)],
                        [_pad_rows(by_head(vn_ref), HEAD_DIM).astype(BF16)], mask,
                        jnp.zeros(c_scr.shape, F32), jnp.zeros(acc_scr.shape, F32))
        c_scr[...] = c
        acc_scr[...] = acc

    r = lax.broadcasted_iota(jnp.int32, (rows, flat), 0)
    cidx = lax.broadcasted_iota(jnp.int32, (rows, flat), 1)
    same_head = cidx % N_HEADS == r // tp
    c, acc = blocks([k_refs[p][...].astype(BF16) for p in range(npg)],
                    [v_refs[p][...].astype(BF16) for p in range(npg)], same_head, c_scr[...], acc_scr[...])
    c_scr[...] = c
    acc_scr[...] = acc

    @pl.when(j == pl.num_programs(1) - 1)
    def _():
        acc = acc_scr[...]
        for h in range(N_HEADS):
            o_ref[:, h * HEAD_DIM:(h + 1) * HEAD_DIM] = acc[h * tp:(h + 1) * tp].astype(o_ref.dtype)


def _sb_sample(proj, cache_k, cache_v, page_table, b_sb, l, pages_per_step=16):
    bsz, tp, _ = proj.shape
    n_pages = page_table.shape[1]
    depth, n_phys, page = cache_k.shape[:3]
    flat = page * N_HEADS
    npg = pages_per_step
    steps = n_pages // npg
    cache_k = cache_k.reshape(depth, n_phys, flat, HEAD_DIM)
    cache_v = cache_v.reshape(depth, n_phys, flat, HEAD_DIM)
    suffix = jnp.tril(jnp.ones((HEAD_DIM, HEAD_DIM), BF16), -1)

    def page_spec(p):
        def imap(b, j, pt):
            return (l, pt[b * n_pages + (n_pages - 1 - (j * npg + p))], 0, 0)
        return pl.BlockSpec((None, None, flat, HEAD_DIM), imap)

    col = lambda c: pl.BlockSpec((None, tp, MIX_W), lambda b, j, pt: (b, 0, c))
    kern = functools.partial(_sb_sample_kernel, pages_per_step=npg)
    return pl.pallas_call(
        kern,
        out_shape=jax.ShapeDtypeStruct((bsz, tp, MIX_W), BF16),
        grid_spec=pltpu.PrefetchScalarGridSpec(
            num_scalar_prefetch=1,
            grid=(bsz, steps),
            in_specs=[pl.BlockSpec(memory_space=pltpu.SMEM), col(COL_BQ), col(COL_BK), col(COL_BV),
                      pl.BlockSpec((HEAD_DIM, HEAD_DIM), lambda b, j, pt: (0, 0))]
                     + [page_spec(p) for p in range(npg)] * 2,
            out_specs=pl.BlockSpec((None, tp, MIX_W), lambda b, j, pt: (b, 0, 0)),
            scratch_shapes=[pltpu.VMEM((N_HEADS * tp, 1), F32), pltpu.VMEM((N_HEADS * tp, HEAD_DIM), F32)]),
        compiler_params=_params(("parallel", "arbitrary"), V7X_VMEM_LIMIT),
        name="sb_sample",
    )(page_table.reshape(-1), b_sb[l], proj, proj, proj, suffix, *([cache_k] * npg), *([cache_v] * npg))


def _cmul(ar, ai, br, bi):
    return ar * br - ai * bi, ar * bi + ai * br


def _s5_prep_kernel(lre_ref, lim_ref, ls_ref, bre_ref, bim_ref, tab_ref, bbre_ref, bbim_ref):
    lam_re, lam_im = lre_ref[...], lim_ref[...]
    step = jnp.exp(ls_ref[...])
    decay = jnp.exp(lam_re * step)
    a_re = decay * jnp.cos(lam_im * step)
    a_im = decay * jnp.sin(lam_im * step)
    inv = 1.0 / (lam_re * lam_re + lam_im * lam_im)
    f_re = ((a_re - 1.0) * lam_re + a_im * lam_im) * inv
    f_im = (a_im * lam_re - (a_re - 1.0) * lam_im) * inv
    b_re, b_im = bre_ref[...], bim_ref[...]
    bbre_ref[...] = f_re * b_re - f_im * b_im
    bbim_ref[...] = f_re * b_im + f_im * b_re

    pw = {1: (a_re, a_im)}
    pw[2] = _cmul(*pw[1], *pw[1])
    pw[3] = _cmul(*pw[2], *pw[1])
    pw[4] = _cmul(*pw[2], *pw[2])
    pw[5] = _cmul(*pw[4], *pw[1])
    pw[6] = _cmul(*pw[4], *pw[2])
    pw[7] = _cmul(*pw[4], *pw[3])
    pw[8] = _cmul(*pw[4], *pw[4])
    row = lax.broadcasted_iota(jnp.int32, (SUBLANES, lam_re.shape[1]), 0)
    for part in range(2):
        carry = jnp.zeros(row.shape, F32)
        for r in range(SUBLANES):
            carry = jnp.where(row == r, pw[r + 1][part], carry)
        tab_ref[part] = carry
        for idx, k in enumerate((1, 2, 4)):
            tab_ref[2 + 2 * idx + part] = jnp.where(row >= k, pw[k][part], 0.0)


def _s5_prep(lam_re, lam_im, log_step, b_re, b_im):
    depth = lam_re.shape[0]
    flat = lambda a: a.reshape(depth, 1, S5_WIDTH)
    bt = lambda a: a.transpose(0, 3, 1, 2).reshape(depth, S5_GROUP_CH, S5_WIDTH)
    step = jnp.broadcast_to(log_step[:, :, None], (depth, S5_GROUPS, S5_STATE))
    vec = pl.BlockSpec((None, 1, S5_WIDTH), lambda l: (l, 0, 0))
    mat = pl.BlockSpec((None, S5_GROUP_CH, S5_WIDTH), lambda l: (l, 0, 0))
    return pl.pallas_call(
        _s5_prep_kernel,
        out_shape=(jax.ShapeDtypeStruct((depth, 8, SUBLANES, S5_WIDTH), F32),
                   jax.ShapeDtypeStruct((depth, S5_GROUP_CH, S5_WIDTH), F32),
                   jax.ShapeDtypeStruct((depth, S5_GROUP_CH, S5_WIDTH), F32)),
        grid=(depth,),
        in_specs=[vec, vec, vec, mat, mat],
        out_specs=(pl.BlockSpec((None, 8, SUBLANES, S5_WIDTH), lambda l: (l, 0, 0, 0)), mat, mat),
        compiler_params=_params(("parallel",)),
        name="s5_prep",
    )(flat(lam_re), flat(lam_im), flat(step), bt(b_re), bt(b_im))


def _s5_kernel(u_ref, wb_ref, tab_ref, hre_ref, him_ref, wcre_ref, wcim_ref, d_ref, wglu_ref,
               y_ref, sre_out, sim_out, sre, sim, cre, cim, *, valid_last, slab):
    ci = pl.program_id(1)
    tc = u_ref.shape[0]
    sub = MIX_W // S5_SUPER
    wid = S5_WIDTH // S5_SUPER

    @pl.when(ci == 0)
    def _():
        cre[...] = jnp.broadcast_to(hre_ref[...], cre.shape)
        cim[...] = jnp.broadcast_to(him_ref[...], cim.shape)

    u = u_ref[...]
    ub = u.astype(BF16)
    for g in range(S5_SUPER):
        bu = _dot(ub[:, g * sub:(g + 1) * sub], wb_ref[g])
        sre[:, g * wid:(g + 1) * wid] = bu[:, :wid]
        sim[:, g * wid:(g + 1) * wid] = bu[:, wid:]

    for s0 in range(0, S5_WIDTH, slab):
        lanes = slice(s0, s0 + slab)
        pr, pi = tab_ref[0, :, lanes], tab_ref[1, :, lanes]
        levels = [(k, tab_ref[2 + 2 * idx, :, lanes], tab_ref[3 + 2 * idx, :, lanes])
                  for idx, k in enumerate((1, 2, 4))]

        def body(r, carry, lanes=lanes, pr=pr, pi=pi, levels=levels):
            c_re, c_im = carry
            row = pl.multiple_of(r * SUBLANES, SUBLANES)
            xr = sre[pl.ds(row, SUBLANES), lanes]
            xi = sim[pl.ds(row, SUBLANES), lanes]
            for k, mr, mi in levels:
                rr = pltpu.roll(xr, k, axis=0)
                ri = pltpu.roll(xi, k, axis=0)
                xr, xi = xr + mr * rr - mi * ri, xi + mr * ri + mi * rr
            xr, xi = xr + pr * c_re - pi * c_im, xi + pr * c_im + pi * c_re
            sre[pl.ds(row, SUBLANES), lanes] = xr
            sim[pl.ds(row, SUBLANES), lanes] = xi
            last = SUBLANES - 1
            return (jnp.broadcast_to(xr[last:last + 1, :], xr.shape),
                    jnp.broadcast_to(xi[last:last + 1, :], xi.shape))

        nblk = tc // SUBLANES
        c_re, c_im = lax.fori_loop(0, nblk, body, (cre[:, lanes], cim[:, lanes]), unroll=min(2, nblk))
        cre[:, lanes] = c_re
        cim[:, lanes] = c_im

    s_re_b = sre[...].astype(BF16)
    s_im_b = sim[...].astype(BF16)
    y = jnp.concatenate(
        [_dot(s_re_b[:, g * wid:(g + 1) * wid], wcre_ref[g]) - _dot(s_im_b[:, g * wid:(g + 1) * wid], wcim_ref[g])
         for g in range(S5_SUPER)], axis=1)
    y = y + d_ref[...] * u
    y = 0.5 * y * (1.0 + jnp.tanh(math.sqrt(2.0 / math.pi) * (y + 0.044715 * (y * y * y))))
    yy = _dot(y.astype(BF16), wglu_ref[...])
    y_ref[...] = (yy[:, :MIX_W] * jax.nn.sigmoid(yy[:, MIX_W:])).astype(y_ref.dtype)

    @pl.when(ci == pl.num_programs(1) - 1)
    def _():
        sre_out[...] = sre[valid_last - 1:valid_last, :]
        sim_out[...] = sim[valid_last - 1:valid_last, :]


def _s5(proj, wb, tab, h_re, h_im, wc_re, wc_im, d_skip, w_glu, l, tc, valid_last, slab=512):
    bsz, t, _ = proj.shape
    tc = min(tc, t)
    layer = lambda s: pl.BlockSpec((None,) + s, lambda b, i: (l,) + (0,) * len(s))
    st = pl.BlockSpec((None, 1, S5_WIDTH), lambda b, i: (b, 0, 0))
    kern = functools.partial(_s5_kernel, valid_last=valid_last, slab=slab)
    wid = S5_WIDTH // S5_SUPER
    return pl.pallas_call(
        kern,
        out_shape=(jax.ShapeDtypeStruct((bsz, t, MIX_W), BF16),
                   jax.ShapeDtypeStruct((bsz, 1, S5_WIDTH), F32),
                   jax.ShapeDtypeStruct((bsz, 1, S5_WIDTH), F32)),
        grid=(bsz, t // tc),
        in_specs=[pl.BlockSpec((None, tc, MIX_W), lambda b, i: (b, i, COL_SU)),
                  layer((S5_SUPER, MIX_W // S5_SUPER, 2 * wid)),
                  layer((8, SUBLANES, S5_WIDTH)),
                  st, st,
                  layer((S5_SUPER, wid, MIX_W // S5_SUPER)), layer((S5_SUPER, wid, MIX_W // S5_SUPER)),
                  layer((1, MIX_W)), layer((MIX_W, 2 * MIX_W))],
        out_specs=(pl.BlockSpec((None, tc, MIX_W), lambda b, i: (b, i, 0)), st, st),
        scratch_shapes=[pltpu.VMEM((tc, S5_WIDTH), F32), pltpu.VMEM((tc, S5_WIDTH), F32),
                        pltpu.VMEM((SUBLANES, S5_WIDTH), F32), pltpu.VMEM((SUBLANES, S5_WIDTH), F32)],
        compiler_params=_params(("parallel", "arbitrary"), V7X_VMEM_LIMIT),
        name="s5",
    )(proj, wb, tab, h_re, h_im, wc_re, wc_im, d_skip, w_glu)


def _block_diag_groups(w):
    depth, _, a, b = w.shape
    per = S5_GROUPS // S5_SUPER
    w = w.reshape(depth, S5_SUPER, per, a, b)
    eye = jnp.eye(per, dtype=w.dtype)
    bd = w[:, :, :, :, None, :] * eye[None, None, :, None, :, None]
    return bd.reshape(depth, S5_SUPER, per * a, per * b)


def _conv_kernel(gb_ref, gc_ref, xv_ref, hc_ref, hx_ref, buf_ref, w_ref, y_ref, new_ref, *, valid_last):
    ci = pl.program_id(1)
    tc = gb_ref.shape[0]
    z = gc_ref[...] * xv_ref[...]
    prev = jnp.where(ci == 0, buf_ref[...], hc_ref[...] * hx_ref[...])
    zz = jnp.concatenate([prev, z], axis=0)
    w = w_ref[...]
    y = sum(w[j:j + 1, :] * zz[SUBLANES - (CONV_K - 1) + j:SUBLANES - (CONV_K - 1) + j + tc] for j in range(CONV_K))
    y_ref[...] = (gb_ref[...] * y).astype(y_ref.dtype)

    @pl.when(ci == pl.num_programs(1) - 1)
    def _():
        end = SUBLANES + valid_last
        new_ref[...] = zz[end - (CONV_K - 1):end]


def _conv(proj, buf8, w, l, tc, valid_last):
    bsz, t, _ = proj.shape
    tc = min(tc, t)
    per = tc // SUBLANES
    col = lambda c: pl.BlockSpec((None, tc, MIX_W), lambda b, i: (b, i, c))
    halo = lambda c: pl.BlockSpec((None, SUBLANES, MIX_W), lambda b, i: (b, jnp.maximum(i * per - 1, 0), c))
    kern = functools.partial(_conv_kernel, valid_last=valid_last)
    return pl.pallas_call(
        kern,
        out_shape=(jax.ShapeDtypeStruct((bsz, t, MIX_W), BF16),
                   jax.ShapeDtypeStruct((bsz, CONV_K - 1, MIX_W), F32)),
        grid=(bsz, t // tc),
        in_specs=[col(COL_CB), col(COL_CC), col(COL_CX), halo(COL_CC), halo(COL_CX),
                  pl.BlockSpec((None, SUBLANES, MIX_W), lambda b, i: (b, 0, 0)),
                  pl.BlockSpec((None, CONV_K, MIX_W), lambda b, i: (l, 0, 0))],
        out_specs=(pl.BlockSpec((None, tc, MIX_W), lambda b, i: (b, i, 0)),
                   pl.BlockSpec((None, CONV_K - 1, MIX_W), lambda b, i: (b, 0, 0))),
        compiler_params=_params(("parallel", "arbitrary")),
        name="short_conv",
    )(proj, proj, proj, proj, proj, buf8, w)


def _layer(x, h, mod, next_norm, wts, l, past, sizes, kv_bufs):
    bsz, t, d = x.shape
    sh1, sc1, g1, sh2, sc2, g2 = mod
    tm = sizes["tm"]
    flat = sizes["flat"]
    as_mm = lambda a: a.reshape(flat + a.shape[2:])
    as_seq = lambda a: a.reshape((bsz, t) + a.shape[2:])
    mm_mod = lambda a: a if a.shape[1] == 1 else as_mm(a)
    mm_norm = lambda nrm: (nrm[0], nrm[1], mm_mod(nrm[2]), mm_mod(nrm[3]), nrm[4])

    hm = as_mm(h)
    proj, gates, kv_bufs = _in_proj(hm, wts["w_in"], wts["w_if"], l, kv_bufs, tm)
    proj, gates = as_seq(proj), as_seq(gates)
    gates_t = jnp.swapaxes(gates[:, :, :SUBLANES], 1, 2)
    if sizes["chunk"] > t:
        gates_t = jnp.pad(gates_t, ((0, 0), (0, 0), (0, sizes["chunk"] - t)))

    y_a, mc, mn, mm = _mlstm(proj, gates, gates_t, wts["gate_bias_c"], wts["gate_bias_r"], wts["g_head"], l,
                             past["mlstm_c"], past["mlstm_n"], past["mlstm_m"],
                             sizes["chunk"], sizes["rows"], sizes["valid"])
    if "sb_k" in past:
        y_b = _sb_sample(proj, past["sb_k"], past["sb_v"], past["page_table"], wts["b_sb"], l)
    else:
        y_b = _sb_prompt(proj, wts["b_sb"], l)
    y_s, s_re, s_im = _s5(proj, wts["s5_wb"], wts["s5_tab"], past["s5_re"], past["s5_im"],
                          wts["s5_wc_re"], wts["s5_wc_im"], wts["s5_d"], wts["w_s5_glu"], l,
                          sizes["tc"], sizes["valid_last"])
    y_c, conv_new = _conv(proj, past["conv"], wts["conv_w"], l, sizes["tc"], sizes["valid_last"])

    merged = _gate_merge(hm, wts["w_gate"], wts["b_gate"], [as_mm(y) for y in (y_a, y_b, y_s, y_c)],
                         wts["w_branch"], l, tm, 256)
    tmr = sizes["tm_res"]
    x_mm, h2 = _linear_residual_norm(merged, wts["w_out"], l, as_mm(x), mm_mod(g1),
                                     mm_norm((wts["g_norm2"], l, sc2, sh2, BF16)), tmr, 1, "out_proj")
    hidden = _ffn_up(h2, wts["w_ffn_gate"], wts["w_ffn_up"], l, tm, 512)
    x_new = as_seq(_linear_residual(hidden, wts["w_ffn_down"], l, x_mm, mm_mod(g2), tm, 1024, 2, "ffn_down"))
    h_next = _norm_mod(x_new, *next_norm)

    state = {"mlstm_c": mc, "mlstm_n": mn.reshape(bsz, N_HEADS, HEAD_DIM), "mlstm_m": mm.reshape(bsz, N_HEADS),
             "s5_re": s_re.reshape(bsz, S5_GROUPS, S5_STATE), "s5_im": s_im.reshape(bsz, S5_GROUPS, S5_STATE),
             "conv": conv_new}
    return x_new, h_next, state, kv_bufs


def kernel(x_prompt, x_sample, cache_sb_k, cache_sb_v, state_mlstm_c, state_mlstm_n, state_mlstm_m, state_s5_re, state_s5_im, state_conv, page_table, c_prompt, c_sample, w_ada, b_ada, g_norm1, g_norm2, w_in, b_mlstm_i, b_mlstm_f, g_mlstm_head, b_sb, s5_lambda_re, s5_lambda_im, s5_b_re, s5_b_im, s5_c_re, s5_c_im, s5_d, s5_log_step, w_s5_glu, conv_w, w_gate, b_gate, w_branch, w_out, w_ffn_gate, w_ffn_up, w_ffn_down, g_final):
    depth = w_in.shape[0]
    bp, tp, d = x_prompt.shape
    bs, ts, _ = x_sample.shape
    tpad = SUBLANES
    n_gate = N_GATE_COLS
    split = COL_BQ * MIX_W

    gate_bias = jnp.concatenate([b_mlstm_i, b_mlstm_f], axis=1)
    wts = {
        "g_norm1": g_norm1.reshape(depth, 1, d), "g_norm2": g_norm2.reshape(depth, 1, d),
        "w_in": jnp.concatenate([w_in[:, :, :split], w_in[:, :, split + n_gate:]], axis=2).astype(BF16),
        "w_if": jnp.pad(w_in[:, :, split:split + n_gate], ((0, 0), (0, 0), (0, GATE_LANES - n_gate))).astype(BF16),
        "gate_bias_c": jnp.pad(gate_bias, ((0, 0), (0, GATE_LANES - n_gate))).reshape(depth, 1, GATE_LANES),
        "gate_bias_r": gate_bias.reshape(depth, n_gate, 1),
        "g_head": g_mlstm_head.reshape(depth, 1, MIX_W),
        "b_sb": b_sb,
        "s5_d": s5_d.reshape(depth, 1, MIX_W),
        "w_s5_glu": w_s5_glu.astype(BF16),
        "conv_w": conv_w,
        "w_gate": w_gate, "b_gate": b_gate.reshape(depth, 1, N_BRANCH * d), "w_branch": w_branch,
        "w_out": w_out.astype(BF16),
        "w_ffn_gate": w_ffn_gate, "w_ffn_up": w_ffn_up,
        "w_ffn_down": w_ffn_down.astype(BF16),
    }
    tc_p = 512
    wts["s5_tab"], bb_re, bb_im = _s5_prep(s5_lambda_re, s5_lambda_im, s5_log_step, s5_b_re, s5_b_im)
    to_gcp = lambda a: a.reshape(depth, S5_GROUP_CH, S5_GROUPS, S5_STATE).transpose(0, 2, 1, 3)
    wts["s5_wb"] = jnp.concatenate([_block_diag_groups(to_gcp(bb_re)), _block_diag_groups(to_gcp(bb_im))],
                                   axis=3).astype(BF16)
    wts["s5_wc_re"] = _block_diag_groups(s5_c_re.transpose(0, 1, 3, 2)).astype(BF16)
    wts["s5_wc_im"] = _block_diag_groups(s5_c_im.transpose(0, 1, 3, 2)).astype(BF16)

    c_all = jnp.concatenate([c_prompt, c_sample], axis=0)
    c_all = jnp.pad(c_all, ((0, -c_all.shape[0] % SUBLANES), (0, 0)))
    mod_all = _ada_all(c_all, w_ada, b_ada).reshape(depth, c_all.shape[0], 6, d)

    sizes_p = {"tm": 1024, "tm_res": 512, "flat": (bp, tp), "chunk": 256, "rows": 256, "valid": 256, "tc": tc_p,
               "valid_last": tc_p}
    sizes_s = {"tm": bs * tpad, "tm_res": bs * tpad, "flat": (1, bs * tpad), "chunk": 128, "rows": tpad, "valid": ts,
               "tc": tpad, "valid_last": ts}

    xp = x_prompt
    xs = jnp.pad(x_sample, ((0, 0), (0, tpad - ts), (0, 0)))
    zeros_p = {
        "mlstm_c": jnp.zeros((bp, N_HEADS, HEAD_DIM, HEAD_DIM), F32),
        "mlstm_n": jnp.zeros((bp, N_HEADS, 1, HEAD_DIM), F32),
        "mlstm_m": jnp.zeros((bp, N_HEADS, 1, 1), F32),
        "s5_re": jnp.zeros((bp, 1, S5_WIDTH), F32), "s5_im": jnp.zeros((bp, 1, S5_WIDTH), F32),
        "conv": jnp.zeros((bp, SUBLANES, MIX_W), F32),
    }
    new_p, new_s = [], []
    kv_p = kv_s = None
    mods_p = [[mod_all[l, :bp, i].reshape(bp, 1, d) for i in range(6)] for l in range(depth)]
    mods_s = [[jnp.repeat(mod_all[l, bp:bp + bs, i], tpad, axis=0).reshape(bs, tpad, d) for i in range(6)]
              for l in range(depth)]
    hp = _norm_mod(xp, wts["g_norm1"], 0, mods_p[0][1], mods_p[0][0])
    hs = _norm_mod(xs, wts["g_norm1"], 0, mods_s[0][1], mods_s[0][0])
    g_last = g_final.reshape(1, 1, d)

    def following_norm(mods, l, like):
        if l + 1 < depth:
            return wts["g_norm1"], l + 1, mods[l + 1][1], mods[l + 1][0], BF16
        return g_last, 0, jnp.zeros_like(like), jnp.zeros_like(like), F32

    for l in range(depth):
        mod_p, mod_s = mods_p[l], mods_s[l]
        past_s = {
            "sb_k": cache_sb_k, "sb_v": cache_sb_v, "page_table": page_table,
            "mlstm_c": state_mlstm_c[l], "mlstm_n": state_mlstm_n[l].reshape(bs, N_HEADS, 1, HEAD_DIM),
            "mlstm_m": state_mlstm_m[l].reshape(bs, N_HEADS, 1, 1),
            "s5_re": state_s5_re[l].reshape(bs, 1, S5_WIDTH), "s5_im": state_s5_im[l].reshape(bs, 1, S5_WIDTH),
            "conv": jnp.pad(state_conv[l], ((0, 0), (SUBLANES - (CONV_K - 1), 0), (0, 0))),
        }
        xp, hp, st_p, kv_p = _layer(xp, hp, mod_p, following_norm(mods_p, l, mod_p[0]), wts, l, zeros_p, sizes_p, kv_p)
        xs, hs, st_s, kv_s = _layer(xs, hs, mod_s, following_norm(mods_s, l, mod_s[0]), wts, l, past_s, sizes_s, kv_s)
        new_p.append(st_p)
        new_s.append(st_s)

    y_prompt = hp
    y_sample = hs[:, :ts]
    stk = lambda states, name: jnp.stack([s[name] for s in states])
    heads_p = lambda a: a.reshape(depth, bp, tp, N_HEADS, HEAD_DIM)
    heads_s = lambda a: a.reshape(depth, bs, tpad, N_HEADS, HEAD_DIM)[:, :, :ts]
    return (y_prompt, y_sample,
            heads_p(kv_p[0]), heads_p(kv_p[1]), heads_s(kv_s[0]), heads_s(kv_s[1]),
            stk(new_p, "mlstm_c"), stk(new_p, "mlstm_n"), stk(new_p, "mlstm_m"),
            stk(new_s, "mlstm_c"), stk(new_s, "mlstm_n"), stk(new_s, "mlstm_m"),
            stk(new_p, "s5_re"), stk(new_p, "s5_im"), stk(new_s, "s5_re"), stk(new_s, "s5_im"),
            stk(new_p, "conv"), stk(new_s, "conv"))
```

```python
import functools
import math

import jax
import jax.numpy as jnp
from jax import lax
from jax.experimental import pallas as pl
from jax.experimental.pallas import tpu as pltpu

F32 = jnp.float32
BF16 = jnp.bfloat16

N_HEADS = 4
HEAD_DIM = 128
MIX_W = N_HEADS * HEAD_DIM
N_BRANCH = 4
S5_GROUP_CH = 16
S5_GROUPS = MIX_W // S5_GROUP_CH
S5_STATE = 64
S5_WIDTH = S5_GROUPS * S5_STATE
S5_SUPER = 4
CONV_K = 3
EPS = 1e-6
LOG2_E = 1.0 / math.log(2.0)
SUBLANES = 8
LANES = 128
GATE_LANES = 128
V7X_VMEM_LIMIT = 56 * 1024 * 1024

COL_AQ, COL_AK, COL_AV, COL_AO, COL_BQ, COL_BK, COL_BV, COL_SU, COL_CB, COL_CC, COL_CX = range(11)
N_PROJ_BLOCKS = 11

NT_DIMS = (((1,), (1,)), ((), ()))
TN_DIMS = (((0,), (0,)), ((), ()))


def _dot(a, b):
    return jnp.dot(a, b, preferred_element_type=F32)


def _dot_nt(a, b):
    return lax.dot_general(a, b, NT_DIMS, preferred_element_type=F32)


def _dot_exact(a, b):
    return jnp.dot(a, b, precision=lax.Precision.HIGHEST, preferred_element_type=F32)


def _softplus_neg_abs(z):
    return jnp.log(1.0 + jnp.exp(-jnp.abs(z)))


def _log_sigmoid(z):
    return jnp.minimum(z, 0.0) - _softplus_neg_abs(z)


def _pad_rows(x, rows):
    if x.shape[0] == rows:
        return x
    return jnp.concatenate([x, jnp.zeros((rows - x.shape[0],) + x.shape[1:], x.dtype)], axis=0)


def _params(sem, vmem=None):
    return pltpu.CompilerParams(dimension_semantics=sem, vmem_limit_bytes=vmem)


def _ada_kernel(c_ref, w_ref, b_ref, o_ref):
    c = c_ref[...]
    a = (c * jax.nn.sigmoid(c)).astype(BF16)
    o_ref[...] = _dot(a, w_ref[...].astype(BF16)) + b_ref[...]


def _ada_all(c, w_ada, b_ada, tn=1024):
    depth, d, n = w_ada.shape
    rows = c.shape[0]
    return pl.pallas_call(
        _ada_kernel,
        out_shape=jax.ShapeDtypeStruct((depth, rows, n), F32),
        grid=(depth, n // tn),
        in_specs=[pl.BlockSpec((rows, d), lambda l, j: (0, 0)),
                  pl.BlockSpec((None, d, tn), lambda l, j: (l, 0, j)),
                  pl.BlockSpec((None, 1, tn), lambda l, j: (l, 0, j))],
        out_specs=pl.BlockSpec((None, rows, tn), lambda l, j: (l, 0, j)),
        compiler_params=_params(("parallel", "parallel"), V7X_VMEM_LIMIT),
        name="ada_mod",
    )(c, w_ada, b_ada.reshape(depth, 1, n))


def _norm_mod_kernel(x_ref, g_ref, sc_ref, sh_ref, o_ref):
    x = x_ref[...]
    y = x * lax.rsqrt(jnp.mean(x * x, axis=-1, keepdims=True) + EPS) * g_ref[...]
    o_ref[...] = (y * (1.0 + sc_ref[...]) + sh_ref[...]).astype(o_ref.dtype)


def _row_spec(arr, tm, width, col):
    if arr.shape[1] == 1:
        return pl.BlockSpec((None, 1, width), lambda b, i, *r: (b, 0, col(*r)))
    return pl.BlockSpec((None, tm, width), lambda b, i, *r: (b, i, col(*r)))


def _norm_mod(x, g, l, sc, sh, out_dtype=BF16, tm=512):
    bsz, t, d = x.shape
    tm = min(tm, t)
    zero = lambda *r: 0
    return pl.pallas_call(
        _norm_mod_kernel,
        out_shape=jax.ShapeDtypeStruct((bsz, t, d), out_dtype),
        grid=(bsz, t // tm),
        in_specs=[pl.BlockSpec((None, tm, d), lambda b, i: (b, i, 0)),
                  pl.BlockSpec((None, 1, d), lambda b, i: (l, 0, 0)),
                  _row_spec(sc, tm, d, zero), _row_spec(sh, tm, d, zero)],
        out_specs=pl.BlockSpec((None, tm, d), lambda b, i: (b, i, 0)),
        compiler_params=_params(("parallel", "parallel")),
        name="norm_mod",
    )(x, g, sc, sh)


N_GATE_COLS = 2 * N_HEADS


def _in_proj_kernel(a_ref, w_ref, wif_ref, *refs):
    o_ref, gates_ref, kb_ref, vb_ref = refs[-4:]
    j = pl.program_id(2)
    a = a_ref[...]
    res = _dot(a, w_ref[...])
    o_ref[...] = res

    @pl.when(j == 0)
    def _():
        gates_ref[...] = _dot(a, wif_ref[...])

    @pl.when(j == COL_BK)
    def _():
        kb_ref[...] = res

    @pl.when(j == COL_BV)
    def _():
        vb_ref[...] = res


def _in_proj(a, w, w_if, l, kv_bufs, tm):
    bsz, t, k = a.shape
    depth = w.shape[0]
    n = w.shape[-1]
    tm = min(tm, t)
    buf = jax.ShapeDtypeStruct((depth, bsz, t, MIX_W), F32)
    buf_spec = pl.BlockSpec((None, None, tm, MIX_W), lambda b, i, j: (l, b, i, 0))
    carried = [] if kv_bufs is None else list(kv_bufs)
    proj, gates, kbuf, vbuf = pl.pallas_call(
        _in_proj_kernel,
        out_shape=(jax.ShapeDtypeStruct((bsz, t, n), F32), jax.ShapeDtypeStruct((bsz, t, GATE_LANES), F32), buf, buf),
        grid=(bsz, t // tm, n // MIX_W),
        in_specs=[pl.BlockSpec((None, tm, k), lambda b, i, j: (b, i, 0)),
                  pl.BlockSpec((None, k, MIX_W), lambda b, i, j: (l, 0, j)),
                  pl.BlockSpec((None, k, GATE_LANES), lambda b, i, j: (l, 0, 0))]
                 + [pl.BlockSpec(memory_space=pl.ANY)] * len(carried),
        out_specs=(pl.BlockSpec((None, tm, MIX_W), lambda b, i, j: (b, i, j)),
                   pl.BlockSpec((None, tm, GATE_LANES), lambda b, i, j: (b, i, 0)), buf_spec, buf_spec),
        input_output_aliases={3 + n_: 2 + n_ for n_ in range(len(carried))},
        compiler_params=_params(("parallel", "parallel", "arbitrary"), V7X_VMEM_LIMIT),
        name="in_proj",
    )(a, w, w_if, *carried)
    return proj, gates, (kbuf, vbuf)


def _mm_res_norm_kernel(a_ref, w_ref, x_ref, g_ref, gn_ref, sc_ref, sh_ref, xo_ref, ho_ref, *acc, nk):
    part = _dot(a_ref[...], w_ref[...])

    def finish(total):
        x = x_ref[...] + g_ref[...] * total
        xo_ref[...] = x
        y = x * lax.rsqrt(jnp.mean(x * x, axis=-1, keepdims=True) + EPS) * gn_ref[...]
        ho_ref[...] = (y * (1.0 + sc_ref[...]) + sh_ref[...]).astype(ho_ref.dtype)

    if nk == 1:
        finish(part)
        return
    acc_ref, = acc
    kk = pl.program_id(2)

    @pl.when(kk == 0)
    def _():
        acc_ref[...] = part

    @pl.when((kk > 0) & (kk < nk - 1))
    def _():
        acc_ref[...] += part

    @pl.when(kk == nk - 1)
    def _():
        finish(acc_ref[...] + part)


def _linear_residual_norm(a, w, l, x, g, norm, tm, nk, name):
    bsz, t, k = a.shape
    n = w.shape[-1]
    gn, ln, sc, sh, h_dtype = norm
    tm = min(tm, t)
    tk = k // nk
    zero = lambda kk: 0
    row = pl.BlockSpec((None, tm, n), lambda b, i, kk: (b, i, 0))
    return pl.pallas_call(
        functools.partial(_mm_res_norm_kernel, nk=nk),
        out_shape=(jax.ShapeDtypeStruct((bsz, t, n), F32), jax.ShapeDtypeStruct((bsz, t, n), h_dtype)),
        grid=(bsz, t // tm, nk),
        in_specs=[pl.BlockSpec((None, tm, tk), lambda b, i, kk: (b, i, kk)),
                  pl.BlockSpec((None, tk, n), lambda b, i, kk: (l, kk, 0)),
                  row, _row_spec(g, tm, n, zero),
                  pl.BlockSpec((None, 1, n), lambda b, i, kk: (ln, 0, 0)),
                  _row_spec(sc, tm, n, zero), _row_spec(sh, tm, n, zero)],
        out_specs=(row, row),
        scratch_shapes=[pltpu.VMEM((tm, n), F32)] if nk > 1 else [],
        compiler_params=_params(("parallel", "parallel", "arbitrary"), V7X_VMEM_LIMIT),
        name=name,
    )(a, w, x, g, gn, sc, sh)


def _first_row_tile():
    return (pl.program_id(1) == 0) & (pl.program_id(2) == 0)


def _side_spec(rows, width, col):
    return pl.BlockSpec((None, rows, width), lambda j, b, i: (0, 0, col(j)))


def _ffn_up_kernel(a_ref, as_ref, wg_ref, wu_ref, o_ref, os_ref, wg_bf, wu_bf):
    def act(a):
        gate = _dot(a, wg_bf[...])
        return (gate * jax.nn.sigmoid(gate) * _dot(a, wu_bf[...])).astype(o_ref.dtype)

    @pl.when(_first_row_tile())
    def _():
        wg_bf[...] = wg_ref[...].astype(BF16)
        wu_bf[...] = wu_ref[...].astype(BF16)
        os_ref[...] = act(as_ref[...])

    o_ref[...] = act(a_ref[...])


def _ffn_up(a, a_side, wg, wu, l, tm, tn):
    bsz, t, k = a.shape
    ts = a_side.shape[1]
    n = wg.shape[-1]
    tm, tn = min(tm, t), min(tn, n)
    wspec = pl.BlockSpec((None, k, tn), lambda j, b, i: (l, 0, j))
    return pl.pallas_call(
        _ffn_up_kernel,
        out_shape=(jax.ShapeDtypeStruct((bsz, t, n), BF16), jax.ShapeDtypeStruct((1, ts, n), BF16)),
        grid=(n // tn, bsz, t // tm),
        in_specs=[pl.BlockSpec((None, tm, k), lambda j, b, i: (b, i, 0)), _side_spec(ts, k, lambda j: 0),
                  wspec, wspec],
        out_specs=(pl.BlockSpec((None, tm, tn), lambda j, b, i: (b, i, j)), _side_spec(ts, tn, lambda j: j)),
        scratch_shapes=[pltpu.VMEM((k, tn), BF16), pltpu.VMEM((k, tn), BF16)],
        compiler_params=_params(("arbitrary", "arbitrary", "arbitrary"), V7X_VMEM_LIMIT),
        name="ffn_up",
    )(a, a_side, wg, wu)


def _ffn_down_kernel(a_ref, as_ref, w_ref, x_ref, g_ref, xs_ref, gs_ref, o_ref, os_ref, w_bf):
    @pl.when(_first_row_tile())
    def _():
        w_bf[...] = w_ref[...].astype(BF16)
        os_ref[...] = xs_ref[...] + gs_ref[...] * _dot(as_ref[...], w_bf[...])

    o_ref[...] = x_ref[...] + g_ref[...] * _dot(a_ref[...], w_bf[...])


def _ffn_down(a, a_side, w, l, x, g, x_side, g_side, tm, tn):
    bsz, t, k = a.shape
    ts = a_side.shape[1]
    n = w.shape[-1]
    tm, tn = min(tm, t), min(tn, n)
    tile = pl.BlockSpec((None, tm, tn), lambda j, b, i: (b, i, j))
    side = _side_spec(ts, tn, lambda j: j)
    return pl.pallas_call(
        _ffn_down_kernel,
        out_shape=(jax.ShapeDtypeStruct((bsz, t, n), F32), jax.ShapeDtypeStruct((1, ts, n), F32)),
        grid=(n // tn, bsz, t // tm),
        in_specs=[pl.BlockSpec((None, tm, k), lambda j, b, i: (b, i, 0)), _side_spec(ts, k, lambda j: 0),
                  pl.BlockSpec((None, k, tn), lambda j, b, i: (l, 0, j)),
                  tile, pl.BlockSpec((None, 1, tn), lambda j, b, i: (b, 0, j)), side, side],
        out_specs=(tile, side),
        scratch_shapes=[pltpu.VMEM((k, tn), BF16)],
        compiler_params=_params(("arbitrary", "arbitrary", "arbitrary"), V7X_VMEM_LIMIT),
        name="ffn_down",
    )(a, a_side, w, x, g, x_side, g_side)


def _gate_merge_kernel(*refs):
    nb = N_BRANCH
    h_ref, y_refs, hs_ref, ys_refs = refs[0], refs[1:1 + nb], refs[1 + nb], refs[2 + nb:2 + 2 * nb]
    wg_refs, bg_refs = refs[2 + 2 * nb:2 + 3 * nb], refs[2 + 3 * nb:2 + 4 * nb]
    wb_ref, o_ref, os_ref, wg_bf, wb_bf = refs[2 + 4 * nb:]

    def merge(h, ys):
        acc = None
        for g in range(nb):
            gate = jax.nn.sigmoid(_dot(h, wg_bf[g]) + bg_refs[g][...])
            term = gate * _dot(ys[g][...], wb_bf[g])
            acc = term if acc is None else acc + term
        return acc.astype(o_ref.dtype)

    @pl.when(_first_row_tile())
    def _():
        for g in range(nb):
            wg_bf[g] = wg_refs[g][...].astype(BF16)
        wb_bf[...] = wb_ref[...].astype(BF16)
        os_ref[...] = merge(hs_ref[...], ys_refs)

    o_ref[...] = merge(h_ref[...], y_refs)


def _gate_merge(h, ys, h_side, ys_side, wg, bg, wb, l, tm, tn):
    bsz, t, k = h.shape
    ts = h_side.shape[1]
    d = wb.shape[-1]
    tm, tn = min(tm, t), min(tn, d)
    per = d // tn
    zero = lambda j: 0
    yspec = pl.BlockSpec((None, tm, MIX_W), lambda j, b, i: (b, i, 0))
    wspec = lambda g: pl.BlockSpec((None, k, tn), lambda j, b, i: (l, 0, g * per + j))
    bspec = lambda g: pl.BlockSpec((None, 1, tn), lambda j, b, i: (l, 0, g * per + j))
    branches = range(N_BRANCH)
    return pl.pallas_call(
        _gate_merge_kernel,
        out_shape=(jax.ShapeDtypeStruct((bsz, t, d), BF16), jax.ShapeDtypeStruct((1, ts, d), BF16)),
        grid=(d // tn, bsz, t // tm),
        in_specs=[pl.BlockSpec((None, tm, k), lambda j, b, i: (b, i, 0))] + [yspec] * N_BRANCH
                 + [_side_spec(ts, k, zero)] + [_side_spec(ts, MIX_W, zero)] * N_BRANCH
                 + [wspec(g) for g in branches] + [bspec(g) for g in branches]
                 + [pl.BlockSpec((None, N_BRANCH, MIX_W, tn), lambda j, b, i: (l, 0, 0, j))],
        out_specs=(pl.BlockSpec((None, tm, tn), lambda j, b, i: (b, i, j)), _side_spec(ts, tn, lambda j: j)),
        scratch_shapes=[pltpu.VMEM((N_BRANCH, k, tn), BF16), pltpu.VMEM((N_BRANCH, MIX_W, tn), BF16)],
        compiler_params=_params(("arbitrary", "arbitrary", "arbitrary"), V7X_VMEM_LIMIT),
        name="gate_merge",
    )(h, *ys, h_side, *ys_side, *([wg] * N_BRANCH), *([bg] * N_BRANCH), wb)


def _mlstm_kernel(q_ref, k_ref, v_ref, og_ref, gc_ref, gr_ref, bc_ref, br_ref, gh_ref, c0_ref, n0_ref, m0_ref,
                  y_ref, c_out, n_out, m_out, c_scr, n_scr, m_scr, *, chunk, valid):
    ci = pl.program_id(1)
    rows = q_ref.shape[0]

    @pl.when(ci == 0)
    def _():
        c_scr[...] = c0_ref[...]
        n_scr[...] = n0_ref[...]
        m_scr[...] = m0_ref[...]

    pos_c = lax.broadcasted_iota(jnp.int32, (chunk, 1), 0)
    pos_r = lax.broadcasted_iota(jnp.int32, (1, chunk), 1)
    tri_r = lax.broadcasted_iota(jnp.int32, (chunk, chunk), 0)
    tri_c = lax.broadcasted_iota(jnp.int32, (chunk, chunk), 1)
    causal = tri_c <= tri_r
    lower = causal.astype(F32)
    upper = (tri_r <= tri_c).astype(F32)

    gates_c = _pad_rows(gc_ref[...], chunk) + bc_ref[...]
    gates_r = gr_ref[...] + br_ref[...]
    lf_c = _log_sigmoid(gates_c)
    lf_r = _log_sigmoid(gates_r)
    if valid < chunk:
        lf_c = jnp.where(pos_c < valid, lf_c, 0.0)
        lf_r = jnp.where(pos_r < valid, lf_r, 0.0)
        gates_c = jnp.where(pos_c < valid, gates_c, -jnp.inf)
        gates_r = jnp.where(pos_r < valid, gates_r, -jnp.inf)
    cum_c = _dot_exact(lower, lf_c)
    cum_r = _dot_exact(lf_r, upper)

    heads = range(N_HEADS)
    sls = [slice(h * HEAD_DIM, (h + 1) * HEAD_DIM) for h in heads]
    qs = [_pad_rows(q_ref[:, sl], chunk) for sl in sls]
    ks = [_pad_rows(k_ref[:, sl], chunk) * (HEAD_DIM ** -0.5) for sl in sls]
    vs = [_pad_rows(v_ref[:, sl], chunk) for sl in sls]
    qbs, kbs, vbs = ([x.astype(BF16) for x in xs] for xs in (qs, ks, vs))
    b_cs = [cum_c[:, N_HEADS + h:N_HEADS + h + 1] for h in heads]
    b_rs = [cum_r[N_HEADS + h:N_HEADS + h + 1, :] for h in heads]
    m_prevs = [m_scr[h] for h in heads]
    c_prevs = [c_scr[h] for h in heads]
    n_prevs = [n_scr[h] for h in heads]

    qk = [_dot_nt(qbs[h], kbs[h]) for h in heads]
    cq = [_dot_nt(qbs[h], c_prevs[h].astype(BF16)) for h in heads]

    s_all, w_inters, m_ts = [], [], []
    for h in heads:
        d_intra = jnp.where(causal, b_cs[h] - b_rs[h] + gates_r[h:h + 1, :], -jnp.inf)
        m_inter = b_cs[h] + m_prevs[h]
        m_t = jnp.maximum(m_inter, jnp.max(d_intra, axis=1, keepdims=True))
        s_all.append(jnp.exp(d_intra - m_t) * qk[h])
        w_inters.append(jnp.exp(m_inter - m_t))
        m_ts.append(m_t)
    sv = [_dot(s_all[h].astype(BF16), vbs[h]) for h in heads]

    for h in heads:
        num = w_inters[h] * cq[h] + sv[h]
        den = (w_inters[h] * jnp.sum(qs[h] * n_prevs[h], axis=1, keepdims=True)
               + jnp.sum(s_all[h], axis=1, keepdims=True))
        hh = num / jnp.maximum(jnp.abs(den), jnp.exp(-m_ts[h]))
        hn = hh * lax.rsqrt(jnp.mean(hh * hh, axis=-1, keepdims=True) + EPS) * gh_ref[:, sls[h]]
        y = hn[:rows] * jax.nn.sigmoid(og_ref[:, sls[h]])
        y_ref[:, sls[h]] = y.astype(y_ref.dtype)

    w_ks, decays, m_ends = [], [], []
    for h in heads:
        b_end = b_cs[h][chunk - 1:chunk, :]
        log_w = b_end - b_cs[h] + gates_c[:, h:h + 1]
        m_end = jnp.maximum(b_end + m_prevs[h], jnp.max(log_w, axis=0, keepdims=True))
        w_ks.append(jnp.exp(log_w - m_end))
        decays.append(jnp.exp(b_end + m_prevs[h] - m_end))
        m_ends.append(m_end)
    kv = [lax.dot_general((w_ks[h] * vs[h]).astype(BF16), kbs[h], TN_DIMS, preferred_element_type=F32)
          for h in heads]
    for h in heads:
        c_scr[h] = decays[h] * c_prevs[h] + kv[h]
        n_scr[h] = decays[h] * n_prevs[h] + jnp.sum(w_ks[h] * ks[h], axis=0, keepdims=True)
        m_scr[h] = m_ends[h]

    @pl.when(ci == pl.num_programs(1) - 1)
    def _():
        c_out[...] = c_scr[...]
        n_out[...] = n_scr[...]
        m_out[...] = m_scr[...]


def _mlstm(proj, gates, gates_t, bias_c, bias_r, ghead, l, c0, n0, m0, chunk, rows, valid):
    bsz, t, _ = proj.shape
    nc = t // rows
    col = lambda c: pl.BlockSpec((None, rows, MIX_W), lambda b, i: (b, i, c))
    st4 = lambda s: pl.BlockSpec((None,) + s, lambda b, i: (b, 0, 0, 0))
    kern = functools.partial(_mlstm_kernel, chunk=chunk, valid=valid)
    return pl.pallas_call(
        kern,
        out_shape=(jax.ShapeDtypeStruct((bsz, t, MIX_W), BF16),
                   jax.ShapeDtypeStruct((bsz, N_HEADS, HEAD_DIM, HEAD_DIM), F32),
                   jax.ShapeDtypeStruct((bsz, N_HEADS, 1, HEAD_DIM), F32),
                   jax.ShapeDtypeStruct((bsz, N_HEADS, 1, 1), F32)),
        grid=(bsz, nc),
        in_specs=[col(COL_AQ), col(COL_AK), col(COL_AV), col(COL_AO),
                  pl.BlockSpec((None, rows, GATE_LANES), lambda b, i: (b, i, 0)),
                  pl.BlockSpec((None, SUBLANES, chunk), lambda b, i: (b, 0, i)),
                  pl.BlockSpec((None, 1, GATE_LANES), lambda b, i: (l, 0, 0)),
                  pl.BlockSpec((None, SUBLANES, 1), lambda b, i: (l, 0, 0)),
                  pl.BlockSpec((None, 1, MIX_W), lambda b, i: (l, 0, 0)),
                  st4((N_HEADS, HEAD_DIM, HEAD_DIM)), st4((N_HEADS, 1, HEAD_DIM)), st4((N_HEADS, 1, 1))],
        out_specs=(pl.BlockSpec((None, rows, MIX_W), lambda b, i: (b, i, 0)),
                   st4((N_HEADS, HEAD_DIM, HEAD_DIM)), st4((N_HEADS, 1, HEAD_DIM)), st4((N_HEADS, 1, 1))),
        scratch_shapes=[pltpu.VMEM((N_HEADS, HEAD_DIM, HEAD_DIM), F32),
                        pltpu.VMEM((N_HEADS, 1, HEAD_DIM), F32),
                        pltpu.VMEM((N_HEADS, 1, 1), F32)],
        compiler_params=_params(("parallel", "arbitrary"), V7X_VMEM_LIMIT),
        name="mlstm",
    )(proj, proj, proj, proj, gates, gates_t, bias_c, bias_r, ghead, c0, n0, m0)


def _sb_blocks(zs, carries, suffix, mask, chained):
    rows, n = zs[0].shape
    w = suffix.shape[0]
    nt = n // w
    tiles = [slice(t * w, (t + 1) * w) for t in range(nt)]
    ls_all, stacked, sums = [], [], []
    for z in zs:
        sp = jnp.log(1.0 + jnp.exp2(-jnp.abs(z))) * LOG2_E
        ls = jnp.minimum(z, 0.0) - sp
        ls_all.append(ls)
        l1m = ls - z
        if mask is not None:
            l1m = jnp.where(mask, l1m, 0.0)
        hi = l1m.astype(BF16)
        lo = (l1m - hi.astype(F32)).astype(BF16)
        stacked += [hi[:, s] for s in tiles] + [lo[:, s] for s in tiles]
        sums.append([jnp.sum(l1m[:, s], axis=1, keepdims=True) for s in tiles])
    ex = _dot(jnp.concatenate(stacked, axis=0), suffix)
    weights, out = [], []
    later = carries[0]
    for u in range(len(zs)):
        if not chained:
            later = carries[u]
        base = u * 2 * nt * rows
        parts = [None] * nt
        for t in reversed(range(nt)):
            hi_rows = ex[base + t * rows:base + (t + 1) * rows]
            lo_rows = ex[base + (nt + t) * rows:base + (nt + t + 1) * rows]
            parts[t] = hi_rows + lo_rows + later
            later = later + sums[u][t]
        excl = parts[0] if nt == 1 else jnp.concatenate(parts, axis=1)
        a = jnp.exp2(ls_all[u] + excl)
        if mask is not None:
            a = jnp.where(mask, a, 0.0)
        weights.append(a)
        out.append(later)
    return weights, out


def _strict_suffix_matrix(n):
    r = lax.broadcasted_iota(jnp.int32, (n, n), 0)
    c = lax.broadcasted_iota(jnp.int32, (n, n), 1)
    return (r > c).astype(BF16)


SB_HEADS_PER_STEP = 4
SB_HEADS_PER_MATMUL = 2


def _sb_prompt_kernel(bias_ref, q_ref, k_ref, v_ref, o_ref, *, blk):
    hp = pl.program_id(1)
    i = pl.program_id(2)
    scale = HEAD_DIM ** -0.5 * LOG2_E
    heads = range(SB_HEADS_PER_STEP)
    lanes = [slice(n * HEAD_DIM, (n + 1) * HEAD_DIM) for n in heads]
    bias = [bias_ref[hp * SB_HEADS_PER_STEP + n] * LOG2_E for n in heads]
    qb = [q_ref[:, lanes[n]].astype(BF16) for n in heads]
    suffix = _strict_suffix_matrix(blk)
    r = lax.broadcasted_iota(jnp.int32, (blk, blk), 0)
    cidx = lax.broadcasted_iota(jnp.int32, (blk, blk), 1)

    def step(jb, carry, mask):
        start = pl.multiple_of(jb * blk, blk)
        zs = [_dot_nt(qb[n], k_ref[pl.ds(start, blk), lanes[n]].astype(BF16)) * scale + bias[n] for n in heads]
        ws, cs = [], []
        for g in range(0, SB_HEADS_PER_STEP, SB_HEADS_PER_MATMUL):
            grp = range(g, g + SB_HEADS_PER_MATMUL)
            w_g, c_g = _sb_blocks([zs[n] for n in grp], [carry[n][0] for n in grp], suffix, mask, chained=False)
            ws += w_g
            cs += c_g
        return tuple((cs[n], carry[n][1] + _dot(ws[n].astype(BF16), v_ref[pl.ds(start, blk), lanes[n]].astype(BF16)))
                     for n in heads)

    zero = (jnp.zeros((blk, 1), F32), jnp.zeros((blk, HEAD_DIM), F32))
    carry = step(i, (zero,) * SB_HEADS_PER_STEP, cidx < r)
    carry = lax.fori_loop(0, i, lambda t, ca: step(i - 1 - t, ca, None), carry)
    for n in heads:
        o_ref[:, lanes[n]] = carry[n][1].astype(o_ref.dtype)


def _sb_prompt(proj, b_sb, l, blk=256):
    bsz, t, _ = proj.shape
    blk = min(blk, t)
    width = SB_HEADS_PER_STEP * HEAD_DIM
    groups = N_HEADS // SB_HEADS_PER_STEP
    kern = functools.partial(_sb_prompt_kernel, blk=blk)
    kv = lambda c0: pl.BlockSpec((None, t, width), lambda b, h, i: (b, 0, c0 * groups + h))
    return pl.pallas_call(
        kern,
        out_shape=jax.ShapeDtypeStruct((bsz, t, MIX_W), BF16),
        grid=(bsz, groups, t // blk),
        in_specs=[pl.BlockSpec(memory_space=pltpu.SMEM),
                  pl.BlockSpec((None, blk, width), lambda b, h, i: (b, i, COL_BQ * groups + h)),
                  kv(COL_BK), kv(COL_BV)],
        out_specs=pl.BlockSpec((None, blk, width), lambda b, h, i: (b, i, h)),
        compiler_params=_params(("parallel", "parallel", "arbitrary")),
        name="sb_prompt",
    )(b_sb[l], proj, proj, proj)


def _sb_sample_kernel(pt_ref, bias_ref, q_ref, kn_ref, vn_ref, suf_ref, *refs, pages_per_step):
    npg = pages_per_step
    k_refs, v_refs = refs[:npg], refs[npg:2 * npg]
    o_ref, c_scr, acc_scr = refs[2 * npg:]
    j = pl.program_id(1)
    tp = q_ref.shape[0]
    rows = N_HEADS * tp
    flat = k_refs[0].shape[0]
    scale = HEAD_DIM ** -0.5 * LOG2_E
    by_head = lambda ref: jnp.concatenate(
        [ref[:, h * HEAD_DIM:(h + 1) * HEAD_DIM] for h in range(N_HEADS)], axis=0)
    q_all = by_head(q_ref).astype(BF16)
    bias = jnp.concatenate([jnp.full((tp, 1), bias_ref[h] * LOG2_E, F32) for h in range(N_HEADS)], axis=0)

    def blocks(k_list, v_list, mask, c, acc):
        zs = [_dot_nt(q_all, kb) * scale + bias for kb in k_list]
        ws, cs = _sb_blocks(zs, [c], suf_ref[...], mask, chained=True)
        for a, vb in zip(ws, v_list):
            acc = acc + _dot(a.astype(BF16), vb)
        return cs[-1], acc

    @pl.when(j == 0)
    def _():
        r = lax.broadcasted_iota(jnp.int32, (rows, HEAD_DIM), 0)
        cidx = lax.broadcasted_iota(jnp.int32, (rows, HEAD_DIM), 1)
        mask = (cidx < rows) & (cidx // tp == r // tp) & (cidx % tp < r % tp)
        c, acc = blocks([_pad_rows(by_head(kn_ref), HEAD_DIM).astype(BF16)],
                        [_pad_rows(by_head(vn_ref), HEAD_DIM).astype(BF16)], mask,
                        jnp.zeros(c_scr.shape, F32), jnp.zeros(acc_scr.shape, F32))
        c_scr[...] = c
        acc_scr[...] = acc

    r = lax.broadcasted_iota(jnp.int32, (rows, flat), 0)
    cidx = lax.broadcasted_iota(jnp.int32, (rows, flat), 1)
    same_head = cidx % N_HEADS == r // tp
    c, acc = blocks([k_refs[p][...].astype(BF16) for p in range(npg)],
                    [v_refs[p][...].astype(BF16) for p in range(npg)], same_head, c_scr[...], acc_scr[...])
    c_scr[...] = c
    acc_scr[...] = acc

    @pl.when(j == pl.num_programs(1) - 1)
    def _():
        acc = acc_scr[...]
        for h in range(N_HEADS):
            o_ref[:, h * HEAD_DIM:(h + 1) * HEAD_DIM] = acc[h * tp:(h + 1) * tp].astype(o_ref.dtype)


def _sb_sample(proj, cache_k, cache_v, page_table, b_sb, l, pages_per_step=16):
    bsz, tp, _ = proj.shape
    n_pages = page_table.shape[1]
    depth, n_phys, page = cache_k.shape[:3]
    flat = page * N_HEADS
    npg = pages_per_step
    steps = n_pages // npg
    cache_k = cache_k.reshape(depth, n_phys, flat, HEAD_DIM)
    cache_v = cache_v.reshape(depth, n_phys, flat, HEAD_DIM)
    suffix = jnp.tril(jnp.ones((HEAD_DIM, HEAD_DIM), BF16), -1)

    def page_spec(p):
        def imap(b, j, pt):
            return (l, pt[b * n_pages + (n_pages - 1 - (j * npg + p))], 0, 0)
        return pl.BlockSpec((None, None, flat, HEAD_DIM), imap)

    col = lambda c: pl.BlockSpec((None, tp, MIX_W), lambda b, j, pt: (b, 0, c))
    kern = functools.partial(_sb_sample_kernel, pages_per_step=npg)
    return pl.pallas_call(
        kern,
        out_shape=jax.ShapeDtypeStruct((bsz, tp, MIX_W), BF16),
        grid_spec=pltpu.PrefetchScalarGridSpec(
            num_scalar_prefetch=1,
            grid=(bsz, steps),
            in_specs=[pl.BlockSpec(memory_space=pltpu.SMEM), col(COL_BQ), col(COL_BK), col(COL_BV),
                      pl.BlockSpec((HEAD_DIM, HEAD_DIM), lambda b, j, pt: (0, 0))]
                     + [page_spec(p) for p in range(npg)] * 2,
            out_specs=pl.BlockSpec((None, tp, MIX_W), lambda b, j, pt: (b, 0, 0)),
            scratch_shapes=[pltpu.VMEM((N_HEADS * tp, 1), F32), pltpu.VMEM((N_HEADS * tp, HEAD_DIM), F32)]),
        compiler_params=_params(("parallel", "arbitrary"), V7X_VMEM_LIMIT),
        name="sb_sample",
    )(page_table.reshape(-1), b_sb[l], proj, proj, proj, suffix, *([cache_k] * npg), *([cache_v] * npg))


def _cmul(ar, ai, br, bi):
    return ar * br - ai * bi, ar * bi + ai * br


def _s5_prep_kernel(lre_ref, lim_ref, ls_ref, bre_ref, bim_ref, tab_ref, bbre_ref, bbim_ref):
    lam_re, lam_im = lre_ref[...], lim_ref[...]
    step = jnp.exp(ls_ref[...])
    decay = jnp.exp(lam_re * step)
    a_re = decay * jnp.cos(lam_im * step)
    a_im = decay * jnp.sin(lam_im * step)
    inv = 1.0 / (lam_re * lam_re + lam_im * lam_im)
    f_re = ((a_re - 1.0) * lam_re + a_im * lam_im) * inv
    f_im = (a_im * lam_re - (a_re - 1.0) * lam_im) * inv
    b_re, b_im = bre_ref[...], bim_ref[...]
    bbre_ref[...] = f_re * b_re - f_im * b_im
    bbim_ref[...] = f_re * b_im + f_im * b_re

    pw = {1: (a_re, a_im)}
    pw[2] = _cmul(*pw[1], *pw[1])
    pw[3] = _cmul(*pw[2], *pw[1])
    pw[4] = _cmul(*pw[2], *pw[2])
    pw[5] = _cmul(*pw[4], *pw[1])
    pw[6] = _cmul(*pw[4], *pw[2])
    pw[7] = _cmul(*pw[4], *pw[3])
    pw[8] = _cmul(*pw[4], *pw[4])
    row = lax.broadcasted_iota(jnp.int32, (SUBLANES, lam_re.shape[1]), 0)
    for part in range(2):
        carry = jnp.zeros(row.shape, F32)
        for r in range(SUBLANES):
            carry = jnp.where(row == r, pw[r + 1][part], carry)
        tab_ref[part] = carry
        for idx, k in enumerate((1, 2, 4)):
            tab_ref[2 + 2 * idx + part] = jnp.where(row >= k, pw[k][part], 0.0)


def _s5_prep(lam_re, lam_im, log_step, b_re, b_im):
    depth = lam_re.shape[0]
    flat = lambda a: a.reshape(depth, 1, S5_WIDTH)
    bt = lambda a: a.transpose(0, 3, 1, 2).reshape(depth, S5_GROUP_CH, S5_WIDTH)
    step = jnp.broadcast_to(log_step[:, :, None], (depth, S5_GROUPS, S5_STATE))
    vec = pl.BlockSpec((None, 1, S5_WIDTH), lambda l: (l, 0, 0))
    mat = pl.BlockSpec((None, S5_GROUP_CH, S5_WIDTH), lambda l: (l, 0, 0))
    return pl.pallas_call(
        _s5_prep_kernel,
        out_shape=(jax.ShapeDtypeStruct((depth, 8, SUBLANES, S5_WIDTH), F32),
                   jax.ShapeDtypeStruct((depth, S5_GROUP_CH, S5_WIDTH), F32),
                   jax.ShapeDtypeStruct((depth, S5_GROUP_CH, S5_WIDTH), F32)),
        grid=(depth,),
        in_specs=[vec, vec, vec, mat, mat],
        out_specs=(pl.BlockSpec((None, 8, SUBLANES, S5_WIDTH), lambda l: (l, 0, 0, 0)), mat, mat),
        compiler_params=_params(("parallel",)),
        name="s5_prep",
    )(flat(lam_re), flat(lam_im), flat(step), bt(b_re), bt(b_im))


def _s5_kernel(u_ref, wb_ref, tab_ref, hre_ref, him_ref, wcre_ref, wcim_ref, d_ref, wglu_ref,
               y_ref, sre_out, sim_out, sre, sim, cre, cim, *, valid_last, slab):
    ci = pl.program_id(1)
    tc = u_ref.shape[0]
    sub = MIX_W // S5_SUPER
    wid = S5_WIDTH // S5_SUPER

    @pl.when(ci == 0)
    def _():
        cre[...] = jnp.broadcast_to(hre_ref[...], cre.shape)
        cim[...] = jnp.broadcast_to(him_ref[...], cim.shape)

    u = u_ref[...]
    ub = u.astype(BF16)
    for g in range(S5_SUPER):
        bu = _dot(ub[:, g * sub:(g + 1) * sub], wb_ref[g])
        sre[:, g * wid:(g + 1) * wid] = bu[:, :wid]
        sim[:, g * wid:(g + 1) * wid] = bu[:, wid:]

    for s0 in range(0, S5_WIDTH, slab):
        lanes = slice(s0, s0 + slab)
        pr, pi = tab_ref[0, :, lanes], tab_ref[1, :, lanes]
        levels = [(k, tab_ref[2 + 2 * idx, :, lanes], tab_ref[3 + 2 * idx, :, lanes])
                  for idx, k in enumerate((1, 2, 4))]

        def body(r, carry, lanes=lanes, pr=pr, pi=pi, levels=levels):
            c_re, c_im = carry
            row = pl.multiple_of(r * SUBLANES, SUBLANES)
            xr = sre[pl.ds(row, SUBLANES), lanes]
            xi = sim[pl.ds(row, SUBLANES), lanes]
            for k, mr, mi in levels:
                rr = pltpu.roll(xr, k, axis=0)
                ri = pltpu.roll(xi, k, axis=0)
                xr, xi = xr + mr * rr - mi * ri, xi + mr * ri + mi * rr
            xr, xi = xr + pr * c_re - pi * c_im, xi + pr * c_im + pi * c_re
            sre[pl.ds(row, SUBLANES), lanes] = xr
            sim[pl.ds(row, SUBLANES), lanes] = xi
            last = SUBLANES - 1
            return (jnp.broadcast_to(xr[last:last + 1, :], xr.shape),
                    jnp.broadcast_to(xi[last:last + 1, :], xi.shape))

        nblk = tc // SUBLANES
        c_re, c_im = lax.fori_loop(0, nblk, body, (cre[:, lanes], cim[:, lanes]), unroll=min(2, nblk))
        cre[:, lanes] = c_re
        cim[:, lanes] = c_im

    s_re_b = sre[...].astype(BF16)
    s_im_b = sim[...].astype(BF16)
    y = jnp.concatenate(
        [_dot(s_re_b[:, g * wid:(g + 1) * wid], wcre_ref[g]) - _dot(s_im_b[:, g * wid:(g + 1) * wid], wcim_ref[g])
         for g in range(S5_SUPER)], axis=1)
    y = y + d_ref[...] * u
    y = 0.5 * y * (1.0 + jnp.tanh(math.sqrt(2.0 / math.pi) * (y + 0.044715 * (y * y * y))))
    yy = _dot(y.astype(BF16), wglu_ref[...])
    y_ref[...] = (yy[:, :MIX_W] * jax.nn.sigmoid(yy[:, MIX_W:])).astype(y_ref.dtype)

    @pl.when(ci == pl.num_programs(1) - 1)
    def _():
        sre_out[...] = sre[valid_last - 1:valid_last, :]
        sim_out[...] = sim[valid_last - 1:valid_last, :]


def _s5(proj, wb, tab, h_re, h_im, wc_re, wc_im, d_skip, w_glu, l, tc, valid_last, slab=512):
    bsz, t, _ = proj.shape
    tc = min(tc, t)
    layer = lambda s: pl.BlockSpec((None,) + s, lambda b, i: (l,) + (0,) * len(s))
    st = pl.BlockSpec((None, 1, S5_WIDTH), lambda b, i: (b, 0, 0))
    kern = functools.partial(_s5_kernel, valid_last=valid_last, slab=slab)
    wid = S5_WIDTH // S5_SUPER
    return pl.pallas_call(
        kern,
        out_shape=(jax.ShapeDtypeStruct((bsz, t, MIX_W), BF16),
                   jax.ShapeDtypeStruct((bsz, 1, S5_WIDTH), F32),
                   jax.ShapeDtypeStruct((bsz, 1, S5_WIDTH), F32)),
        grid=(bsz, t // tc),
        in_specs=[pl.BlockSpec((None, tc, MIX_W), lambda b, i: (b, i, COL_SU)),
                  layer((S5_SUPER, MIX_W // S5_SUPER, 2 * wid)),
                  layer((8, SUBLANES, S5_WIDTH)),
                  st, st,
                  layer((S5_SUPER, wid, MIX_W // S5_SUPER)), layer((S5_SUPER, wid, MIX_W // S5_SUPER)),
                  layer((1, MIX_W)), layer((MIX_W, 2 * MIX_W))],
        out_specs=(pl.BlockSpec((None, tc, MIX_W), lambda b, i: (b, i, 0)), st, st),
        scratch_shapes=[pltpu.VMEM((tc, S5_WIDTH), F32), pltpu.VMEM((tc, S5_WIDTH), F32),
                        pltpu.VMEM((SUBLANES, S5_WIDTH), F32), pltpu.VMEM((SUBLANES, S5_WIDTH), F32)],
        compiler_params=_params(("parallel", "arbitrary"), V7X_VMEM_LIMIT),
        name="s5",
    )(proj, wb, tab, h_re, h_im, wc_re, wc_im, d_skip, w_glu)


def _block_diag_groups(w):
    depth, _, a, b = w.shape
    per = S5_GROUPS // S5_SUPER
    w = w.reshape(depth, S5_SUPER, per, a, b)
    eye = jnp.eye(per, dtype=w.dtype)
    bd = w[:, :, :, :, None, :] * eye[None, None, :, None, :, None]
    return bd.reshape(depth, S5_SUPER, per * a, per * b)


def _conv_kernel(gb_ref, gc_ref, xv_ref, hc_ref, hx_ref, buf_ref, w_ref, y_ref, new_ref, *, valid_last):
    ci = pl.program_id(1)
    tc = gb_ref.shape[0]
    z = gc_ref[...] * xv_ref[...]
    prev = jnp.where(ci == 0, buf_ref[...], hc_ref[...] * hx_ref[...])
    zz = jnp.concatenate([prev, z], axis=0)
    w = w_ref[...]
    y = sum(w[j:j + 1, :] * zz[SUBLANES - (CONV_K - 1) + j:SUBLANES - (CONV_K - 1) + j + tc] for j in range(CONV_K))
    y_ref[...] = (gb_ref[...] * y).astype(y_ref.dtype)

    @pl.when(ci == pl.num_programs(1) - 1)
    def _():
        end = SUBLANES + valid_last
        new_ref[...] = zz[end - (CONV_K - 1):end]


def _conv(proj, buf8, w, l, tc, valid_last):
    bsz, t, _ = proj.shape
    tc = min(tc, t)
    per = tc // SUBLANES
    col = lambda c: pl.BlockSpec((None, tc, MIX_W), lambda b, i: (b, i, c))
    halo = lambda c: pl.BlockSpec((None, SUBLANES, MIX_W), lambda b, i: (b, jnp.maximum(i * per - 1, 0), c))
    kern = functools.partial(_conv_kernel, valid_last=valid_last)
    return pl.pallas_call(
        kern,
        out_shape=(jax.ShapeDtypeStruct((bsz, t, MIX_W), BF16),
                   jax.ShapeDtypeStruct((bsz, CONV_K - 1, MIX_W), F32)),
        grid=(bsz, t // tc),
        in_specs=[col(COL_CB), col(COL_CC), col(COL_CX), halo(COL_CC), halo(COL_CX),
                  pl.BlockSpec((None, SUBLANES, MIX_W), lambda b, i: (b, 0, 0)),
                  pl.BlockSpec((None, CONV_K, MIX_W), lambda b, i: (l, 0, 0))],
        out_specs=(pl.BlockSpec((None, tc, MIX_W), lambda b, i: (b, i, 0)),
                   pl.BlockSpec((None, CONV_K - 1, MIX_W), lambda b, i: (b, 0, 0))),
        compiler_params=_params(("parallel", "arbitrary")),
        name="short_conv",
    )(proj, proj, proj, proj, proj, buf8, w)


def _mixers(h, wts, l, past, sizes, kv_bufs):
    bsz, t, d = h.shape
    tm = sizes["tm"]
    flat = sizes["flat"]
    as_mm = lambda a: a.reshape(flat + a.shape[2:])
    as_seq = lambda a: a.reshape((bsz, t) + a.shape[2:])

    hm = as_mm(h)
    proj, gates, kv_bufs = _in_proj(hm, wts["w_in"], wts["w_if"], l, kv_bufs, tm)
    proj, gates = as_seq(proj), as_seq(gates)
    gates_t = jnp.swapaxes(gates[:, :, :SUBLANES], 1, 2)
    if sizes["chunk"] > t:
        gates_t = jnp.pad(gates_t, ((0, 0), (0, 0), (0, sizes["chunk"] - t)))

    y_a, mc, mn, mm = _mlstm(proj, gates, gates_t, wts["gate_bias_c"], wts["gate_bias_r"], wts["g_head"], l,
                             past["mlstm_c"], past["mlstm_n"], past["mlstm_m"],
                             sizes["chunk"], sizes["rows"], sizes["valid"])
    if "sb_k" in past:
        y_b = _sb_sample(proj, past["sb_k"], past["sb_v"], past["page_table"], wts["b_sb"], l)
    else:
        y_b = _sb_prompt(proj, wts["b_sb"], l)
    y_s, s_re, s_im = _s5(proj, wts["s5_wb"], wts["s5_tab"], past["s5_re"], past["s5_im"],
                          wts["s5_wc_re"], wts["s5_wc_im"], wts["s5_d"], wts["w_s5_glu"], l,
                          sizes["tc"], sizes["valid_last"])
    y_c, conv_new = _conv(proj, past["conv"], wts["conv_w"], l, sizes["tc"], sizes["valid_last"])

    state = {"mlstm_c": mc, "mlstm_n": mn.reshape(bsz, N_HEADS, HEAD_DIM), "mlstm_m": mm.reshape(bsz, N_HEADS),
             "s5_re": s_re.reshape(bsz, S5_GROUPS, S5_STATE), "s5_im": s_im.reshape(bsz, S5_GROUPS, S5_STATE),
             "conv": conv_new}
    return hm, [as_mm(y) for y in (y_a, y_b, y_s, y_c)], state, kv_bufs


def _merge_and_ffn(xs, branches, mods, next_norms, wts, l, sizes):
    (hm_p, ys_p), (hm_s, ys_s) = branches
    merged = _gate_merge(hm_p, ys_p, hm_s, ys_s, wts["w_gate"], wts["b_gate"], wts["w_branch"], l,
                         sizes[0]["tm"], 256)
    x_mm, h2 = [], []
    for x, mod, size, mrg in zip(xs, mods, sizes, merged):
        as_mm = lambda a, size=size: a if a.shape[1] == 1 else a.reshape(size["flat"] + a.shape[2:])
        g1, sc2, sh2 = mod[2], mod[4], mod[3]
        xo, ho = _linear_residual_norm(mrg, wts["w_out"], l, x.reshape(size["flat"] + x.shape[2:]), as_mm(g1),
                                       (wts["g_norm2"], l, as_mm(sc2), as_mm(sh2), BF16), size["tm_res"], 1,
                                       "out_proj")
        x_mm.append(xo)
        h2.append(ho)
    hidden = _ffn_up(h2[0], h2[1], wts["w_ffn_gate"], wts["w_ffn_up"], l, sizes[0]["tm"], 512)
    g2_p, g2_s = mods[0][5], mods[1][5]
    x_new = _ffn_down(hidden[0], hidden[1], wts["w_ffn_down"], l, x_mm[0], g2_p, x_mm[1],
                      g2_s.reshape(sizes[1]["flat"] + g2_s.shape[2:]), 512, 512)
    out = []
    for x, xn, nrm in zip(xs, x_new, next_norms):
        xn = xn.reshape(x.shape)
        out.append((xn, _norm_mod(xn, *nrm)))
    return out


def kernel(x_prompt, x_sample, cache_sb_k, cache_sb_v, state_mlstm_c, state_mlstm_n, state_mlstm_m, state_s5_re, state_s5_im, state_conv, page_table, c_prompt, c_sample, w_ada, b_ada, g_norm1, g_norm2, w_in, b_mlstm_i, b_mlstm_f, g_mlstm_head, b_sb, s5_lambda_re, s5_lambda_im, s5_b_re, s5_b_im, s5_c_re, s5_c_im, s5_d, s5_log_step, w_s5_glu, conv_w, w_gate, b_gate, w_branch, w_out, w_ffn_gate, w_ffn_up, w_ffn_down, g_final):
    depth = w_in.shape[0]
    bp, tp, d = x_prompt.shape
    bs, ts, _ = x_sample.shape
    tpad = SUBLANES
    n_gate = N_GATE_COLS
    split = COL_BQ * MIX_W

    gate_bias = jnp.concatenate([b_mlstm_i, b_mlstm_f], axis=1)
    w_in16 = w_in.astype(BF16)
    wts = {
        "g_norm1": g_norm1.reshape(depth, 1, d), "g_norm2": g_norm2.reshape(depth, 1, d),
        "w_in": jnp.concatenate([w_in16[:, :, :split], w_in16[:, :, split + n_gate:]], axis=2),
        "w_if": jnp.pad(w_in16[:, :, split:split + n_gate], ((0, 0), (0, 0), (0, GATE_LANES - n_gate))),
        "gate_bias_c": jnp.pad(gate_bias, ((0, 0), (0, GATE_LANES - n_gate))).reshape(depth, 1, GATE_LANES),
        "gate_bias_r": gate_bias.reshape(depth, n_gate, 1),
        "g_head": g_mlstm_head.reshape(depth, 1, MIX_W),
        "b_sb": b_sb,
        "s5_d": s5_d.reshape(depth, 1, MIX_W),
        "w_s5_glu": w_s5_glu.astype(BF16),
        "conv_w": conv_w,
        "w_gate": w_gate, "b_gate": b_gate.reshape(depth, 1, N_BRANCH * d), "w_branch": w_branch,
        "w_out": w_out.astype(BF16),
        "w_ffn_gate": w_ffn_gate, "w_ffn_up": w_ffn_up,
        "w_ffn_down": w_ffn_down,
    }
    tc_p = 512
    wts["s5_tab"], bb_re, bb_im = _s5_prep(s5_lambda_re, s5_lambda_im, s5_log_step, s5_b_re, s5_b_im)
    to_gcp = lambda a: a.reshape(depth, S5_GROUP_CH, S5_GROUPS, S5_STATE).transpose(0, 2, 1, 3)
    wts["s5_wb"] = jnp.concatenate([_block_diag_groups(to_gcp(bb_re)), _block_diag_groups(to_gcp(bb_im))],
                                   axis=3).astype(BF16)
    wts["s5_wc_re"] = _block_diag_groups(s5_c_re.transpose(0, 1, 3, 2)).astype(BF16)
    wts["s5_wc_im"] = _block_diag_groups(s5_c_im.transpose(0, 1, 3, 2)).astype(BF16)

    c_all = jnp.concatenate([c_prompt, c_sample], axis=0)
    c_all = jnp.pad(c_all, ((0, -c_all.shape[0] % SUBLANES), (0, 0)))
    mod_all = _ada_all(c_all, w_ada, b_ada).reshape(depth, c_all.shape[0], 6, d)

    sizes_p = {"tm": 1024, "tm_res": 512, "flat": (bp, tp), "chunk": 256, "rows": 256, "valid": 256, "tc": tc_p,
               "valid_last": tc_p}
    sizes_s = {"tm": bs * tpad, "tm_res": bs * tpad, "flat": (1, bs * tpad), "chunk": 128, "rows": tpad, "valid": ts,
               "tc": tpad, "valid_last": ts}

    xp = x_prompt
    xs = jnp.pad(x_sample, ((0, 0), (0, tpad - ts), (0, 0)))
    zeros_p = {
        "mlstm_c": jnp.zeros((bp, N_HEADS, HEAD_DIM, HEAD_DIM), F32),
        "mlstm_n": jnp.zeros((bp, N_HEADS, 1, HEAD_DIM), F32),
        "mlstm_m": jnp.zeros((bp, N_HEADS, 1, 1), F32),
        "s5_re": jnp.zeros((bp, 1, S5_WIDTH), F32), "s5_im": jnp.zeros((bp, 1, S5_WIDTH), F32),
        "conv": jnp.zeros((bp, SUBLANES, MIX_W), F32),
    }
    new_p, new_s = [], []
    kv_p = kv_s = None
    mods_p = [[mod_all[l, :bp, i].reshape(bp, 1, d) for i in range(6)] for l in range(depth)]
    mods_s = [[jnp.repeat(mod_all[l, bp:bp + bs, i], tpad, axis=0).reshape(bs, tpad, d) for i in range(6)]
              for l in range(depth)]
    hp = _norm_mod(xp, wts["g_norm1"], 0, mods_p[0][1], mods_p[0][0])
    hs = _norm_mod(xs, wts["g_norm1"], 0, mods_s[0][1], mods_s[0][0])
    g_last = g_final.reshape(1, 1, d)

    def following_norm(mods, l, like):
        if l + 1 < depth:
            return wts["g_norm1"], l + 1, mods[l + 1][1], mods[l + 1][0], BF16
        return g_last, 0, jnp.zeros_like(like), jnp.zeros_like(like), F32

    for l in range(depth):
        mod_p, mod_s = mods_p[l], mods_s[l]
        past_s = {
            "sb_k": cache_sb_k, "sb_v": cache_sb_v, "page_table": page_table,
            "mlstm_c": state_mlstm_c[l], "mlstm_n": state_mlstm_n[l].reshape(bs, N_HEADS, 1, HEAD_DIM),
            "mlstm_m": state_mlstm_m[l].reshape(bs, N_HEADS, 1, 1),
            "s5_re": state_s5_re[l].reshape(bs, 1, S5_WIDTH), "s5_im": state_s5_im[l].reshape(bs, 1, S5_WIDTH),
            "conv": jnp.pad(state_conv[l], ((0, 0), (SUBLANES - (CONV_K - 1), 0), (0, 0))),
        }
        hm_p, ys_p, st_p, kv_p = _mixers(hp, wts, l, zeros_p, sizes_p, kv_p)
        hm_s, ys_s, st_s, kv_s = _mixers(hs, wts, l, past_s, sizes_s, kv_s)
        new_p.append(st_p)
        new_s.append(st_s)
        (xp, hp), (xs, hs) = _merge_and_ffn(
            (xp, xs), ((hm_p, ys_p), (hm_s, ys_s)), (mod_p, mod_s),
            (following_norm(mods_p, l, mod_p[0]), following_norm(mods_s, l, mod_s[0])), wts, l, (sizes_p, sizes_s))

    y_prompt = hp
    y_sample = hs[:, :ts]
    stk = lambda states, name: jnp.stack([s[name] for s in states])
    heads_p = lambda a: a.reshape(depth, bp, tp, N_HEADS, HEAD_DIM)
    heads_s = lambda a: a.reshape(depth, bs, tpad, N_HEADS, HEAD_DIM)[:, :, :ts]
    return (y_prompt, y_sample,
            heads_p(kv_p[0]), heads_p(kv_p[1]), heads_s(kv_s[0]), heads_s(kv_s[1]),
            stk(new_p, "mlstm_c"), stk(new_p, "mlstm_n"), stk(new_p, "mlstm_m"),
            stk(new_s, "mlstm_c"), stk(new_s, "mlstm_n"), stk(new_s, "mlstm_m"),
            stk(new_p, "s5_re"), stk(new_p, "s5_im"), stk(new_s, "s5_re"), stk(new_s, "s5_im"),
            stk(new_p, "conv"), stk(new_s, "conv"))
```

```python
import functools
import math

import jax
import jax.numpy as jnp
from jax import lax
from jax.experimental import pallas as pl
from jax.experimental.pallas import tpu as pltpu

F32 = jnp.float32
BF16 = jnp.bfloat16

N_HEADS = 4
HEAD_DIM = 128
MIX_W = N_HEADS * HEAD_DIM
N_BRANCH = 4
S5_GROUP_CH = 16
S5_GROUPS = MIX_W // S5_GROUP_CH
S5_STATE = 64
S5_WIDTH = S5_GROUPS * S5_STATE
S5_SUPER = 4
CONV_K = 3
EPS = 1e-6
LOG2_E = 1.0 / math.log(2.0)
SUBLANES = 8
LANES = 128
GATE_LANES = 128
V7X_VMEM_LIMIT = 56 * 1024 * 1024

COL_AQ, COL_AK, COL_AV, COL_AO, COL_BQ, COL_BK, COL_BV, COL_SU, COL_CB, COL_CC, COL_CX = range(11)
N_PROJ_BLOCKS = 11

NT_DIMS = (((1,), (1,)), ((), ()))
TN_DIMS = (((0,), (0,)), ((), ()))


def _dot(a, b):
    return jnp.dot(a, b, preferred_element_type=F32)


def _dot_nt(a, b):
    return lax.dot_general(a, b, NT_DIMS, preferred_element_type=F32)


def _dot_exact(a, b):
    return jnp.dot(a, b, precision=lax.Precision.HIGHEST, preferred_element_type=F32)


def _softplus_neg_abs(z):
    return jnp.log(1.0 + jnp.exp(-jnp.abs(z)))


def _log_sigmoid(z):
    return jnp.minimum(z, 0.0) - _softplus_neg_abs(z)


def _pad_rows(x, rows):
    if x.shape[0] == rows:
        return x
    return jnp.concatenate([x, jnp.zeros((rows - x.shape[0],) + x.shape[1:], x.dtype)], axis=0)


def _params(sem, vmem=None):
    return pltpu.CompilerParams(dimension_semantics=sem, vmem_limit_bytes=vmem)


def _ada_kernel(c_ref, w_ref, b_ref, o_ref):
    c = c_ref[...]
    a = (c * jax.nn.sigmoid(c)).astype(BF16)
    o_ref[...] = _dot(a, w_ref[...].astype(BF16)) + b_ref[...]


def _ada_all(c, w_ada, b_ada, tn=1024):
    depth, d, n = w_ada.shape
    rows = c.shape[0]
    return pl.pallas_call(
        _ada_kernel,
        out_shape=jax.ShapeDtypeStruct((depth, rows, n), F32),
        grid=(depth, n // tn),
        in_specs=[pl.BlockSpec((rows, d), lambda l, j: (0, 0)),
                  pl.BlockSpec((None, d, tn), lambda l, j: (l, 0, j)),
                  pl.BlockSpec((None, 1, tn), lambda l, j: (l, 0, j))],
        out_specs=pl.BlockSpec((None, rows, tn), lambda l, j: (l, 0, j)),
        compiler_params=_params(("parallel", "parallel"), V7X_VMEM_LIMIT),
        name="ada_mod",
    )(c, w_ada, b_ada.reshape(depth, 1, n))


def _norm_mod_kernel(x_ref, g_ref, sc_ref, sh_ref, o_ref):
    x = x_ref[...]
    y = x * lax.rsqrt(jnp.mean(x * x, axis=-1, keepdims=True) + EPS) * g_ref[...]
    o_ref[...] = (y * (1.0 + sc_ref[...]) + sh_ref[...]).astype(o_ref.dtype)


def _row_spec(arr, tm, width, col):
    if arr.shape[1] == 1:
        return pl.BlockSpec((None, 1, width), lambda b, i, *r: (b, 0, col(*r)))
    return pl.BlockSpec((None, tm, width), lambda b, i, *r: (b, i, col(*r)))


def _norm_mod(x, g, l, sc, sh, out_dtype=BF16, tm=512):
    bsz, t, d = x.shape
    tm = min(tm, t)
    zero = lambda *r: 0
    return pl.pallas_call(
        _norm_mod_kernel,
        out_shape=jax.ShapeDtypeStruct((bsz, t, d), out_dtype),
        grid=(bsz, t // tm),
        in_specs=[pl.BlockSpec((None, tm, d), lambda b, i: (b, i, 0)),
                  pl.BlockSpec((None, 1, d), lambda b, i: (l, 0, 0)),
                  _row_spec(sc, tm, d, zero), _row_spec(sh, tm, d, zero)],
        out_specs=pl.BlockSpec((None, tm, d), lambda b, i: (b, i, 0)),
        compiler_params=_params(("parallel", "parallel")),
        name="norm_mod",
    )(x, g, sc, sh)


N_GATE_COLS = 2 * N_HEADS


def _in_proj_kernel(a_ref, w_ref, wif_ref, kb_in, vb_in, o_ref, gates_ref, kb_ref, vb_ref):
    del kb_in, vb_in
    j = pl.program_id(2)
    a = a_ref[...]
    res = _dot(a, w_ref[...])
    o_ref[...] = res

    @pl.when(j == 0)
    def _():
        gates_ref[...] = _dot(a, wif_ref[...])

    @pl.when(j == COL_BK)
    def _():
        kb_ref[...] = res

    @pl.when(j == COL_BV)
    def _():
        vb_ref[...] = res


def _in_proj(a, w, w_if, l, kv_bufs, tm):
    bsz, t, k = a.shape
    depth = w.shape[0]
    n = w.shape[-1]
    tm = min(tm, t)
    buf = jax.ShapeDtypeStruct((depth, bsz, t, MIX_W), F32)
    buf_spec = pl.BlockSpec((None, None, tm, MIX_W), lambda b, i, j: (l, b, i, 0))
    carried = list(kv_bufs)
    proj, gates, kbuf, vbuf = pl.pallas_call(
        _in_proj_kernel,
        out_shape=(jax.ShapeDtypeStruct((bsz, t, n), F32), jax.ShapeDtypeStruct((bsz, t, GATE_LANES), F32), buf, buf),
        grid=(bsz, t // tm, n // MIX_W),
        in_specs=[pl.BlockSpec((None, tm, k), lambda b, i, j: (b, i, 0)),
                  pl.BlockSpec((None, k, MIX_W), lambda b, i, j: (l, 0, j)),
                  pl.BlockSpec((None, k, GATE_LANES), lambda b, i, j: (l, 0, 0))]
                 + [pl.BlockSpec(memory_space=pl.ANY)] * len(carried),
        out_specs=(pl.BlockSpec((None, tm, MIX_W), lambda b, i, j: (b, i, j)),
                   pl.BlockSpec((None, tm, GATE_LANES), lambda b, i, j: (b, i, 0)), buf_spec, buf_spec),
        input_output_aliases={3 + n_: 2 + n_ for n_ in range(len(carried))},
        compiler_params=_params(("parallel", "parallel", "arbitrary"), V7X_VMEM_LIMIT),
        name="in_proj",
    )(a, w, w_if, *carried)
    return proj, gates, (kbuf, vbuf)


def _mm_res_norm_kernel(a_ref, w_ref, x_ref, g_ref, gn_ref, sc_ref, sh_ref, xo_ref, ho_ref, *acc, nk):
    part = _dot(a_ref[...], w_ref[...])

    def finish(total):
        x = x_ref[...] + g_ref[...] * total
        xo_ref[...] = x
        y = x * lax.rsqrt(jnp.mean(x * x, axis=-1, keepdims=True) + EPS) * gn_ref[...]
        ho_ref[...] = (y * (1.0 + sc_ref[...]) + sh_ref[...]).astype(ho_ref.dtype)

    if nk == 1:
        finish(part)
        return
    acc_ref, = acc
    kk = pl.program_id(2)

    @pl.when(kk == 0)
    def _():
        acc_ref[...] = part

    @pl.when((kk > 0) & (kk < nk - 1))
    def _():
        acc_ref[...] += part

    @pl.when(kk == nk - 1)
    def _():
        finish(acc_ref[...] + part)


def _linear_residual_norm(a, w, l, x, g, norm, tm, nk, name):
    bsz, t, k = a.shape
    n = w.shape[-1]
    gn, ln, sc, sh, h_dtype = norm
    tm = min(tm, t)
    tk = k // nk
    zero = lambda kk: 0
    row = pl.BlockSpec((None, tm, n), lambda b, i, kk: (b, i, 0))
    return pl.pallas_call(
        functools.partial(_mm_res_norm_kernel, nk=nk),
        out_shape=(jax.ShapeDtypeStruct((bsz, t, n), F32), jax.ShapeDtypeStruct((bsz, t, n), h_dtype)),
        grid=(bsz, t // tm, nk),
        in_specs=[pl.BlockSpec((None, tm, tk), lambda b, i, kk: (b, i, kk)),
                  pl.BlockSpec((None, tk, n), lambda b, i, kk: (l, kk, 0)),
                  row, _row_spec(g, tm, n, zero),
                  pl.BlockSpec((None, 1, n), lambda b, i, kk: (ln, 0, 0)),
                  _row_spec(sc, tm, n, zero), _row_spec(sh, tm, n, zero)],
        out_specs=(row, row),
        scratch_shapes=[pltpu.VMEM((tm, n), F32)] if nk > 1 else [],
        compiler_params=_params(("parallel", "parallel", "arbitrary"), V7X_VMEM_LIMIT),
        name=name,
    )(a, w, x, g, gn, sc, sh)


def _first_row_tile():
    return (pl.program_id(1) == 0) & (pl.program_id(2) == 0)


def _side_spec(rows, width, col):
    return pl.BlockSpec((None, rows, width), lambda j, b, i: (0, 0, col(j)))


def _ffn_up_kernel(a_ref, as_ref, wg_ref, wu_ref, o_ref, os_ref, wg_bf, wu_bf):
    def act(a):
        gate = _dot(a, wg_bf[...])
        return (gate * jax.nn.sigmoid(gate) * _dot(a, wu_bf[...])).astype(o_ref.dtype)

    @pl.when(_first_row_tile())
    def _():
        wg_bf[...] = wg_ref[...].astype(BF16)
        wu_bf[...] = wu_ref[...].astype(BF16)
        os_ref[...] = act(as_ref[...])

    o_ref[...] = act(a_ref[...])


def _ffn_up(a, a_side, wg, wu, l, tm, tn):
    bsz, t, k = a.shape
    ts = a_side.shape[1]
    n = wg.shape[-1]
    tm, tn = min(tm, t), min(tn, n)
    wspec = pl.BlockSpec((None, k, tn), lambda j, b, i: (l, 0, j))
    return pl.pallas_call(
        _ffn_up_kernel,
        out_shape=(jax.ShapeDtypeStruct((bsz, t, n), BF16), jax.ShapeDtypeStruct((1, ts, n), BF16)),
        grid=(n // tn, bsz, t // tm),
        in_specs=[pl.BlockSpec((None, tm, k), lambda j, b, i: (b, i, 0)), _side_spec(ts, k, lambda j: 0),
                  wspec, wspec],
        out_specs=(pl.BlockSpec((None, tm, tn), lambda j, b, i: (b, i, j)), _side_spec(ts, tn, lambda j: j)),
        scratch_shapes=[pltpu.VMEM((k, tn), BF16), pltpu.VMEM((k, tn), BF16)],
        compiler_params=_params(("arbitrary", "arbitrary", "arbitrary"), V7X_VMEM_LIMIT),
        name="ffn_up",
    )(a, a_side, wg, wu)


def _ffn_down_kernel(a_ref, as_ref, w_ref, x_ref, g_ref, xs_ref, gs_ref, o_ref, os_ref, w_bf):
    @pl.when(_first_row_tile())
    def _():
        w_bf[...] = w_ref[...].astype(BF16)
        os_ref[...] = xs_ref[...] + gs_ref[...] * _dot(as_ref[...], w_bf[...])

    o_ref[...] = x_ref[...] + g_ref[...] * _dot(a_ref[...], w_bf[...])


def _ffn_down(a, a_side, w, l, x, g, x_side, g_side, tm, tn):
    bsz, t, k = a.shape
    ts = a_side.shape[1]
    n = w.shape[-1]
    tm, tn = min(tm, t), min(tn, n)
    tile = pl.BlockSpec((None, tm, tn), lambda j, b, i: (b, i, j))
    side = _side_spec(ts, tn, lambda j: j)
    return pl.pallas_call(
        _ffn_down_kernel,
        out_shape=(jax.ShapeDtypeStruct((bsz, t, n), F32), jax.ShapeDtypeStruct((1, ts, n), F32)),
        grid=(n // tn, bsz, t // tm),
        in_specs=[pl.BlockSpec((None, tm, k), lambda j, b, i: (b, i, 0)), _side_spec(ts, k, lambda j: 0),
                  pl.BlockSpec((None, k, tn), lambda j, b, i: (l, 0, j)),
                  tile, pl.BlockSpec((None, 1, tn), lambda j, b, i: (b, 0, j)), side, side],
        out_specs=(tile, side),
        scratch_shapes=[pltpu.VMEM((k, tn), BF16)],
        compiler_params=_params(("arbitrary", "arbitrary", "arbitrary"), V7X_VMEM_LIMIT),
        name="ffn_down",
    )(a, a_side, w, x, g, x_side, g_side)


def _gate_merge_kernel(*refs):
    nb = N_BRANCH
    h_ref, y_refs, hs_ref, ys_refs = refs[0], refs[1:1 + nb], refs[1 + nb], refs[2 + nb:2 + 2 * nb]
    wg_refs, bg_refs = refs[2 + 2 * nb:2 + 3 * nb], refs[2 + 3 * nb:2 + 4 * nb]
    wb_ref, o_ref, os_ref, wg_bf, wb_bf = refs[2 + 4 * nb:]

    def merge(h, ys):
        acc = None
        for g in range(nb):
            gate = jax.nn.sigmoid(_dot(h, wg_bf[g]) + bg_refs[g][...])
            term = gate * _dot(ys[g][...], wb_bf[g])
            acc = term if acc is None else acc + term
        return acc.astype(o_ref.dtype)

    @pl.when(_first_row_tile())
    def _():
        for g in range(nb):
            wg_bf[g] = wg_refs[g][...].astype(BF16)
        wb_bf[...] = wb_ref[...].astype(BF16)
        os_ref[...] = merge(hs_ref[...], ys_refs)

    o_ref[...] = merge(h_ref[...], y_refs)


def _gate_merge(h, ys, h_side, ys_side, wg, bg, wb, l, tm, tn):
    bsz, t, k = h.shape
    ts = h_side.shape[1]
    d = wb.shape[-1]
    tm, tn = min(tm, t), min(tn, d)
    per = d // tn
    zero = lambda j: 0
    yspec = pl.BlockSpec((None, tm, MIX_W), lambda j, b, i: (b, i, 0))
    wspec = lambda g: pl.BlockSpec((None, k, tn), lambda j, b, i: (l, 0, g * per + j))
    bspec = lambda g: pl.BlockSpec((None, 1, tn), lambda j, b, i: (l, 0, g * per + j))
    branches = range(N_BRANCH)
    return pl.pallas_call(
        _gate_merge_kernel,
        out_shape=(jax.ShapeDtypeStruct((bsz, t, d), BF16), jax.ShapeDtypeStruct((1, ts, d), BF16)),
        grid=(d // tn, bsz, t // tm),
        in_specs=[pl.BlockSpec((None, tm, k), lambda j, b, i: (b, i, 0))] + [yspec] * N_BRANCH
                 + [_side_spec(ts, k, zero)] + [_side_spec(ts, MIX_W, zero)] * N_BRANCH
                 + [wspec(g) for g in branches] + [bspec(g) for g in branches]
                 + [pl.BlockSpec((None, N_BRANCH, MIX_W, tn), lambda j, b, i: (l, 0, 0, j))],
        out_specs=(pl.BlockSpec((None, tm, tn), lambda j, b, i: (b, i, j)), _side_spec(ts, tn, lambda j: j)),
        scratch_shapes=[pltpu.VMEM((N_BRANCH, k, tn), BF16), pltpu.VMEM((N_BRANCH, MIX_W, tn), BF16)],
        compiler_params=_params(("arbitrary", "arbitrary", "arbitrary"), V7X_VMEM_LIMIT),
        name="gate_merge",
    )(h, *ys, h_side, *ys_side, *([wg] * N_BRANCH), *([bg] * N_BRANCH), wb)


def _mlstm_kernel(q_ref, k_ref, v_ref, og_ref, gc_ref, gr_ref, bc_ref, br_ref, gh_ref, c0_ref, n0_ref, m0_ref,
                  y_ref, c_out, n_out, m_out, c_scr, n_scr, m_scr, *, chunk, valid):
    ci = pl.program_id(1)
    rows = q_ref.shape[0]

    @pl.when(ci == 0)
    def _():
        c_scr[...] = c0_ref[...]
        n_scr[...] = n0_ref[...]
        m_scr[...] = m0_ref[...]

    pos_c = lax.broadcasted_iota(jnp.int32, (chunk, 1), 0)
    pos_r = lax.broadcasted_iota(jnp.int32, (1, chunk), 1)
    tri_r = lax.broadcasted_iota(jnp.int32, (chunk, chunk), 0)
    tri_c = lax.broadcasted_iota(jnp.int32, (chunk, chunk), 1)
    causal = tri_c <= tri_r
    lower = causal.astype(F32)
    upper = (tri_r <= tri_c).astype(F32)

    gates_c = _pad_rows(gc_ref[...], chunk) + bc_ref[...]
    gates_r = gr_ref[...] + br_ref[...]
    lf_c = _log_sigmoid(gates_c)
    lf_r = _log_sigmoid(gates_r)
    if valid < chunk:
        lf_c = jnp.where(pos_c < valid, lf_c, 0.0)
        lf_r = jnp.where(pos_r < valid, lf_r, 0.0)
        gates_c = jnp.where(pos_c < valid, gates_c, -jnp.inf)
        gates_r = jnp.where(pos_r < valid, gates_r, -jnp.inf)
    cum_c = _dot_exact(lower, lf_c)
    cum_r = _dot_exact(lf_r, upper)

    heads = range(N_HEADS)
    sls = [slice(h * HEAD_DIM, (h + 1) * HEAD_DIM) for h in heads]
    qs = [_pad_rows(q_ref[:, sl], chunk) for sl in sls]
    ks = [_pad_rows(k_ref[:, sl], chunk) * (HEAD_DIM ** -0.5) for sl in sls]
    vs = [_pad_rows(v_ref[:, sl], chunk) for sl in sls]
    qbs, kbs, vbs = ([x.astype(BF16) for x in xs] for xs in (qs, ks, vs))
    b_cs = [cum_c[:, N_HEADS + h:N_HEADS + h + 1] for h in heads]
    b_rs = [cum_r[N_HEADS + h:N_HEADS + h + 1, :] for h in heads]
    m_prevs = [m_scr[h] for h in heads]
    c_prevs = [c_scr[h] for h in heads]
    n_prevs = [n_scr[h] for h in heads]

    qk = [_dot_nt(qbs[h], kbs[h]) for h in heads]
    cq = [_dot_nt(qbs[h], c_prevs[h].astype(BF16)) for h in heads]

    s_all, w_inters, m_ts = [], [], []
    for h in heads:
        d_intra = jnp.where(causal, b_cs[h] - b_rs[h] + gates_r[h:h + 1, :], -jnp.inf)
        m_inter = b_cs[h] + m_prevs[h]
        m_t = jnp.maximum(m_inter, jnp.max(d_intra, axis=1, keepdims=True))
        s_all.append(jnp.exp(d_intra - m_t) * qk[h])
        w_inters.append(jnp.exp(m_inter - m_t))
        m_ts.append(m_t)
    sv = [_dot(s_all[h].astype(BF16), vbs[h]) for h in heads]

    for h in heads:
        num = w_inters[h] * cq[h] + sv[h]
        den = (w_inters[h] * jnp.sum(qs[h] * n_prevs[h], axis=1, keepdims=True)
               + jnp.sum(s_all[h], axis=1, keepdims=True))
        hh = num / jnp.maximum(jnp.abs(den), jnp.exp(-m_ts[h]))
        hn = hh * lax.rsqrt(jnp.mean(hh * hh, axis=-1, keepdims=True) + EPS) * gh_ref[:, sls[h]]
        y = hn[:rows] * jax.nn.sigmoid(og_ref[:, sls[h]])
        y_ref[:, sls[h]] = y.astype(y_ref.dtype)

    w_ks, decays, m_ends = [], [], []
    for h in heads:
        b_end = b_cs[h][chunk - 1:chunk, :]
        log_w = b_end - b_cs[h] + gates_c[:, h:h + 1]
        m_end = jnp.maximum(b_end + m_prevs[h], jnp.max(log_w, axis=0, keepdims=True))
        w_ks.append(jnp.exp(log_w - m_end))
        decays.append(jnp.exp(b_end + m_prevs[h] - m_end))
        m_ends.append(m_end)
    kv = [lax.dot_general((w_ks[h] * vs[h]).astype(BF16), kbs[h], TN_DIMS, preferred_element_type=F32)
          for h in heads]
    for h in heads:
        c_scr[h] = decays[h] * c_prevs[h] + kv[h]
        n_scr[h] = decays[h] * n_prevs[h] + jnp.sum(w_ks[h] * ks[h], axis=0, keepdims=True)
        m_scr[h] = m_ends[h]

    @pl.when(ci == pl.num_programs(1) - 1)
    def _():
        c_out[...] = c_scr[...]
        n_out[...] = n_scr[...]
        m_out[...] = m_scr[...]


def _mlstm(proj, gates, gates_t, bias_c, bias_r, ghead, l, c0, n0, m0, chunk, rows, valid):
    bsz, t, _ = proj.shape
    nc = t // rows
    col = lambda c: pl.BlockSpec((None, rows, MIX_W), lambda b, i: (b, i, c))
    st4 = lambda s: pl.BlockSpec((None,) + s, lambda b, i: (b, 0, 0, 0))
    kern = functools.partial(_mlstm_kernel, chunk=chunk, valid=valid)
    return pl.pallas_call(
        kern,
        out_shape=(jax.ShapeDtypeStruct((bsz, t, MIX_W), BF16),
                   jax.ShapeDtypeStruct((bsz, N_HEADS, HEAD_DIM, HEAD_DIM), F32),
                   jax.ShapeDtypeStruct((bsz, N_HEADS, 1, HEAD_DIM), F32),
                   jax.ShapeDtypeStruct((bsz, N_HEADS, 1, 1), F32)),
        grid=(bsz, nc),
        in_specs=[col(COL_AQ), col(COL_AK), col(COL_AV), col(COL_AO),
                  pl.BlockSpec((None, rows, GATE_LANES), lambda b, i: (b, i, 0)),
                  pl.BlockSpec((None, SUBLANES, chunk), lambda b, i: (b, 0, i)),
                  pl.BlockSpec((None, 1, GATE_LANES), lambda b, i: (l, 0, 0)),
                  pl.BlockSpec((None, SUBLANES, 1), lambda b, i: (l, 0, 0)),
                  pl.BlockSpec((None, 1, MIX_W), lambda b, i: (l, 0, 0)),
                  st4((N_HEADS, HEAD_DIM, HEAD_DIM)), st4((N_HEADS, 1, HEAD_DIM)), st4((N_HEADS, 1, 1))],
        out_specs=(pl.BlockSpec((None, rows, MIX_W), lambda b, i: (b, i, 0)),
                   st4((N_HEADS, HEAD_DIM, HEAD_DIM)), st4((N_HEADS, 1, HEAD_DIM)), st4((N_HEADS, 1, 1))),
        scratch_shapes=[pltpu.VMEM((N_HEADS, HEAD_DIM, HEAD_DIM), F32),
                        pltpu.VMEM((N_HEADS, 1, HEAD_DIM), F32),
                        pltpu.VMEM((N_HEADS, 1, 1), F32)],
        compiler_params=_params(("parallel", "arbitrary"), V7X_VMEM_LIMIT),
        name="mlstm",
    )(proj, proj, proj, proj, gates, gates_t, bias_c, bias_r, ghead, c0, n0, m0)


def _sb_blocks(zs, carries, suffix, mask, chained):
    rows, n = zs[0].shape
    w = suffix.shape[0]
    nt = n // w
    tiles = [slice(t * w, (t + 1) * w) for t in range(nt)]
    ls_all, stacked, sums = [], [], []
    for z in zs:
        sp = jnp.log(1.0 + jnp.exp2(-jnp.abs(z))) * LOG2_E
        ls = jnp.minimum(z, 0.0) - sp
        ls_all.append(ls)
        l1m = ls - z
        if mask is not None:
            l1m = jnp.where(mask, l1m, 0.0)
        hi = l1m.astype(BF16)
        lo = (l1m - hi.astype(F32)).astype(BF16)
        stacked += [hi[:, s] for s in tiles] + [lo[:, s] for s in tiles]
        sums.append([jnp.sum(l1m[:, s], axis=1, keepdims=True) for s in tiles])
    ex = _dot(jnp.concatenate(stacked, axis=0), suffix)
    weights, out = [], []
    later = carries[0]
    for u in range(len(zs)):
        if not chained:
            later = carries[u]
        base = u * 2 * nt * rows
        parts = [None] * nt
        for t in reversed(range(nt)):
            hi_rows = ex[base + t * rows:base + (t + 1) * rows]
            lo_rows = ex[base + (nt + t) * rows:base + (nt + t + 1) * rows]
            parts[t] = hi_rows + lo_rows + later
            later = later + sums[u][t]
        excl = parts[0] if nt == 1 else jnp.concatenate(parts, axis=1)
        a = jnp.exp2(ls_all[u] + excl)
        if mask is not None:
            a = jnp.where(mask, a, 0.0)
        weights.append(a)
        out.append(later)
    return weights, out


def _strict_suffix_matrix(n):
    r = lax.broadcasted_iota(jnp.int32, (n, n), 0)
    c = lax.broadcasted_iota(jnp.int32, (n, n), 1)
    return (r > c).astype(BF16)


SB_HEADS_PER_STEP = 4
SB_HEADS_PER_MATMUL = 2


def _sb_prompt_kernel(bias_ref, q_ref, k_ref, v_ref, o_ref, *, blk):
    hp = pl.program_id(1)
    i = pl.program_id(2)
    scale = HEAD_DIM ** -0.5 * LOG2_E
    heads = range(SB_HEADS_PER_STEP)
    lanes = [slice(n * HEAD_DIM, (n + 1) * HEAD_DIM) for n in heads]
    bias = [bias_ref[hp * SB_HEADS_PER_STEP + n] * LOG2_E for n in heads]
    qb = [q_ref[:, lanes[n]].astype(BF16) for n in heads]
    suffix = _strict_suffix_matrix(blk)
    r = lax.broadcasted_iota(jnp.int32, (blk, blk), 0)
    cidx = lax.broadcasted_iota(jnp.int32, (blk, blk), 1)

    def step(jb, carry, mask):
        start = pl.multiple_of(jb * blk, blk)
        zs = [_dot_nt(qb[n], k_ref[pl.ds(start, blk), lanes[n]].astype(BF16)) * scale + bias[n] for n in heads]
        ws, cs = [], []
        for g in range(0, SB_HEADS_PER_STEP, SB_HEADS_PER_MATMUL):
            grp = range(g, g + SB_HEADS_PER_MATMUL)
            w_g, c_g = _sb_blocks([zs[n] for n in grp], [carry[n][0] for n in grp], suffix, mask, chained=False)
            ws += w_g
            cs += c_g
        return tuple((cs[n], carry[n][1] + _dot(ws[n].astype(BF16), v_ref[pl.ds(start, blk), lanes[n]].astype(BF16)))
                     for n in heads)

    zero = (jnp.zeros((blk, 1), F32), jnp.zeros((blk, HEAD_DIM), F32))
    carry = step(i, (zero,) * SB_HEADS_PER_STEP, cidx < r)
    carry = lax.fori_loop(0, i, lambda t, ca: step(i - 1 - t, ca, None), carry)
    for n in heads:
        o_ref[:, lanes[n]] = carry[n][1].astype(o_ref.dtype)


def _sb_prompt(proj, b_sb, l, blk=256):
    bsz, t, _ = proj.shape
    blk = min(blk, t)
    width = SB_HEADS_PER_STEP * HEAD_DIM
    groups = N_HEADS // SB_HEADS_PER_STEP
    kern = functools.partial(_sb_prompt_kernel, blk=blk)
    kv = lambda c0: pl.BlockSpec((None, t, width), lambda b, h, i: (b, 0, c0 * groups + h))
    return pl.pallas_call(
        kern,
        out_shape=jax.ShapeDtypeStruct((bsz, t, MIX_W), BF16),
        grid=(bsz, groups, t // blk),
        in_specs=[pl.BlockSpec(memory_space=pltpu.SMEM),
                  pl.BlockSpec((None, blk, width), lambda b, h, i: (b, i, COL_BQ * groups + h)),
                  kv(COL_BK), kv(COL_BV)],
        out_specs=pl.BlockSpec((None, blk, width), lambda b, h, i: (b, i, h)),
        compiler_params=_params(("parallel", "parallel", "arbitrary")),
        name="sb_prompt",
    )(b_sb[l], proj, proj, proj)


def _sb_sample_kernel(pt_ref, bias_ref, q_ref, kn_ref, vn_ref, suf_ref, *refs, pages_per_step):
    npg = pages_per_step
    k_refs, v_refs = refs[:npg], refs[npg:2 * npg]
    o_ref, c_scr, acc_scr = refs[2 * npg:]
    j = pl.program_id(1)
    tp = q_ref.shape[0]
    rows = N_HEADS * tp
    flat = k_refs[0].shape[0]
    scale = HEAD_DIM ** -0.5 * LOG2_E
    by_head = lambda ref: jnp.concatenate(
        [ref[:, h * HEAD_DIM:(h + 1) * HEAD_DIM] for h in range(N_HEADS)], axis=0)
    q_all = by_head(q_ref).astype(BF16)
    bias = jnp.concatenate([jnp.full((tp, 1), bias_ref[h] * LOG2_E, F32) for h in range(N_HEADS)], axis=0)

    def blocks(k_list, v_list, mask, c, acc):
        zs = [_dot_nt(q_all, kb) * scale + bias for kb in k_list]
        ws, cs = _sb_blocks(zs, [c], suf_ref[...], mask, chained=True)
        for a, vb in zip(ws, v_list):
            acc = acc + _dot(a.astype(BF16), vb)
        return cs[-1], acc

    @pl.when(j == 0)
    def _():
        r = lax.broadcasted_iota(jnp.int32, (rows, HEAD_DIM), 0)
        cidx = lax.broadcasted_iota(jnp.int32, (rows, HEAD_DIM), 1)
        mask = (cidx < rows) & (cidx // tp == r // tp) & (cidx % tp < r % tp)
        c, acc = blocks([_pad_rows(by_head(kn_ref), HEAD_DIM).astype(BF16)],
                        [_pad_rows(by_head(vn_ref), HEAD_DIM).astype(BF16)], mask,
                        jnp.zeros(c_scr.shape, F32), jnp.zeros(acc_scr.shape, F32))
        c_scr[...] = c
        acc_scr[...] = acc

    r = lax.broadcasted_iota(jnp.int32, (rows, flat), 0)
    cidx = lax.broadcasted_iota(jnp.int32, (rows, flat), 1)
    same_head = cidx % N_HEADS == r // tp
    c, acc = blocks([k_refs[p][...].astype(BF16) for p in range(npg)],
                    [v_refs[p][...].astype(BF16) for p in range(npg)], same_head, c_scr[...], acc_scr[...])
    c_scr[...] = c
    acc_scr[...] = acc

    @pl.when(j == pl.num_programs(1) - 1)
    def _():
        acc = acc_scr[...]
        for h in range(N_HEADS):
            o_ref[:, h * HEAD_DIM:(h + 1) * HEAD_DIM] = acc[h * tp:(h + 1) * tp].astype(o_ref.dtype)


def _sb_sample(proj, cache_k, cache_v, page_table, b_sb, l, pages_per_step=32):
    bsz, tp, _ = proj.shape
    n_pages = page_table.shape[1]
    depth, n_phys, page = cache_k.shape[:3]
    flat = page * N_HEADS
    npg = pages_per_step
    steps = n_pages // npg
    cache_k = cache_k.reshape(depth, n_phys, flat, HEAD_DIM)
    cache_v = cache_v.reshape(depth, n_phys, flat, HEAD_DIM)
    suffix = jnp.tril(jnp.ones((HEAD_DIM, HEAD_DIM), BF16), -1)

    def page_spec(p):
        def imap(b, j, pt):
            return (l, pt[b * n_pages + (n_pages - 1 - (j * npg + p))], 0, 0)
        return pl.BlockSpec((None, None, flat, HEAD_DIM), imap)

    col = lambda c: pl.BlockSpec((None, tp, MIX_W), lambda b, j, pt: (b, 0, c))
    kern = functools.partial(_sb_sample_kernel, pages_per_step=npg)
    return pl.pallas_call(
        kern,
        out_shape=jax.ShapeDtypeStruct((bsz, tp, MIX_W), BF16),
        grid_spec=pltpu.PrefetchScalarGridSpec(
            num_scalar_prefetch=1,
            grid=(bsz, steps),
            in_specs=[pl.BlockSpec(memory_space=pltpu.SMEM), col(COL_BQ), col(COL_BK), col(COL_BV),
                      pl.BlockSpec((HEAD_DIM, HEAD_DIM), lambda b, j, pt: (0, 0))]
                     + [page_spec(p) for p in range(npg)] * 2,
            out_specs=pl.BlockSpec((None, tp, MIX_W), lambda b, j, pt: (b, 0, 0)),
            scratch_shapes=[pltpu.VMEM((N_HEADS * tp, 1), F32), pltpu.VMEM((N_HEADS * tp, HEAD_DIM), F32)]),
        compiler_params=_params(("parallel", "arbitrary"), V7X_VMEM_LIMIT),
        name="sb_sample",
    )(page_table.reshape(-1), b_sb[l], proj, proj, proj, suffix, *([cache_k] * npg), *([cache_v] * npg))


def _cmul(ar, ai, br, bi):
    return ar * br - ai * bi, ar * bi + ai * br


def _s5_prep_kernel(lre_ref, lim_ref, ls_ref, bre_ref, bim_ref, tab_ref, bbre_ref, bbim_ref):
    lam_re, lam_im = lre_ref[...], lim_ref[...]
    step = jnp.exp(ls_ref[...])
    decay = jnp.exp(lam_re * step)
    a_re = decay * jnp.cos(lam_im * step)
    a_im = decay * jnp.sin(lam_im * step)
    inv = 1.0 / (lam_re * lam_re + lam_im * lam_im)
    f_re = ((a_re - 1.0) * lam_re + a_im * lam_im) * inv
    f_im = (a_im * lam_re - (a_re - 1.0) * lam_im) * inv
    b_re, b_im = bre_ref[...], bim_ref[...]
    bbre_ref[...] = f_re * b_re - f_im * b_im
    bbim_ref[...] = f_re * b_im + f_im * b_re

    pw = {1: (a_re, a_im)}
    pw[2] = _cmul(*pw[1], *pw[1])
    pw[3] = _cmul(*pw[2], *pw[1])
    pw[4] = _cmul(*pw[2], *pw[2])
    pw[5] = _cmul(*pw[4], *pw[1])
    pw[6] = _cmul(*pw[4], *pw[2])
    pw[7] = _cmul(*pw[4], *pw[3])
    pw[8] = _cmul(*pw[4], *pw[4])
    row = lax.broadcasted_iota(jnp.int32, (SUBLANES, lam_re.shape[1]), 0)
    for part in range(2):
        carry = jnp.zeros(row.shape, F32)
        for r in range(SUBLANES):
            carry = jnp.where(row == r, pw[r + 1][part], carry)
        tab_ref[part] = carry
        for idx, k in enumerate((1, 2, 4)):
            tab_ref[2 + 2 * idx + part] = jnp.where(row >= k, pw[k][part], 0.0)


def _s5_prep(lam_re, lam_im, log_step, b_re, b_im):
    depth = lam_re.shape[0]
    flat = lambda a: a.reshape(depth, 1, S5_WIDTH)
    bt = lambda a: a.transpose(0, 3, 1, 2).reshape(depth, S5_GROUP_CH, S5_WIDTH)
    step = jnp.broadcast_to(log_step[:, :, None], (depth, S5_GROUPS, S5_STATE))
    vec = pl.BlockSpec((None, 1, S5_WIDTH), lambda l: (l, 0, 0))
    mat = pl.BlockSpec((None, S5_GROUP_CH, S5_WIDTH), lambda l: (l, 0, 0))
    return pl.pallas_call(
        _s5_prep_kernel,
        out_shape=(jax.ShapeDtypeStruct((depth, 8, SUBLANES, S5_WIDTH), F32),
                   jax.ShapeDtypeStruct((depth, S5_GROUP_CH, S5_WIDTH), F32),
                   jax.ShapeDtypeStruct((depth, S5_GROUP_CH, S5_WIDTH), F32)),
        grid=(depth,),
        in_specs=[vec, vec, vec, mat, mat],
        out_specs=(pl.BlockSpec((None, 8, SUBLANES, S5_WIDTH), lambda l: (l, 0, 0, 0)), mat, mat),
        compiler_params=_params(("parallel",)),
        name="s5_prep",
    )(flat(lam_re), flat(lam_im), flat(step), bt(b_re), bt(b_im))


def _s5_kernel(u_ref, wb_ref, tab_ref, hre_ref, him_ref, wcre_ref, wcim_ref, d_ref, wglu_ref,
               y_ref, sre_out, sim_out, sre, sim, cre, cim, *, valid_last, slab):
    ci = pl.program_id(1)
    tc = u_ref.shape[0]
    sub = MIX_W // S5_SUPER
    wid = S5_WIDTH // S5_SUPER

    @pl.when(ci == 0)
    def _():
        cre[...] = jnp.broadcast_to(hre_ref[...], cre.shape)
        cim[...] = jnp.broadcast_to(him_ref[...], cim.shape)

    u = u_ref[...]
    ub = u.astype(BF16)
    for g in range(S5_SUPER):
        bu = _dot(ub[:, g * sub:(g + 1) * sub], wb_ref[g])
        sre[:, g * wid:(g + 1) * wid] = bu[:, :wid]
        sim[:, g * wid:(g + 1) * wid] = bu[:, wid:]

    for s0 in range(0, S5_WIDTH, slab):
        lanes = slice(s0, s0 + slab)
        pr, pi = tab_ref[0, :, lanes], tab_ref[1, :, lanes]
        levels = [(k, tab_ref[2 + 2 * idx, :, lanes], tab_ref[3 + 2 * idx, :, lanes])
                  for idx, k in enumerate((1, 2, 4))]

        def body(r, carry, lanes=lanes, pr=pr, pi=pi, levels=levels):
            c_re, c_im = carry
            row = pl.multiple_of(r * SUBLANES, SUBLANES)
            xr = sre[pl.ds(row, SUBLANES), lanes]
            xi = sim[pl.ds(row, SUBLANES), lanes]
            for k, mr, mi in levels:
                rr = pltpu.roll(xr, k, axis=0)
                ri = pltpu.roll(xi, k, axis=0)
                xr, xi = xr + mr * rr - mi * ri, xi + mr * ri + mi * rr
            xr, xi = xr + pr * c_re - pi * c_im, xi + pr * c_im + pi * c_re
            sre[pl.ds(row, SUBLANES), lanes] = xr
            sim[pl.ds(row, SUBLANES), lanes] = xi
            last = SUBLANES - 1
            return (jnp.broadcast_to(xr[last:last + 1, :], xr.shape),
                    jnp.broadcast_to(xi[last:last + 1, :], xi.shape))

        nblk = tc // SUBLANES
        c_re, c_im = lax.fori_loop(0, nblk, body, (cre[:, lanes], cim[:, lanes]), unroll=min(2, nblk))
        cre[:, lanes] = c_re
        cim[:, lanes] = c_im

    s_re_b = sre[...].astype(BF16)
    s_im_b = sim[...].astype(BF16)
    y = jnp.concatenate(
        [_dot(s_re_b[:, g * wid:(g + 1) * wid], wcre_ref[g]) - _dot(s_im_b[:, g * wid:(g + 1) * wid], wcim_ref[g])
         for g in range(S5_SUPER)], axis=1)
    y = y + d_ref[...] * u
    y = 0.5 * y * (1.0 + jnp.tanh(math.sqrt(2.0 / math.pi) * (y + 0.044715 * (y * y * y))))
    yy = _dot(y.astype(BF16), wglu_ref[...])
    y_ref[...] = (yy[:, :MIX_W] * jax.nn.sigmoid(yy[:, MIX_W:])).astype(y_ref.dtype)

    @pl.when(ci == pl.num_programs(1) - 1)
    def _():
        sre_out[...] = sre[valid_last - 1:valid_last, :]
        sim_out[...] = sim[valid_last - 1:valid_last, :]


def _s5(proj, wb, tab, h_re, h_im, wc_re, wc_im, d_skip, w_glu, l, tc, valid_last, slab=512):
    bsz, t, _ = proj.shape
    tc = min(tc, t)
    layer = lambda s: pl.BlockSpec((None,) + s, lambda b, i: (l,) + (0,) * len(s))
    st = pl.BlockSpec((None, 1, S5_WIDTH), lambda b, i: (b, 0, 0))
    kern = functools.partial(_s5_kernel, valid_last=valid_last, slab=slab)
    wid = S5_WIDTH // S5_SUPER
    return pl.pallas_call(
        kern,
        out_shape=(jax.ShapeDtypeStruct((bsz, t, MIX_W), BF16),
                   jax.ShapeDtypeStruct((bsz, 1, S5_WIDTH), F32),
                   jax.ShapeDtypeStruct((bsz, 1, S5_WIDTH), F32)),
        grid=(bsz, t // tc),
        in_specs=[pl.BlockSpec((None, tc, MIX_W), lambda b, i: (b, i, COL_SU)),
                  layer((S5_SUPER, MIX_W // S5_SUPER, 2 * wid)),
                  layer((8, SUBLANES, S5_WIDTH)),
                  st, st,
                  layer((S5_SUPER, wid, MIX_W // S5_SUPER)), layer((S5_SUPER, wid, MIX_W // S5_SUPER)),
                  layer((1, MIX_W)), layer((MIX_W, 2 * MIX_W))],
        out_specs=(pl.BlockSpec((None, tc, MIX_W), lambda b, i: (b, i, 0)), st, st),
        scratch_shapes=[pltpu.VMEM((tc, S5_WIDTH), F32), pltpu.VMEM((tc, S5_WIDTH), F32),
                        pltpu.VMEM((SUBLANES, S5_WIDTH), F32), pltpu.VMEM((SUBLANES, S5_WIDTH), F32)],
        compiler_params=_params(("parallel", "arbitrary"), V7X_VMEM_LIMIT),
        name="s5",
    )(proj, wb, tab, h_re, h_im, wc_re, wc_im, d_skip, w_glu)


def _block_diag_groups(w):
    depth, _, a, b = w.shape
    per = S5_GROUPS // S5_SUPER
    w = w.reshape(depth, S5_SUPER, per, a, b)
    eye = jnp.eye(per, dtype=w.dtype)
    bd = w[:, :, :, :, None, :] * eye[None, None, :, None, :, None]
    return bd.reshape(depth, S5_SUPER, per * a, per * b)


def _conv_kernel(gb_ref, gc_ref, xv_ref, hc_ref, hx_ref, buf_ref, w_ref, y_ref, new_ref, *, valid_last):
    ci = pl.program_id(1)
    tc = gb_ref.shape[0]
    z = gc_ref[...] * xv_ref[...]
    prev = jnp.where(ci == 0, buf_ref[...], hc_ref[...] * hx_ref[...])
    zz = jnp.concatenate([prev, z], axis=0)
    w = w_ref[...]
    y = sum(w[j:j + 1, :] * zz[SUBLANES - (CONV_K - 1) + j:SUBLANES - (CONV_K - 1) + j + tc] for j in range(CONV_K))
    y_ref[...] = (gb_ref[...] * y).astype(y_ref.dtype)

    @pl.when(ci == pl.num_programs(1) - 1)
    def _():
        end = SUBLANES + valid_last
        new_ref[...] = zz[end - (CONV_K - 1):end]


def _conv(proj, buf8, w, l, tc, valid_last):
    bsz, t, _ = proj.shape
    tc = min(tc, t)
    per = tc // SUBLANES
    col = lambda c: pl.BlockSpec((None, tc, MIX_W), lambda b, i: (b, i, c))
    halo = lambda c: pl.BlockSpec((None, SUBLANES, MIX_W), lambda b, i: (b, jnp.maximum(i * per - 1, 0), c))
    kern = functools.partial(_conv_kernel, valid_last=valid_last)
    return pl.pallas_call(
        kern,
        out_shape=(jax.ShapeDtypeStruct((bsz, t, MIX_W), BF16),
                   jax.ShapeDtypeStruct((bsz, CONV_K - 1, MIX_W), F32)),
        grid=(bsz, t // tc),
        in_specs=[col(COL_CB), col(COL_CC), col(COL_CX), halo(COL_CC), halo(COL_CX),
                  pl.BlockSpec((None, SUBLANES, MIX_W), lambda b, i: (b, 0, 0)),
                  pl.BlockSpec((None, CONV_K, MIX_W), lambda b, i: (l, 0, 0))],
        out_specs=(pl.BlockSpec((None, tc, MIX_W), lambda b, i: (b, i, 0)),
                   pl.BlockSpec((None, CONV_K - 1, MIX_W), lambda b, i: (b, 0, 0))),
        compiler_params=_params(("parallel", "arbitrary")),
        name="short_conv",
    )(proj, proj, proj, proj, proj, buf8, w)


def _mixers(h, wts, l, past, sizes, kv_bufs):
    bsz, t, d = h.shape
    tm = sizes["tm"]
    flat = sizes["flat"]
    as_mm = lambda a: a.reshape(flat + a.shape[2:])
    as_seq = lambda a: a.reshape((bsz, t) + a.shape[2:])

    hm = as_mm(h)
    proj, gates, kv_bufs = _in_proj(hm, wts["w_in"], wts["w_if"], l, kv_bufs, tm)
    proj, gates = as_seq(proj), as_seq(gates)
    gates_t = jnp.swapaxes(gates[:, :, :SUBLANES], 1, 2)
    if sizes["chunk"] > t:
        gates_t = jnp.pad(gates_t, ((0, 0), (0, 0), (0, sizes["chunk"] - t)))

    y_a, mc, mn, mm = _mlstm(proj, gates, gates_t, wts["gate_bias_c"], wts["gate_bias_r"], wts["g_head"], l,
                             past["mlstm_c"], past["mlstm_n"], past["mlstm_m"],
                             sizes["chunk"], sizes["rows"], sizes["valid"])
    if "sb_k" in past:
        y_b = _sb_sample(proj, past["sb_k"], past["sb_v"], past["page_table"], wts["b_sb"], l)
    else:
        y_b = _sb_prompt(proj, wts["b_sb"], l)
    y_s, s_re, s_im = _s5(proj, wts["s5_wb"], wts["s5_tab"], past["s5_re"], past["s5_im"],
                          wts["s5_wc_re"], wts["s5_wc_im"], wts["s5_d"], wts["w_s5_glu"], l,
                          sizes["tc"], sizes["valid_last"])
    y_c, conv_new = _conv(proj, past["conv"], wts["conv_w"], l, sizes["tc"], sizes["valid_last"])

    state = {"mlstm_c": mc, "mlstm_n": mn.reshape(bsz, N_HEADS, HEAD_DIM), "mlstm_m": mm.reshape(bsz, N_HEADS),
             "s5_re": s_re.reshape(bsz, S5_GROUPS, S5_STATE), "s5_im": s_im.reshape(bsz, S5_GROUPS, S5_STATE),
             "conv": conv_new}
    return hm, [as_mm(y) for y in (y_a, y_b, y_s, y_c)], state, kv_bufs


def _merge_and_ffn(xs, branches, mods, next_norms, wts, l, sizes):
    (hm_p, ys_p), (hm_s, ys_s) = branches
    merged = _gate_merge(hm_p, ys_p, hm_s, ys_s, wts["w_gate"], wts["b_gate"], wts["w_branch"], l,
                         sizes[0]["tm"], 256)
    x_mm, h2 = [], []
    for x, mod, size, mrg in zip(xs, mods, sizes, merged):
        as_mm = lambda a, size=size: a if a.shape[1] == 1 else a.reshape(size["flat"] + a.shape[2:])
        g1, sc2, sh2 = mod[2], mod[4], mod[3]
        xo, ho = _linear_residual_norm(mrg, wts["w_out"], l, x.reshape(size["flat"] + x.shape[2:]), as_mm(g1),
                                       (wts["g_norm2"], l, as_mm(sc2), as_mm(sh2), BF16), size["tm_res"], 1,
                                       "out_proj")
        x_mm.append(xo)
        h2.append(ho)
    hidden = _ffn_up(h2[0], h2[1], wts["w_ffn_gate"], wts["w_ffn_up"], l, sizes[0]["tm"], 512)
    g2_p, g2_s = mods[0][5], mods[1][5]
    x_new = _ffn_down(hidden[0], hidden[1], wts["w_ffn_down"], l, x_mm[0], g2_p, x_mm[1],
                      g2_s.reshape(sizes[1]["flat"] + g2_s.shape[2:]), 512, 512)
    out = []
    for x, xn, nrm in zip(xs, x_new, next_norms):
        xn = xn.reshape(x.shape)
        out.append((xn, _norm_mod(xn, *nrm)))
    return out


def kernel(x_prompt, x_sample, cache_sb_k, cache_sb_v, state_mlstm_c, state_mlstm_n, state_mlstm_m, state_s5_re, state_s5_im, state_conv, page_table, c_prompt, c_sample, w_ada, b_ada, g_norm1, g_norm2, w_in, b_mlstm_i, b_mlstm_f, g_mlstm_head, b_sb, s5_lambda_re, s5_lambda_im, s5_b_re, s5_b_im, s5_c_re, s5_c_im, s5_d, s5_log_step, w_s5_glu, conv_w, w_gate, b_gate, w_branch, w_out, w_ffn_gate, w_ffn_up, w_ffn_down, g_final):
    depth = w_in.shape[0]
    bp, tp, d = x_prompt.shape
    bs, ts, _ = x_sample.shape
    tpad = SUBLANES
    n_gate = N_GATE_COLS
    split = COL_BQ * MIX_W

    gate_bias = jnp.concatenate([b_mlstm_i, b_mlstm_f], axis=1)
    w_in16 = w_in.astype(BF16)
    wts = {
        "g_norm1": g_norm1.reshape(depth, 1, d), "g_norm2": g_norm2.reshape(depth, 1, d),
        "w_in": jnp.concatenate([w_in16[:, :, :split], w_in16[:, :, split + n_gate:]], axis=2),
        "w_if": jnp.pad(w_in16[:, :, split:split + n_gate], ((0, 0), (0, 0), (0, GATE_LANES - n_gate))),
        "gate_bias_c": jnp.pad(gate_bias, ((0, 0), (0, GATE_LANES - n_gate))).reshape(depth, 1, GATE_LANES),
        "gate_bias_r": gate_bias.reshape(depth, n_gate, 1),
        "g_head": g_mlstm_head.reshape(depth, 1, MIX_W),
        "b_sb": b_sb,
        "s5_d": s5_d.reshape(depth, 1, MIX_W),
        "w_s5_glu": w_s5_glu.astype(BF16),
        "conv_w": conv_w,
        "w_gate": w_gate, "b_gate": b_gate.reshape(depth, 1, N_BRANCH * d), "w_branch": w_branch,
        "w_out": w_out.astype(BF16),
        "w_ffn_gate": w_ffn_gate, "w_ffn_up": w_ffn_up,
        "w_ffn_down": w_ffn_down,
    }
    tc_p = 512
    wts["s5_tab"], bb_re, bb_im = _s5_prep(s5_lambda_re, s5_lambda_im, s5_log_step, s5_b_re, s5_b_im)
    to_gcp = lambda a: a.reshape(depth, S5_GROUP_CH, S5_GROUPS, S5_STATE).transpose(0, 2, 1, 3)
    wts["s5_wb"] = jnp.concatenate([_block_diag_groups(to_gcp(bb_re)), _block_diag_groups(to_gcp(bb_im))],
                                   axis=3).astype(BF16)
    wts["s5_wc_re"] = _block_diag_groups(s5_c_re.transpose(0, 1, 3, 2)).astype(BF16)
    wts["s5_wc_im"] = _block_diag_groups(s5_c_im.transpose(0, 1, 3, 2)).astype(BF16)

    c_all = jnp.concatenate([c_prompt, c_sample], axis=0)
    c_all = jnp.pad(c_all, ((0, -c_all.shape[0] % SUBLANES), (0, 0)))
    mod_all = _ada_all(c_all, w_ada, b_ada).reshape(depth, c_all.shape[0], 6, d)

    sizes_p = {"tm": 1024, "tm_res": 512, "flat": (bp, tp), "chunk": 256, "rows": 256, "valid": 256, "tc": tc_p,
               "valid_last": tc_p}
    sizes_s = {"tm": bs * tpad, "tm_res": bs * tpad, "flat": (1, bs * tpad), "chunk": 128, "rows": tpad, "valid": ts,
               "tc": tpad, "valid_last": ts}

    xp = x_prompt
    xs = jnp.pad(x_sample, ((0, 0), (0, tpad - ts), (0, 0)))
    zeros_p = {
        "mlstm_c": jnp.zeros((bp, N_HEADS, HEAD_DIM, HEAD_DIM), F32),
        "mlstm_n": jnp.zeros((bp, N_HEADS, 1, HEAD_DIM), F32),
        "mlstm_m": jnp.zeros((bp, N_HEADS, 1, 1), F32),
        "s5_re": jnp.zeros((bp, 1, S5_WIDTH), F32), "s5_im": jnp.zeros((bp, 1, S5_WIDTH), F32),
        "conv": jnp.zeros((bp, SUBLANES, MIX_W), F32),
    }
    new_p, new_s = [], []
    kv_p = tuple(jnp.zeros((depth, bp, tp, MIX_W), F32) for _ in range(2))
    kv_s = tuple(jnp.zeros((depth, 1, bs * tpad, MIX_W), F32) for _ in range(2))
    mods_p = [[mod_all[l, :bp, i].reshape(bp, 1, d) for i in range(6)] for l in range(depth)]
    mods_s = [[jnp.repeat(mod_all[l, bp:bp + bs, i], tpad, axis=0).reshape(bs, tpad, d) for i in range(6)]
              for l in range(depth)]
    hp = _norm_mod(xp, wts["g_norm1"], 0, mods_p[0][1], mods_p[0][0])
    hs = _norm_mod(xs, wts["g_norm1"], 0, mods_s[0][1], mods_s[0][0])
    g_last = g_final.reshape(1, 1, d)

    def following_norm(mods, l, like):
        if l + 1 < depth:
            return wts["g_norm1"], l + 1, mods[l + 1][1], mods[l + 1][0], BF16
        return g_last, 0, jnp.zeros_like(like), jnp.zeros_like(like), F32

    for l in range(depth):
        mod_p, mod_s = mods_p[l], mods_s[l]
        past_s = {
            "sb_k": cache_sb_k, "sb_v": cache_sb_v, "page_table": page_table,
            "mlstm_c": state_mlstm_c[l], "mlstm_n": state_mlstm_n[l].reshape(bs, N_HEADS, 1, HEAD_DIM),
            "mlstm_m": state_mlstm_m[l].reshape(bs, N_HEADS, 1, 1),
            "s5_re": state_s5_re[l].reshape(bs, 1, S5_WIDTH), "s5_im": state_s5_im[l].reshape(bs, 1, S5_WIDTH),
            "conv": jnp.pad(state_conv[l], ((0, 0), (SUBLANES - (CONV_K - 1), 0), (0, 0))),
        }
        hm_p, ys_p, st_p, kv_p = _mixers(hp, wts, l, zeros_p, sizes_p, kv_p)
        hm_s, ys_s, st_s, kv_s = _mixers(hs, wts, l, past_s, sizes_s, kv_s)
        new_p.append(st_p)
        new_s.append(st_s)
        (xp, hp), (xs, hs) = _merge_and_ffn(
            (xp, xs), ((hm_p, ys_p), (hm_s, ys_s)), (mod_p, mod_s),
            (following_norm(mods_p, l, mod_p[0]), following_norm(mods_s, l, mod_s[0])), wts, l, (sizes_p, sizes_s))

    y_prompt = hp
    y_sample = hs[:, :ts]
    stk = lambda states, name: jnp.stack([s[name] for s in states])
    heads_p = lambda a: a.reshape(depth, bp, tp, N_HEADS, HEAD_DIM)
    heads_s = lambda a: a.reshape(depth, bs, tpad, N_HEADS, HEAD_DIM)[:, :, :ts]
    return (y_prompt, y_sample,
            heads_p(kv_p[0]), heads_p(kv_p[1]), heads_s(kv_s[0]), heads_s(kv_s[1]),
            stk(new_p, "mlstm_c"), stk(new_p, "mlstm_n"), stk(new_p, "mlstm_m"),
            stk(new_s, "mlstm_c"), stk(new_s, "mlstm_n"), stk(new_s, "mlstm_m"),
            stk(new_p, "s5_re"), stk(new_p, "s5_im"), stk(new_s, "s5_re"), stk(new_s, "s5_im"),
            stk(new_p, "conv"), stk(new_s, "conv"))
```

```python
import functools
import math

import jax
import jax.numpy as jnp
from jax import lax
from jax.experimental import pallas as pl
from jax.experimental.pallas import tpu as pltpu

F32 = jnp.float32
BF16 = jnp.bfloat16

N_HEADS = 4
HEAD_DIM = 128
MIX_W = N_HEADS * HEAD_DIM
N_BRANCH = 4
S5_GROUP_CH = 16
S5_GROUPS = MIX_W // S5_GROUP_CH
S5_STATE = 64
S5_WIDTH = S5_GROUPS * S5_STATE
S5_SUPER = 4
CONV_K = 3
EPS = 1e-6
LOG2_E = 1.0 / math.log(2.0)
SUBLANES = 8
LANES = 128
GATE_LANES = 128
V7X_VMEM_LIMIT = 56 * 1024 * 1024

COL_AQ, COL_AK, COL_AV, COL_AO, COL_BQ, COL_BK, COL_BV, COL_SU, COL_CB, COL_CC, COL_CX = range(11)

NT_DIMS = (((1,), (1,)), ((), ()))
TN_DIMS = (((0,), (0,)), ((), ()))


def _dot(a, b):
    return jnp.dot(a, b, preferred_element_type=F32)


def _dot_nt(a, b):
    return lax.dot_general(a, b, NT_DIMS, preferred_element_type=F32)


def _dot_exact(a, b):
    return jnp.dot(a, b, precision=lax.Precision.HIGHEST, preferred_element_type=F32)


def _softplus_neg_abs(z):
    return jnp.log(1.0 + jnp.exp(-jnp.abs(z)))


def _log_sigmoid(z):
    return jnp.minimum(z, 0.0) - _softplus_neg_abs(z)


def _pad_rows(x, rows):
    if x.shape[0] == rows:
        return x
    return jnp.concatenate([x, jnp.zeros((rows - x.shape[0],) + x.shape[1:], x.dtype)], axis=0)


def _params(sem, vmem=None):
    return pltpu.CompilerParams(dimension_semantics=sem, vmem_limit_bytes=vmem)


def _ada_kernel(c_ref, w_ref, b_ref, o_ref):
    c = c_ref[...]
    a = (c * jax.nn.sigmoid(c)).astype(BF16)
    o_ref[...] = _dot(a, w_ref[...].astype(BF16)) + b_ref[...]


def _ada_all(c, w_ada, b_ada, tn=1024):
    depth, d, n = w_ada.shape
    rows = c.shape[0]
    return pl.pallas_call(
        _ada_kernel,
        out_shape=jax.ShapeDtypeStruct((depth, rows, n), F32),
        grid=(depth, n // tn),
        in_specs=[pl.BlockSpec((rows, d), lambda l, j: (0, 0)),
                  pl.BlockSpec((None, d, tn), lambda l, j: (l, 0, j)),
                  pl.BlockSpec((None, 1, tn), lambda l, j: (l, 0, j))],
        out_specs=pl.BlockSpec((None, rows, tn), lambda l, j: (l, 0, j)),
        compiler_params=_params(("parallel", "parallel"), V7X_VMEM_LIMIT),
        name="ada_mod",
    )(c, w_ada, b_ada.reshape(depth, 1, n))


def _norm_mod_kernel(x_ref, g_ref, sc_ref, sh_ref, o_ref):
    x = x_ref[...]
    y = x * lax.rsqrt(jnp.mean(x * x, axis=-1, keepdims=True) + EPS) * g_ref[...]
    o_ref[...] = (y * (1.0 + sc_ref[...]) + sh_ref[...]).astype(o_ref.dtype)


def _row_spec(arr, tm, width, col):
    if arr.shape[1] == 1:
        return pl.BlockSpec((None, 1, width), lambda b, i, *r: (b, 0, col(*r)))
    return pl.BlockSpec((None, tm, width), lambda b, i, *r: (b, i, col(*r)))


def _norm_mod(x, g, l, sc, sh, out_dtype=BF16, tm=512):
    bsz, t, d = x.shape
    tm = min(tm, t)
    zero = lambda *r: 0
    return pl.pallas_call(
        _norm_mod_kernel,
        out_shape=jax.ShapeDtypeStruct((bsz, t, d), out_dtype),
        grid=(bsz, t // tm),
        in_specs=[pl.BlockSpec((None, tm, d), lambda b, i: (b, i, 0)),
                  pl.BlockSpec((None, 1, d), lambda b, i: (l, 0, 0)),
                  _row_spec(sc, tm, d, zero), _row_spec(sh, tm, d, zero)],
        out_specs=pl.BlockSpec((None, tm, d), lambda b, i: (b, i, 0)),
        compiler_params=_params(("parallel", "parallel")),
        name="norm_mod",
    )(x, g, sc, sh)


N_GATE_COLS = 2 * N_HEADS


def _in_proj_kernel(a_ref, w_ref, wif_ref, kb_in, vb_in, o_ref, gates_ref, kb_ref, vb_ref):
    del kb_in, vb_in
    j = pl.program_id(2)
    a = a_ref[...]
    res = _dot(a, w_ref[...])
    o_ref[...] = res

    @pl.when(j == 0)
    def _():
        gates_ref[...] = _dot(a, wif_ref[...])

    @pl.when(j == COL_BK)
    def _():
        kb_ref[...] = res

    @pl.when(j == COL_BV)
    def _():
        vb_ref[...] = res


def _in_proj(a, w, w_if, l, kv_bufs, tm):
    bsz, t, k = a.shape
    depth = w.shape[0]
    n = w.shape[-1]
    tm = min(tm, t)
    buf = jax.ShapeDtypeStruct((depth, bsz, t, MIX_W), F32)
    buf_spec = pl.BlockSpec((None, None, tm, MIX_W), lambda b, i, j: (l, b, i, 0))
    carried = list(kv_bufs)
    proj, gates, kbuf, vbuf = pl.pallas_call(
        _in_proj_kernel,
        out_shape=(jax.ShapeDtypeStruct((bsz, t, n), F32), jax.ShapeDtypeStruct((bsz, t, GATE_LANES), F32), buf, buf),
        grid=(bsz, t // tm, n // MIX_W),
        in_specs=[pl.BlockSpec((None, tm, k), lambda b, i, j: (b, i, 0)),
                  pl.BlockSpec((None, k, MIX_W), lambda b, i, j: (l, 0, j)),
                  pl.BlockSpec((None, k, GATE_LANES), lambda b, i, j: (l, 0, 0))]
                 + [pl.BlockSpec(memory_space=pl.ANY)] * len(carried),
        out_specs=(pl.BlockSpec((None, tm, MIX_W), lambda b, i, j: (b, i, j)),
                   pl.BlockSpec((None, tm, GATE_LANES), lambda b, i, j: (b, i, 0)), buf_spec, buf_spec),
        input_output_aliases={3 + n_: 2 + n_ for n_ in range(len(carried))},
        compiler_params=_params(("parallel", "parallel", "arbitrary"), V7X_VMEM_LIMIT),
        name="in_proj",
    )(a, w, w_if, *carried)
    return proj, gates, (kbuf, vbuf)


def _mm_res_norm_kernel(a_ref, w_ref, x_ref, g_ref, gn_ref, sc_ref, sh_ref, xo_ref, ho_ref, *acc, nk):
    part = _dot(a_ref[...], w_ref[...])

    def finish(total):
        x = x_ref[...] + g_ref[...] * total
        xo_ref[...] = x
        y = x * lax.rsqrt(jnp.mean(x * x, axis=-1, keepdims=True) + EPS) * gn_ref[...]
        ho_ref[...] = (y * (1.0 + sc_ref[...]) + sh_ref[...]).astype(ho_ref.dtype)

    if nk == 1:
        finish(part)
        return
    acc_ref, = acc
    kk = pl.program_id(2)

    @pl.when(kk == 0)
    def _():
        acc_ref[...] = part

    @pl.when((kk > 0) & (kk < nk - 1))
    def _():
        acc_ref[...] += part

    @pl.when(kk == nk - 1)
    def _():
        finish(acc_ref[...] + part)


def _linear_residual_norm(a, w, l, x, g, norm, tm, nk, name):
    bsz, t, k = a.shape
    n = w.shape[-1]
    gn, ln, sc, sh, h_dtype = norm
    tm = min(tm, t)
    tk = k // nk
    zero = lambda kk: 0
    row = pl.BlockSpec((None, tm, n), lambda b, i, kk: (b, i, 0))
    return pl.pallas_call(
        functools.partial(_mm_res_norm_kernel, nk=nk),
        out_shape=(jax.ShapeDtypeStruct((bsz, t, n), F32), jax.ShapeDtypeStruct((bsz, t, n), h_dtype)),
        grid=(bsz, t // tm, nk),
        in_specs=[pl.BlockSpec((None, tm, tk), lambda b, i, kk: (b, i, kk)),
                  pl.BlockSpec((None, tk, n), lambda b, i, kk: (l, kk, 0)),
                  row, _row_spec(g, tm, n, zero),
                  pl.BlockSpec((None, 1, n), lambda b, i, kk: (ln, 0, 0)),
                  _row_spec(sc, tm, n, zero), _row_spec(sh, tm, n, zero)],
        out_specs=(row, row),
        scratch_shapes=[pltpu.VMEM((tm, n), F32)] if nk > 1 else [],
        compiler_params=_params(("parallel", "parallel", "arbitrary"), V7X_VMEM_LIMIT),
        name=name,
    )(a, w, x, g, gn, sc, sh)


def _first_row_tile():
    return (pl.program_id(1) == 0) & (pl.program_id(2) == 0)


def _side_spec(rows, width, col):
    return pl.BlockSpec((None, rows, width), lambda j, b, i: (0, 0, col(j)))


def _ffn_up_kernel(a_ref, as_ref, wg_ref, wu_ref, o_ref, os_ref, wg_bf, wu_bf):
    def act(a):
        gate = _dot(a, wg_bf[...])
        return (gate * jax.nn.sigmoid(gate) * _dot(a, wu_bf[...])).astype(o_ref.dtype)

    @pl.when(_first_row_tile())
    def _():
        wg_bf[...] = wg_ref[...].astype(BF16)
        wu_bf[...] = wu_ref[...].astype(BF16)
        os_ref[...] = act(as_ref[...])

    o_ref[...] = act(a_ref[...])


def _ffn_up(a, a_side, wg, wu, l, tm, tn):
    bsz, t, k = a.shape
    ts = a_side.shape[1]
    n = wg.shape[-1]
    tm, tn = min(tm, t), min(tn, n)
    wspec = pl.BlockSpec((None, k, tn), lambda j, b, i: (l, 0, j))
    return pl.pallas_call(
        _ffn_up_kernel,
        out_shape=(jax.ShapeDtypeStruct((bsz, t, n), BF16), jax.ShapeDtypeStruct((1, ts, n), BF16)),
        grid=(n // tn, bsz, t // tm),
        in_specs=[pl.BlockSpec((None, tm, k), lambda j, b, i: (b, i, 0)), _side_spec(ts, k, lambda j: 0),
                  wspec, wspec],
        out_specs=(pl.BlockSpec((None, tm, tn), lambda j, b, i: (b, i, j)), _side_spec(ts, tn, lambda j: j)),
        scratch_shapes=[pltpu.VMEM((k, tn), BF16), pltpu.VMEM((k, tn), BF16)],
        compiler_params=_params(("arbitrary", "arbitrary", "arbitrary"), V7X_VMEM_LIMIT),
        name="ffn_up",
    )(a, a_side, wg, wu)


def _ffn_down_kernel(a_ref, as_ref, w_ref, x_ref, g_ref, xs_ref, gs_ref, o_ref, os_ref, w_bf):
    @pl.when(_first_row_tile())
    def _():
        w_bf[...] = w_ref[...].astype(BF16)
        os_ref[...] = xs_ref[...] + gs_ref[...] * _dot(as_ref[...], w_bf[...])

    o_ref[...] = x_ref[...] + g_ref[...] * _dot(a_ref[...], w_bf[...])


def _ffn_down(a, a_side, w, l, x, g, x_side, g_side, tm, tn):
    bsz, t, k = a.shape
    ts = a_side.shape[1]
    n = w.shape[-1]
    tm, tn = min(tm, t), min(tn, n)
    tile = pl.BlockSpec((None, tm, tn), lambda j, b, i: (b, i, j))
    side = _side_spec(ts, tn, lambda j: j)
    return pl.pallas_call(
        _ffn_down_kernel,
        out_shape=(jax.ShapeDtypeStruct((bsz, t, n), F32), jax.ShapeDtypeStruct((1, ts, n), F32)),
        grid=(n // tn, bsz, t // tm),
        in_specs=[pl.BlockSpec((None, tm, k), lambda j, b, i: (b, i, 0)), _side_spec(ts, k, lambda j: 0),
                  pl.BlockSpec((None, k, tn), lambda j, b, i: (l, 0, j)),
                  tile, pl.BlockSpec((None, 1, tn), lambda j, b, i: (b, 0, j)), side, side],
        out_specs=(tile, side),
        scratch_shapes=[pltpu.VMEM((k, tn), BF16)],
        compiler_params=_params(("arbitrary", "arbitrary", "arbitrary"), V7X_VMEM_LIMIT),
        name="ffn_down",
    )(a, a_side, w, x, g, x_side, g_side)


def _gate_merge_kernel(*refs):
    nb = N_BRANCH
    h_ref, y_refs, hs_ref, ys_refs = refs[0], refs[1:1 + nb], refs[1 + nb], refs[2 + nb:2 + 2 * nb]
    wg_refs, bg_refs = refs[2 + 2 * nb:2 + 3 * nb], refs[2 + 3 * nb:2 + 4 * nb]
    wb_ref, o_ref, os_ref, wg_bf, wb_bf = refs[2 + 4 * nb:]

    def merge(h, ys):
        acc = None
        for g in range(nb):
            gate = jax.nn.sigmoid(_dot(h, wg_bf[g]) + bg_refs[g][...])
            term = gate * _dot(ys[g][...], wb_bf[g])
            acc = term if acc is None else acc + term
        return acc.astype(o_ref.dtype)

    @pl.when(_first_row_tile())
    def _():
        for g in range(nb):
            wg_bf[g] = wg_refs[g][...].astype(BF16)
        wb_bf[...] = wb_ref[...].astype(BF16)
        os_ref[...] = merge(hs_ref[...], ys_refs)

    o_ref[...] = merge(h_ref[...], y_refs)


def _gate_merge(h, ys, h_side, ys_side, wg, bg, wb, l, tm, tn):
    bsz, t, k = h.shape
    ts = h_side.shape[1]
    d = wb.shape[-1]
    tm, tn = min(tm, t), min(tn, d)
    per = d // tn
    zero = lambda j: 0
    yspec = pl.BlockSpec((None, tm, MIX_W), lambda j, b, i: (b, i, 0))
    wspec = lambda g: pl.BlockSpec((None, k, tn), lambda j, b, i: (l, 0, g * per + j))
    bspec = lambda g: pl.BlockSpec((None, 1, tn), lambda j, b, i: (l, 0, g * per + j))
    branches = range(N_BRANCH)
    return pl.pallas_call(
        _gate_merge_kernel,
        out_shape=(jax.ShapeDtypeStruct((bsz, t, d), BF16), jax.ShapeDtypeStruct((1, ts, d), BF16)),
        grid=(d // tn, bsz, t // tm),
        in_specs=[pl.BlockSpec((None, tm, k), lambda j, b, i: (b, i, 0))] + [yspec] * N_BRANCH
                 + [_side_spec(ts, k, zero)] + [_side_spec(ts, MIX_W, zero)] * N_BRANCH
                 + [wspec(g) for g in branches] + [bspec(g) for g in branches]
                 + [pl.BlockSpec((None, N_BRANCH, MIX_W, tn), lambda j, b, i: (l, 0, 0, j))],
        out_specs=(pl.BlockSpec((None, tm, tn), lambda j, b, i: (b, i, j)), _side_spec(ts, tn, lambda j: j)),
        scratch_shapes=[pltpu.VMEM((N_BRANCH, k, tn), BF16), pltpu.VMEM((N_BRANCH, MIX_W, tn), BF16)],
        compiler_params=_params(("arbitrary", "arbitrary", "arbitrary"), V7X_VMEM_LIMIT),
        name="gate_merge",
    )(h, *ys, h_side, *ys_side, *([wg] * N_BRANCH), *([bg] * N_BRANCH), wb)


def _mlstm_kernel(q_ref, k_ref, v_ref, og_ref, gc_ref, gr_ref, bc_ref, br_ref, gh_ref, c0_ref, n0_ref, m0_ref,
                  y_ref, c_out, n_out, m_out, c_scr, n_scr, m_scr, *, chunk, valid):
    ci = pl.program_id(1)
    rows = q_ref.shape[0]

    @pl.when(ci == 0)
    def _():
        c_scr[...] = c0_ref[...]
        n_scr[...] = n0_ref[...]
        m_scr[...] = m0_ref[...]

    pos_c = lax.broadcasted_iota(jnp.int32, (chunk, 1), 0)
    pos_r = lax.broadcasted_iota(jnp.int32, (1, chunk), 1)
    tri_r = lax.broadcasted_iota(jnp.int32, (chunk, chunk), 0)
    tri_c = lax.broadcasted_iota(jnp.int32, (chunk, chunk), 1)
    causal = tri_c <= tri_r
    lower = causal.astype(F32)
    upper = (tri_r <= tri_c).astype(F32)

    gates_c = _pad_rows(gc_ref[...], chunk) + bc_ref[...]
    gates_r = gr_ref[...] + br_ref[...]
    lf_c = _log_sigmoid(gates_c)
    lf_r = _log_sigmoid(gates_r)
    if valid < chunk:
        lf_c = jnp.where(pos_c < valid, lf_c, 0.0)
        lf_r = jnp.where(pos_r < valid, lf_r, 0.0)
        gates_c = jnp.where(pos_c < valid, gates_c, -jnp.inf)
        gates_r = jnp.where(pos_r < valid, gates_r, -jnp.inf)
    cum_c = _dot_exact(lower, lf_c)
    cum_r = _dot_exact(lf_r, upper)

    heads = range(N_HEADS)
    sls = [slice(h * HEAD_DIM, (h + 1) * HEAD_DIM) for h in heads]
    qs = [_pad_rows(q_ref[:, sl], chunk) for sl in sls]
    ks = [_pad_rows(k_ref[:, sl], chunk) * (HEAD_DIM ** -0.5) for sl in sls]
    vs = [_pad_rows(v_ref[:, sl], chunk) for sl in sls]
    qbs, kbs, vbs = ([x.astype(BF16) for x in xs] for xs in (qs, ks, vs))
    b_cs = [cum_c[:, N_HEADS + h:N_HEADS + h + 1] for h in heads]
    b_rs = [cum_r[N_HEADS + h:N_HEADS + h + 1, :] for h in heads]
    m_prevs = [m_scr[h] for h in heads]
    c_prevs = [c_scr[h] for h in heads]
    n_prevs = [n_scr[h] for h in heads]

    qk = [_dot_nt(qbs[h], kbs[h]) for h in heads]
    cq = [_dot_nt(qbs[h], c_prevs[h].astype(BF16)) for h in heads]

    s_all, w_inters, m_ts = [], [], []
    for h in heads:
        d_intra = jnp.where(causal, b_cs[h] - b_rs[h] + gates_r[h:h + 1, :], -jnp.inf)
        m_inter = b_cs[h] + m_prevs[h]
        m_t = jnp.maximum(m_inter, jnp.max(d_intra, axis=1, keepdims=True))
        s_all.append(jnp.exp(d_intra - m_t) * qk[h])
        w_inters.append(jnp.exp(m_inter - m_t))
        m_ts.append(m_t)
    sv = [_dot(s_all[h].astype(BF16), vbs[h]) for h in heads]

    for h in heads:
        num = w_inters[h] * cq[h] + sv[h]
        den = (w_inters[h] * jnp.sum(qs[h] * n_prevs[h], axis=1, keepdims=True)
               + jnp.sum(s_all[h], axis=1, keepdims=True))
        hh = num / jnp.maximum(jnp.abs(den), jnp.exp(-m_ts[h]))
        hn = hh * lax.rsqrt(jnp.mean(hh * hh, axis=-1, keepdims=True) + EPS) * gh_ref[:, sls[h]]
        y = hn[:rows] * jax.nn.sigmoid(og_ref[:, sls[h]])
        y_ref[:, sls[h]] = y.astype(y_ref.dtype)

    w_ks, decays, m_ends = [], [], []
    for h in heads:
        b_end = b_cs[h][chunk - 1:chunk, :]
        log_w = b_end - b_cs[h] + gates_c[:, h:h + 1]
        m_end = jnp.maximum(b_end + m_prevs[h], jnp.max(log_w, axis=0, keepdims=True))
        w_ks.append(jnp.exp(log_w - m_end))
        decays.append(jnp.exp(b_end + m_prevs[h] - m_end))
        m_ends.append(m_end)
    kv = [lax.dot_general((w_ks[h] * vs[h]).astype(BF16), kbs[h], TN_DIMS, preferred_element_type=F32)
          for h in heads]
    for h in heads:
        c_scr[h] = decays[h] * c_prevs[h] + kv[h]
        n_scr[h] = decays[h] * n_prevs[h] + jnp.sum(w_ks[h] * ks[h], axis=0, keepdims=True)
        m_scr[h] = m_ends[h]

    @pl.when(ci == pl.num_programs(1) - 1)
    def _():
        c_out[...] = c_scr[...]
        n_out[...] = n_scr[...]
        m_out[...] = m_scr[...]


def _mlstm(proj, gates, gates_t, bias_c, bias_r, ghead, l, c0, n0, m0, chunk, rows, valid):
    bsz, t, _ = proj.shape
    nc = t // rows
    col = lambda c: pl.BlockSpec((None, rows, MIX_W), lambda b, i: (b, i, c))
    st4 = lambda s: pl.BlockSpec((None,) + s, lambda b, i: (b, 0, 0, 0))
    kern = functools.partial(_mlstm_kernel, chunk=chunk, valid=valid)
    return pl.pallas_call(
        kern,
        out_shape=(jax.ShapeDtypeStruct((bsz, t, MIX_W), BF16),
                   jax.ShapeDtypeStruct((bsz, N_HEADS, HEAD_DIM, HEAD_DIM), F32),
                   jax.ShapeDtypeStruct((bsz, N_HEADS, 1, HEAD_DIM), F32),
                   jax.ShapeDtypeStruct((bsz, N_HEADS, 1, 1), F32)),
        grid=(bsz, nc),
        in_specs=[col(COL_AQ), col(COL_AK), col(COL_AV), col(COL_AO),
                  pl.BlockSpec((None, rows, GATE_LANES), lambda b, i: (b, i, 0)),
                  pl.BlockSpec((None, SUBLANES, chunk), lambda b, i: (b, 0, i)),
                  pl.BlockSpec((None, 1, GATE_LANES), lambda b, i: (l, 0, 0)),
                  pl.BlockSpec((None, SUBLANES, 1), lambda b, i: (l, 0, 0)),
                  pl.BlockSpec((None, 1, MIX_W), lambda b, i: (l, 0, 0)),
                  st4((N_HEADS, HEAD_DIM, HEAD_DIM)), st4((N_HEADS, 1, HEAD_DIM)), st4((N_HEADS, 1, 1))],
        out_specs=(pl.BlockSpec((None, rows, MIX_W), lambda b, i: (b, i, 0)),
                   st4((N_HEADS, HEAD_DIM, HEAD_DIM)), st4((N_HEADS, 1, HEAD_DIM)), st4((N_HEADS, 1, 1))),
        scratch_shapes=[pltpu.VMEM((N_HEADS, HEAD_DIM, HEAD_DIM), F32),
                        pltpu.VMEM((N_HEADS, 1, HEAD_DIM), F32),
                        pltpu.VMEM((N_HEADS, 1, 1), F32)],
        compiler_params=_params(("parallel", "arbitrary"), V7X_VMEM_LIMIT),
        name="mlstm",
    )(proj, proj, proj, proj, gates, gates_t, bias_c, bias_r, ghead, c0, n0, m0)


def _sb_blocks(zs, carries, suffix, mask, chained):
    rows, n = zs[0].shape
    w = suffix.shape[0]
    nt = n // w
    tiles = [slice(t * w, (t + 1) * w) for t in range(nt)]
    ls_all, stacked, sums = [], [], []
    for z in zs:
        sp = jnp.log(1.0 + jnp.exp2(-jnp.abs(z))) * LOG2_E
        ls = jnp.minimum(z, 0.0) - sp
        ls_all.append(ls)
        l1m = ls - z
        if mask is not None:
            l1m = jnp.where(mask, l1m, 0.0)
        hi = l1m.astype(BF16)
        lo = (l1m - hi.astype(F32)).astype(BF16)
        stacked += [hi[:, s] for s in tiles] + [lo[:, s] for s in tiles]
        sums.append([jnp.sum(l1m[:, s], axis=1, keepdims=True) for s in tiles])
    ex = _dot(jnp.concatenate(stacked, axis=0), suffix)
    weights, out = [], []
    later = carries[0]
    for u in range(len(zs)):
        if not chained:
            later = carries[u]
        base = u * 2 * nt * rows
        parts = [None] * nt
        for t in reversed(range(nt)):
            hi_rows = ex[base + t * rows:base + (t + 1) * rows]
            lo_rows = ex[base + (nt + t) * rows:base + (nt + t + 1) * rows]
            parts[t] = hi_rows + lo_rows + later
            later = later + sums[u][t]
        excl = parts[0] if nt == 1 else jnp.concatenate(parts, axis=1)
        a = jnp.exp2(ls_all[u] + excl)
        if mask is not None:
            a = jnp.where(mask, a, 0.0)
        weights.append(a)
        out.append(later)
    return weights, out


def _strict_suffix_matrix(n):
    r = lax.broadcasted_iota(jnp.int32, (n, n), 0)
    c = lax.broadcasted_iota(jnp.int32, (n, n), 1)
    return (r > c).astype(BF16)


SB_HEADS_PER_STEP = 4
SB_HEADS_PER_MATMUL = 2


def _sb_prompt_kernel(bias_ref, q_ref, k_ref, v_ref, o_ref, *, blk):
    hp = pl.program_id(1)
    i = pl.program_id(2)
    scale = HEAD_DIM ** -0.5 * LOG2_E
    heads = range(SB_HEADS_PER_STEP)
    lanes = [slice(n * HEAD_DIM, (n + 1) * HEAD_DIM) for n in heads]
    bias = [bias_ref[hp * SB_HEADS_PER_STEP + n] * LOG2_E for n in heads]
    qb = [q_ref[:, lanes[n]].astype(BF16) for n in heads]
    suffix = _strict_suffix_matrix(blk)
    r = lax.broadcasted_iota(jnp.int32, (blk, blk), 0)
    cidx = lax.broadcasted_iota(jnp.int32, (blk, blk), 1)

    def step(jb, carry, mask):
        start = pl.multiple_of(jb * blk, blk)
        zs = [_dot_nt(qb[n], k_ref[pl.ds(start, blk), lanes[n]].astype(BF16)) * scale + bias[n] for n in heads]
        ws, cs = [], []
        for g in range(0, SB_HEADS_PER_STEP, SB_HEADS_PER_MATMUL):
            grp = range(g, g + SB_HEADS_PER_MATMUL)
            w_g, c_g = _sb_blocks([zs[n] for n in grp], [carry[n][0] for n in grp], suffix, mask, chained=False)
            ws += w_g
            cs += c_g
        return tuple((cs[n], carry[n][1] + _dot(ws[n].astype(BF16), v_ref[pl.ds(start, blk), lanes[n]].astype(BF16)))
                     for n in heads)

    zero = (jnp.zeros((blk, 1), F32), jnp.zeros((blk, HEAD_DIM), F32))
    carry = step(i, (zero,) * SB_HEADS_PER_STEP, cidx < r)
    carry = lax.fori_loop(0, i, lambda t, ca: step(i - 1 - t, ca, None), carry)
    for n in heads:
        o_ref[:, lanes[n]] = carry[n][1].astype(o_ref.dtype)


def _sb_prompt(proj, b_sb, l, blk=256):
    bsz, t, _ = proj.shape
    blk = min(blk, t)
    width = SB_HEADS_PER_STEP * HEAD_DIM
    groups = N_HEADS // SB_HEADS_PER_STEP
    kern = functools.partial(_sb_prompt_kernel, blk=blk)
    kv = lambda c0: pl.BlockSpec((None, t, width), lambda b, h, i: (b, 0, c0 * groups + h))
    return pl.pallas_call(
        kern,
        out_shape=jax.ShapeDtypeStruct((bsz, t, MIX_W), BF16),
        grid=(bsz, groups, t // blk),
        in_specs=[pl.BlockSpec(memory_space=pltpu.SMEM),
                  pl.BlockSpec((None, blk, width), lambda b, h, i: (b, i, COL_BQ * groups + h)),
                  kv(COL_BK), kv(COL_BV)],
        out_specs=pl.BlockSpec((None, blk, width), lambda b, h, i: (b, i, h)),
        compiler_params=_params(("parallel", "parallel", "arbitrary")),
        name="sb_prompt",
    )(b_sb[l], proj, proj, proj)


def _sb_sample_kernel(pt_ref, bias_ref, q_ref, kn_ref, vn_ref, suf_ref, *refs, pages_per_step):
    npg = pages_per_step
    k_refs, v_refs = refs[:npg], refs[npg:2 * npg]
    o_ref, c_scr, acc_scr = refs[2 * npg:]
    j = pl.program_id(1)
    tp = q_ref.shape[0]
    rows = N_HEADS * tp
    flat = k_refs[0].shape[0]
    scale = HEAD_DIM ** -0.5 * LOG2_E
    by_head = lambda ref: jnp.concatenate(
        [ref[:, h * HEAD_DIM:(h + 1) * HEAD_DIM] for h in range(N_HEADS)], axis=0)
    q_all = by_head(q_ref).astype(BF16)
    bias = jnp.concatenate([jnp.full((tp, 1), bias_ref[h] * LOG2_E, F32) for h in range(N_HEADS)], axis=0)

    def blocks(k_list, v_list, mask, c, acc):
        zs = [_dot_nt(q_all, kb) * scale + bias for kb in k_list]
        ws, cs = _sb_blocks(zs, [c], suf_ref[...], mask, chained=True)
        for a, vb in zip(ws, v_list):
            acc = acc + _dot(a.astype(BF16), vb)
        return cs[-1], acc

    @pl.when(j == 0)
    def _():
        r = lax.broadcasted_iota(jnp.int32, (rows, HEAD_DIM), 0)
        cidx = lax.broadcasted_iota(jnp.int32, (rows, HEAD_DIM), 1)
        mask = (cidx < rows) & (cidx // tp == r // tp) & (cidx % tp < r % tp)
        c, acc = blocks([_pad_rows(by_head(kn_ref), HEAD_DIM).astype(BF16)],
                        [_pad_rows(by_head(vn_ref), HEAD_DIM).astype(BF16)], mask,
                        jnp.zeros(c_scr.shape, F32), jnp.zeros(acc_scr.shape, F32))
        c_scr[...] = c
        acc_scr[...] = acc

    r = lax.broadcasted_iota(jnp.int32, (rows, flat), 0)
    cidx = lax.broadcasted_iota(jnp.int32, (rows, flat), 1)
    same_head = cidx % N_HEADS == r // tp
    c, acc = blocks([k_refs[p][...].astype(BF16) for p in range(npg)],
                    [v_refs[p][...].astype(BF16) for p in range(npg)], same_head, c_scr[...], acc_scr[...])
    c_scr[...] = c
    acc_scr[...] = acc

    @pl.when(j == pl.num_programs(1) - 1)
    def _():
        acc = acc_scr[...]
        for h in range(N_HEADS):
            o_ref[:, h * HEAD_DIM:(h + 1) * HEAD_DIM] = acc[h * tp:(h + 1) * tp].astype(o_ref.dtype)


def _sb_sample(proj, cache_k, cache_v, page_table, b_sb, l, pages_per_step=32):
    bsz, tp, _ = proj.shape
    n_pages = page_table.shape[1]
    depth, n_phys, page = cache_k.shape[:3]
    flat = page * N_HEADS
    npg = pages_per_step
    assert n_pages % npg == 0, "the page count must be a multiple of pages_per_step"
    steps = n_pages // npg
    cache_k = cache_k.reshape(depth, n_phys, flat, HEAD_DIM)
    cache_v = cache_v.reshape(depth, n_phys, flat, HEAD_DIM)
    suffix = jnp.tril(jnp.ones((HEAD_DIM, HEAD_DIM), BF16), -1)

    def page_spec(p):
        def imap(b, j, pt):
            return (l, pt[b * n_pages + (n_pages - 1 - (j * npg + p))], 0, 0)
        return pl.BlockSpec((None, None, flat, HEAD_DIM), imap)

    col = lambda c: pl.BlockSpec((None, tp, MIX_W), lambda b, j, pt: (b, 0, c))
    kern = functools.partial(_sb_sample_kernel, pages_per_step=npg)
    return pl.pallas_call(
        kern,
        out_shape=jax.ShapeDtypeStruct((bsz, tp, MIX_W), BF16),
        grid_spec=pltpu.PrefetchScalarGridSpec(
            num_scalar_prefetch=1,
            grid=(bsz, steps),
            in_specs=[pl.BlockSpec(memory_space=pltpu.SMEM), col(COL_BQ), col(COL_BK), col(COL_BV),
                      pl.BlockSpec((HEAD_DIM, HEAD_DIM), lambda b, j, pt: (0, 0))]
                     + [page_spec(p) for p in range(npg)] * 2,
            out_specs=pl.BlockSpec((None, tp, MIX_W), lambda b, j, pt: (b, 0, 0)),
            scratch_shapes=[pltpu.VMEM((N_HEADS * tp, 1), F32), pltpu.VMEM((N_HEADS * tp, HEAD_DIM), F32)]),
        compiler_params=_params(("parallel", "arbitrary"), V7X_VMEM_LIMIT),
        name="sb_sample",
    )(page_table.reshape(-1), b_sb[l], proj, proj, proj, suffix, *([cache_k] * npg), *([cache_v] * npg))


def _cmul(ar, ai, br, bi):
    return ar * br - ai * bi, ar * bi + ai * br


def _s5_prep_kernel(lre_ref, lim_ref, ls_ref, bre_ref, bim_ref, tab_ref, bbre_ref, bbim_ref):
    lam_re, lam_im = lre_ref[...], lim_ref[...]
    step = jnp.exp(ls_ref[...])
    decay = jnp.exp(lam_re * step)
    a_re = decay * jnp.cos(lam_im * step)
    a_im = decay * jnp.sin(lam_im * step)
    inv = 1.0 / (lam_re * lam_re + lam_im * lam_im)
    f_re = ((a_re - 1.0) * lam_re + a_im * lam_im) * inv
    f_im = (a_im * lam_re - (a_re - 1.0) * lam_im) * inv
    b_re, b_im = bre_ref[...], bim_ref[...]
    bbre_ref[...] = f_re * b_re - f_im * b_im
    bbim_ref[...] = f_re * b_im + f_im * b_re

    pw = {1: (a_re, a_im)}
    pw[2] = _cmul(*pw[1], *pw[1])
    pw[3] = _cmul(*pw[2], *pw[1])
    pw[4] = _cmul(*pw[2], *pw[2])
    pw[5] = _cmul(*pw[4], *pw[1])
    pw[6] = _cmul(*pw[4], *pw[2])
    pw[7] = _cmul(*pw[4], *pw[3])
    pw[8] = _cmul(*pw[4], *pw[4])
    row = lax.broadcasted_iota(jnp.int32, (SUBLANES, lam_re.shape[1]), 0)
    for part in range(2):
        carry = jnp.zeros(row.shape, F32)
        for r in range(SUBLANES):
            carry = jnp.where(row == r, pw[r + 1][part], carry)
        tab_ref[part] = carry
        for idx, k in enumerate((1, 2, 4)):
            tab_ref[2 + 2 * idx + part] = jnp.where(row >= k, pw[k][part], 0.0)


def _s5_prep(lam_re, lam_im, log_step, b_re, b_im):
    depth = lam_re.shape[0]
    flat = lambda a: a.reshape(depth, 1, S5_WIDTH)
    bt = lambda a: a.transpose(0, 3, 1, 2).reshape(depth, S5_GROUP_CH, S5_WIDTH)
    step = jnp.broadcast_to(log_step[:, :, None], (depth, S5_GROUPS, S5_STATE))
    vec = pl.BlockSpec((None, 1, S5_WIDTH), lambda l: (l, 0, 0))
    mat = pl.BlockSpec((None, S5_GROUP_CH, S5_WIDTH), lambda l: (l, 0, 0))
    return pl.pallas_call(
        _s5_prep_kernel,
        out_shape=(jax.ShapeDtypeStruct((depth, 8, SUBLANES, S5_WIDTH), F32),
                   jax.ShapeDtypeStruct((depth, S5_GROUP_CH, S5_WIDTH), F32),
                   jax.ShapeDtypeStruct((depth, S5_GROUP_CH, S5_WIDTH), F32)),
        grid=(depth,),
        in_specs=[vec, vec, vec, mat, mat],
        out_specs=(pl.BlockSpec((None, 8, SUBLANES, S5_WIDTH), lambda l: (l, 0, 0, 0)), mat, mat),
        compiler_params=_params(("parallel",)),
        name="s5_prep",
    )(flat(lam_re), flat(lam_im), flat(step), bt(b_re), bt(b_im))


def _s5_kernel(u_ref, wb_ref, tab_ref, hre_ref, him_ref, wcre_ref, wcim_ref, d_ref, wglu_ref,
               y_ref, sre_out, sim_out, sre, sim, cre, cim, *, valid_last, slab):
    ci = pl.program_id(1)
    tc = u_ref.shape[0]
    sub = MIX_W // S5_SUPER
    wid = S5_WIDTH // S5_SUPER

    @pl.when(ci == 0)
    def _():
        cre[...] = jnp.broadcast_to(hre_ref[...], cre.shape)
        cim[...] = jnp.broadcast_to(him_ref[...], cim.shape)

    u = u_ref[...]
    ub = u.astype(BF16)
    for g in range(S5_SUPER):
        bu = _dot(ub[:, g * sub:(g + 1) * sub], wb_ref[g])
        sre[:, g * wid:(g + 1) * wid] = bu[:, :wid]
        sim[:, g * wid:(g + 1) * wid] = bu[:, wid:]

    for s0 in range(0, S5_WIDTH, slab):
        lanes = slice(s0, s0 + slab)
        pr, pi = tab_ref[0, :, lanes], tab_ref[1, :, lanes]
        levels = [(k, tab_ref[2 + 2 * idx, :, lanes], tab_ref[3 + 2 * idx, :, lanes])
                  for idx, k in enumerate((1, 2, 4))]

        def body(r, carry, lanes=lanes, pr=pr, pi=pi, levels=levels):
            c_re, c_im = carry
            row = pl.multiple_of(r * SUBLANES, SUBLANES)
            xr = sre[pl.ds(row, SUBLANES), lanes]
            xi = sim[pl.ds(row, SUBLANES), lanes]
            for k, mr, mi in levels:
                rr = pltpu.roll(xr, k, axis=0)
                ri = pltpu.roll(xi, k, axis=0)
                xr, xi = xr + mr * rr - mi * ri, xi + mr * ri + mi * rr
            xr, xi = xr + pr * c_re - pi * c_im, xi + pr * c_im + pi * c_re
            sre[pl.ds(row, SUBLANES), lanes] = xr
            sim[pl.ds(row, SUBLANES), lanes] = xi
            last = SUBLANES - 1
            return (jnp.broadcast_to(xr[last:last + 1, :], xr.shape),
                    jnp.broadcast_to(xi[last:last + 1, :], xi.shape))

        nblk = tc // SUBLANES
        c_re, c_im = lax.fori_loop(0, nblk, body, (cre[:, lanes], cim[:, lanes]), unroll=min(2, nblk))
        cre[:, lanes] = c_re
        cim[:, lanes] = c_im

    s_re_b = sre[...].astype(BF16)
    s_im_b = sim[...].astype(BF16)
    y = jnp.concatenate(
        [_dot(s_re_b[:, g * wid:(g + 1) * wid], wcre_ref[g]) - _dot(s_im_b[:, g * wid:(g + 1) * wid], wcim_ref[g])
         for g in range(S5_SUPER)], axis=1)
    y = y + d_ref[...] * u
    y = 0.5 * y * (1.0 + jnp.tanh(math.sqrt(2.0 / math.pi) * (y + 0.044715 * (y * y * y))))
    yy = _dot(y.astype(BF16), wglu_ref[...])
    y_ref[...] = (yy[:, :MIX_W] * jax.nn.sigmoid(yy[:, MIX_W:])).astype(y_ref.dtype)

    @pl.when(ci == pl.num_programs(1) - 1)
    def _():
        sre_out[...] = sre[valid_last - 1:valid_last, :]
        sim_out[...] = sim[valid_last - 1:valid_last, :]


def _s5(proj, wb, tab, h_re, h_im, wc_re, wc_im, d_skip, w_glu, l, tc, valid_last, slab=512):
    bsz, t, _ = proj.shape
    tc = min(tc, t)
    layer = lambda s: pl.BlockSpec((None,) + s, lambda b, i: (l,) + (0,) * len(s))
    st = pl.BlockSpec((None, 1, S5_WIDTH), lambda b, i: (b, 0, 0))
    kern = functools.partial(_s5_kernel, valid_last=valid_last, slab=slab)
    wid = S5_WIDTH // S5_SUPER
    return pl.pallas_call(
        kern,
        out_shape=(jax.ShapeDtypeStruct((bsz, t, MIX_W), BF16),
                   jax.ShapeDtypeStruct((bsz, 1, S5_WIDTH), F32),
                   jax.ShapeDtypeStruct((bsz, 1, S5_WIDTH), F32)),
        grid=(bsz, t // tc),
        in_specs=[pl.BlockSpec((None, tc, MIX_W), lambda b, i: (b, i, COL_SU)),
                  layer((S5_SUPER, MIX_W // S5_SUPER, 2 * wid)),
                  layer((8, SUBLANES, S5_WIDTH)),
                  st, st,
                  layer((S5_SUPER, wid, MIX_W // S5_SUPER)), layer((S5_SUPER, wid, MIX_W // S5_SUPER)),
                  layer((1, MIX_W)), layer((MIX_W, 2 * MIX_W))],
        out_specs=(pl.BlockSpec((None, tc, MIX_W), lambda b, i: (b, i, 0)), st, st),
        scratch_shapes=[pltpu.VMEM((tc, S5_WIDTH), F32), pltpu.VMEM((tc, S5_WIDTH), F32),
                        pltpu.VMEM((SUBLANES, S5_WIDTH), F32), pltpu.VMEM((SUBLANES, S5_WIDTH), F32)],
        compiler_params=_params(("parallel", "arbitrary"), V7X_VMEM_LIMIT),
        name="s5",
    )(proj, wb, tab, h_re, h_im, wc_re, wc_im, d_skip, w_glu)


def _block_diag_groups(w):
    depth, _, a, b = w.shape
    per = S5_GROUPS // S5_SUPER
    w = w.reshape(depth, S5_SUPER, per, a, b)
    eye = jnp.eye(per, dtype=w.dtype)
    bd = w[:, :, :, :, None, :] * eye[None, None, :, None, :, None]
    return bd.reshape(depth, S5_SUPER, per * a, per * b)


def _conv_kernel(gb_ref, gc_ref, xv_ref, hc_ref, hx_ref, buf_ref, w_ref, y_ref, new_ref, *, valid_last):
    ci = pl.program_id(1)
    tc = gb_ref.shape[0]
    z = gc_ref[...] * xv_ref[...]
    prev = jnp.where(ci == 0, buf_ref[...], hc_ref[...] * hx_ref[...])
    zz = jnp.concatenate([prev, z], axis=0)
    w = w_ref[...]
    y = sum(w[j:j + 1, :] * zz[SUBLANES - (CONV_K - 1) + j:SUBLANES - (CONV_K - 1) + j + tc] for j in range(CONV_K))
    y_ref[...] = (gb_ref[...] * y).astype(y_ref.dtype)

    @pl.when(ci == pl.num_programs(1) - 1)
    def _():
        end = SUBLANES + valid_last
        new_ref[...] = zz[end - (CONV_K - 1):end]


def _conv(proj, buf8, w, l, tc, valid_last):
    bsz, t, _ = proj.shape
    tc = min(tc, t)
    per = tc // SUBLANES
    col = lambda c: pl.BlockSpec((None, tc, MIX_W), lambda b, i: (b, i, c))
    halo = lambda c: pl.BlockSpec((None, SUBLANES, MIX_W), lambda b, i: (b, jnp.maximum(i * per - 1, 0), c))
    kern = functools.partial(_conv_kernel, valid_last=valid_last)
    return pl.pallas_call(
        kern,
        out_shape=(jax.ShapeDtypeStruct((bsz, t, MIX_W), BF16),
                   jax.ShapeDtypeStruct((bsz, CONV_K - 1, MIX_W), F32)),
        grid=(bsz, t // tc),
        in_specs=[col(COL_CB), col(COL_CC), col(COL_CX), halo(COL_CC), halo(COL_CX),
                  pl.BlockSpec((None, SUBLANES, MIX_W), lambda b, i: (b, 0, 0)),
                  pl.BlockSpec((None, CONV_K, MIX_W), lambda b, i: (l, 0, 0))],
        out_specs=(pl.BlockSpec((None, tc, MIX_W), lambda b, i: (b, i, 0)),
                   pl.BlockSpec((None, CONV_K - 1, MIX_W), lambda b, i: (b, 0, 0))),
        compiler_params=_params(("parallel", "arbitrary")),
        name="short_conv",
    )(proj, proj, proj, proj, proj, buf8, w)


def _mixers(h, wts, l, past, sizes, kv_bufs):
    bsz, t, d = h.shape
    tm = sizes["tm"]
    flat = sizes["flat"]
    as_mm = lambda a: a.reshape(flat + a.shape[2:])
    as_seq = lambda a: a.reshape((bsz, t) + a.shape[2:])

    hm = as_mm(h)
    proj, gates, kv_bufs = _in_proj(hm, wts["w_in"], wts["w_if"], l, kv_bufs, sizes["tm_in"])
    proj, gates = as_seq(proj), as_seq(gates)
    gates_t = jnp.swapaxes(gates[:, :, :SUBLANES], 1, 2)
    if sizes["chunk"] > t:
        gates_t = jnp.pad(gates_t, ((0, 0), (0, 0), (0, sizes["chunk"] - t)))

    y_a, mc, mn, mm = _mlstm(proj, gates, gates_t, wts["gate_bias_c"], wts["gate_bias_r"], wts["g_head"], l,
                             past["mlstm_c"], past["mlstm_n"], past["mlstm_m"],
                             sizes["chunk"], sizes["rows"], sizes["valid"])
    if "sb_k" in past:
        y_b = _sb_sample(proj, past["sb_k"], past["sb_v"], past["page_table"], wts["b_sb"], l)
    else:
        y_b = _sb_prompt(proj, wts["b_sb"], l)
    y_s, s_re, s_im = _s5(proj, wts["s5_wb"], wts["s5_tab"], past["s5_re"], past["s5_im"],
                          wts["s5_wc_re"], wts["s5_wc_im"], wts["s5_d"], wts["w_s5_glu"], l,
                          sizes["tc"], sizes["valid_last"])
    y_c, conv_new = _conv(proj, past["conv"], wts["conv_w"], l, sizes["tc"], sizes["valid_last"])

    state = {"mlstm_c": mc, "mlstm_n": mn.reshape(bsz, N_HEADS, HEAD_DIM), "mlstm_m": mm.reshape(bsz, N_HEADS),
             "s5_re": s_re.reshape(bsz, S5_GROUPS, S5_STATE), "s5_im": s_im.reshape(bsz, S5_GROUPS, S5_STATE),
             "conv": conv_new}
    return hm, [as_mm(y) for y in (y_a, y_b, y_s, y_c)], state, kv_bufs


def _merge_and_ffn(xs, branches, mods, next_norms, wts, l, sizes):
    (hm_p, ys_p), (hm_s, ys_s) = branches
    merged = _gate_merge(hm_p, ys_p, hm_s, ys_s, wts["w_gate"], wts["b_gate"], wts["w_branch"], l,
                         sizes[0]["tm"], 256)
    x_mm, h2 = [], []
    for x, mod, size, mrg in zip(xs, mods, sizes, merged):
        as_mm = lambda a, size=size: a if a.shape[1] == 1 else a.reshape(size["flat"] + a.shape[2:])
        g1, sc2, sh2 = mod[2], mod[4], mod[3]
        xo, ho = _linear_residual_norm(mrg, wts["w_out"], l, x.reshape(size["flat"] + x.shape[2:]), as_mm(g1),
                                       (wts["g_norm2"], l, as_mm(sc2), as_mm(sh2), BF16), size["tm_res"], 1,
                                       "out_proj")
        x_mm.append(xo)
        h2.append(ho)
    hidden = _ffn_up(h2[0], h2[1], wts["w_ffn_gate"], wts["w_ffn_up"], l, sizes[0]["tm"], 512)
    g2_p, g2_s = mods[0][5], mods[1][5]
    x_new = _ffn_down(hidden[0], hidden[1], wts["w_ffn_down"], l, x_mm[0], g2_p, x_mm[1],
                      g2_s.reshape(sizes[1]["flat"] + g2_s.shape[2:]), 512, 512)
    out = []
    for x, xn, nrm in zip(xs, x_new, next_norms):
        xn = xn.reshape(x.shape)
        out.append((xn, _norm_mod(xn, *nrm)))
    return out


def kernel(x_prompt, x_sample, cache_sb_k, cache_sb_v, state_mlstm_c, state_mlstm_n, state_mlstm_m, state_s5_re, state_s5_im, state_conv, page_table, c_prompt, c_sample, w_ada, b_ada, g_norm1, g_norm2, w_in, b_mlstm_i, b_mlstm_f, g_mlstm_head, b_sb, s5_lambda_re, s5_lambda_im, s5_b_re, s5_b_im, s5_c_re, s5_c_im, s5_d, s5_log_step, w_s5_glu, conv_w, w_gate, b_gate, w_branch, w_out, w_ffn_gate, w_ffn_up, w_ffn_down, g_final):
    depth = w_in.shape[0]
    bp, tp, d = x_prompt.shape
    bs, ts, _ = x_sample.shape
    tpad = SUBLANES
    n_gate = N_GATE_COLS
    split = COL_BQ * MIX_W

    gate_bias = jnp.concatenate([b_mlstm_i, b_mlstm_f], axis=1)
    w_in16 = w_in.astype(BF16)
    wts = {
        "g_norm1": g_norm1.reshape(depth, 1, d), "g_norm2": g_norm2.reshape(depth, 1, d),
        "w_in": jnp.concatenate([w_in16[:, :, :split], w_in16[:, :, split + n_gate:]], axis=2),
        "w_if": jnp.pad(w_in16[:, :, split:split + n_gate], ((0, 0), (0, 0), (0, GATE_LANES - n_gate))),
        "gate_bias_c": jnp.pad(gate_bias, ((0, 0), (0, GATE_LANES - n_gate))).reshape(depth, 1, GATE_LANES),
        "gate_bias_r": gate_bias.reshape(depth, n_gate, 1),
        "g_head": g_mlstm_head.reshape(depth, 1, MIX_W),
        "b_sb": b_sb,
        "s5_d": s5_d.reshape(depth, 1, MIX_W),
        "w_s5_glu": w_s5_glu.astype(BF16),
        "conv_w": conv_w,
        "w_gate": w_gate, "b_gate": b_gate.reshape(depth, 1, N_BRANCH * d), "w_branch": w_branch,
        "w_out": w_out.astype(BF16),
        "w_ffn_gate": w_ffn_gate, "w_ffn_up": w_ffn_up,
        "w_ffn_down": w_ffn_down,
    }
    tc_p = 512
    wts["s5_tab"], bb_re, bb_im = _s5_prep(s5_lambda_re, s5_lambda_im, s5_log_step, s5_b_re, s5_b_im)
    to_gcp = lambda a: a.reshape(depth, S5_GROUP_CH, S5_GROUPS, S5_STATE).transpose(0, 2, 1, 3)
    wts["s5_wb"] = jnp.concatenate([_block_diag_groups(to_gcp(bb_re)), _block_diag_groups(to_gcp(bb_im))],
                                   axis=3).astype(BF16)
    wts["s5_wc_re"] = _block_diag_groups(s5_c_re.transpose(0, 1, 3, 2)).astype(BF16)
    wts["s5_wc_im"] = _block_diag_groups(s5_c_im.transpose(0, 1, 3, 2)).astype(BF16)

    c_all = jnp.concatenate([c_prompt, c_sample], axis=0)
    c_all = jnp.pad(c_all, ((0, -c_all.shape[0] % SUBLANES), (0, 0)))
    mod_all = _ada_all(c_all, w_ada, b_ada).reshape(depth, c_all.shape[0], 6, d)

    sizes_p = {"tm": 1024, "tm_in": 1024, "tm_res": 512, "flat": (bp, tp), "chunk": 256, "rows": 256, "valid": 256, "tc": tc_p,
               "valid_last": tc_p}
    sizes_s = {"tm": bs * tpad, "tm_in": bs * tpad, "tm_res": bs * tpad, "flat": (1, bs * tpad), "chunk": 128, "rows": tpad, "valid": ts,
               "tc": tpad, "valid_last": ts}

    xp = x_prompt
    xs = jnp.pad(x_sample, ((0, 0), (0, tpad - ts), (0, 0)))
    zeros_p = {
        "mlstm_c": jnp.zeros((bp, N_HEADS, HEAD_DIM, HEAD_DIM), F32),
        "mlstm_n": jnp.zeros((bp, N_HEADS, 1, HEAD_DIM), F32),
        "mlstm_m": jnp.zeros((bp, N_HEADS, 1, 1), F32),
        "s5_re": jnp.zeros((bp, 1, S5_WIDTH), F32), "s5_im": jnp.zeros((bp, 1, S5_WIDTH), F32),
        "conv": jnp.zeros((bp, SUBLANES, MIX_W), F32),
    }
    new_p, new_s = [], []
    kv_p = tuple(jnp.zeros((depth, bp, tp, MIX_W), F32) for _ in range(2))
    kv_s = tuple(jnp.zeros((depth, 1, bs * tpad, MIX_W), F32) for _ in range(2))
    mods_p = [[mod_all[l, :bp, i].reshape(bp, 1, d) for i in range(6)] for l in range(depth)]
    mods_s = [[jnp.repeat(mod_all[l, bp:bp + bs, i], tpad, axis=0).reshape(bs, tpad, d) for i in range(6)]
              for l in range(depth)]
    hp = _norm_mod(xp, wts["g_norm1"], 0, mods_p[0][1], mods_p[0][0])
    hs = _norm_mod(xs, wts["g_norm1"], 0, mods_s[0][1], mods_s[0][0])
    g_last = g_final.reshape(1, 1, d)

    def following_norm(mods, l, like):
        if l + 1 < depth:
            return wts["g_norm1"], l + 1, mods[l + 1][1], mods[l + 1][0], BF16
        return g_last, 0, jnp.zeros_like(like), jnp.zeros_like(like), F32

    for l in range(depth):
        mod_p, mod_s = mods_p[l], mods_s[l]
        past_s = {
            "sb_k": cache_sb_k, "sb_v": cache_sb_v, "page_table": page_table,
            "mlstm_c": state_mlstm_c[l], "mlstm_n": state_mlstm_n[l].reshape(bs, N_HEADS, 1, HEAD_DIM),
            "mlstm_m": state_mlstm_m[l].reshape(bs, N_HEADS, 1, 1),
            "s5_re": state_s5_re[l].reshape(bs, 1, S5_WIDTH), "s5_im": state_s5_im[l].reshape(bs, 1, S5_WIDTH),
            "conv": jnp.pad(state_conv[l], ((0, 0), (SUBLANES - (CONV_K - 1), 0), (0, 0))),
        }
        hm_p, ys_p, st_p, kv_p = _mixers(hp, wts, l, zeros_p, sizes_p, kv_p)
        hm_s, ys_s, st_s, kv_s = _mixers(hs, wts, l, past_s, sizes_s, kv_s)
        new_p.append(st_p)
        new_s.append(st_s)
        (xp, hp), (xs, hs) = _merge_and_ffn(
            (xp, xs), ((hm_p, ys_p), (hm_s, ys_s)), (mod_p, mod_s),
            (following_norm(mods_p, l, mod_p[0]), following_norm(mods_s, l, mod_s[0])), wts, l, (sizes_p, sizes_s))

    y_prompt = hp
    y_sample = hs[:, :ts]
    stk = lambda states, name: jnp.stack([s[name] for s in states])
    heads_p = lambda a: a.reshape(depth, bp, tp, N_HEADS, HEAD_DIM)
    heads_s = lambda a: a.reshape(depth, bs, tpad, N_HEADS, HEAD_DIM)[:, :, :ts]
    return (y_prompt, y_sample,
            heads_p(kv_p[0]), heads_p(kv_p[1]), heads_s(kv_s[0]), heads_s(kv_s[1]),
            stk(new_p, "mlstm_c"), stk(new_p, "mlstm_n"), stk(new_p, "mlstm_m"),
            stk(new_s, "mlstm_c"), stk(new_s, "mlstm_n"), stk(new_s, "mlstm_m"),
            stk(new_p, "s5_re"), stk(new_p, "s5_im"), stk(new_s, "s5_re"), stk(new_s, "s5_im"),
            stk(new_p, "conv"), stk(new_s, "conv"))
```
